```python
import math
import jax, jax.numpy as jnp
from jax import lax
import numpy as np

D_MODEL = 2048
BATCH = 4
SEQ = 2048
DEPTH = 2
DEC_BATCH = 128
DEC_SEQ = 8
PAST_LEN = 16384
PAGE_SIZE = 128

CONV_WIDTH = D_MODEL // 4
POOL_WIDTH = D_MODEL // 4
RET_WIDTH = D_MODEL // 2
CONV_K = 3
POOL_WINDOWS = (2, 4, 8, 16)
N_POOL_GROUPS = len(POOL_WINDOWS)
POOL_GROUP = POOL_WIDTH // N_POOL_GROUPS
POOL_HIST = max(POOL_WINDOWS) - 1
RET_HEADS = 4
RET_HEAD_DIM = RET_WIDTH // RET_HEADS
RET_CHUNK = 128
RET_LOG_GAMMA = tuple(math.log(1.0 - 2.0 ** (-5 - h)) for h in range(RET_HEADS))
ROPE_BASE = 10000.0
D_FF = 11 * D_MODEL // 4
PLE_DIM = 256
MIX_WIDTH = CONV_WIDTH + POOL_WIDTH + RET_WIDTH
IN_COLS = 3 * CONV_WIDTH + POOL_WIDTH + 4 * RET_WIDTH
IN_SPLITS = (CONV_WIDTH, 2 * CONV_WIDTH, 3 * CONV_WIDTH,
             3 * CONV_WIDTH + POOL_WIDTH,
             3 * CONV_WIDTH + POOL_WIDTH + RET_WIDTH,
             3 * CONV_WIDTH + POOL_WIDTH + 2 * RET_WIDTH,
             3 * CONV_WIDTH + POOL_WIDTH + 3 * RET_WIDTH)
DEEPNORM_ALPHA = (2 * DEPTH) ** 0.25
DEEPNORM_BETA = (8 * DEPTH) ** -0.25
LN_EPS = 1e-5

kernel_name = "hybrid_conv_pool_retention_decoder_step"


def layer_norm(x, g, b):
    xf = x.astype(jnp.float32)
    mu = jnp.mean(xf, axis=-1, keepdims=True)
    var = jnp.mean(jnp.square(xf - mu), axis=-1, keepdims=True)
    y = (xf - mu) * lax.rsqrt(var + LN_EPS) * g.astype(jnp.float32) + b.astype(jnp.float32)
    return y.astype(x.dtype)


def swiglu(x, w_gate, w_up, w_down):
    return (jax.nn.silu(x @ w_gate) * (x @ w_up)) @ w_down


def rotary(x, pos):
    half = x.shape[-1] // 2
    inv = ROPE_BASE ** (-jnp.arange(half, dtype=jnp.float32) / half)
    ang = pos[:, None] * inv[None, :]
    cos = jnp.cos(ang)[None, :, None, :]
    sin = jnp.sin(ang)[None, :, None, :]
    x1, x2 = x[..., :half], x[..., half:]
    return jnp.concatenate([x1 * cos - x2 * sin, x1 * sin + x2 * cos], axis=-1)


def short_conv_group(z_b, z_c, z_h, prefix, conv_w):
    u = z_c * z_h
    full = jnp.concatenate([prefix.astype(u.dtype), u], axis=1)
    L = u.shape[1]
    conv = sum(full[:, j:j + L] * conv_w[j] for j in range(CONV_K))
    return z_b * conv, full[:, -(CONV_K - 1):]


def pool_group(u, prefix, pool_w, pool_scale, start_pos):
    Bn, L, C = u.shape
    full = jnp.concatenate([prefix.astype(u.dtype), u], axis=1)
    ff = full.astype(jnp.float32)
    cs = jnp.concatenate([jnp.zeros((Bn, 1, C), jnp.float32), jnp.cumsum(ff, axis=1)], axis=1)
    pos = start_pos + jnp.arange(L, dtype=jnp.float32)
    o = POOL_HIST + 1
    means = []
    for gi, w in enumerate(POOL_WINDOWS):
        sl = slice(gi * POOL_GROUP, (gi + 1) * POOL_GROUP)
        win_sum = cs[:, o:o + L, sl] - cs[:, o - w:o - w + L, sl]
        cnt = jnp.minimum(jnp.float32(w), pos + 1.0)[None, :, None]
        means.append(win_sum / cnt)
    d = (jnp.concatenate(means, axis=-1) - u.astype(jnp.float32)).reshape(Bn, L, N_POOL_GROUPS, POOL_GROUP)
    y = jnp.einsum('blgc,gcd->blgd', d, pool_w.astype(jnp.float32)).reshape(Bn, L, POOL_WIDTH)
    y = y * pool_scale.astype(jnp.float32)
    return y.astype(u.dtype), full[:, -POOL_HIST:]


def retention_chunk(S, q, k, v, lg):
    L = q.shape[1]
    idx = jnp.arange(L, dtype=jnp.float32)
    diff = idx[:, None] - idx[None, :]
    causal = diff >= 0
    decay = jnp.where(causal[None], jnp.exp(jnp.where(causal, diff, 0.0)[None] * lg[:, None, None]), 0.0)
    scores = jnp.einsum('blhd,bmhd->bhlm', q, k) * decay[None]
    inner = jnp.einsum('bhlm,bmhe->blhe', scores, v)
    q_decay = jnp.exp((idx[:, None] + 1.0) * lg[None, :])
    cross = jnp.einsum('blhd,bhde->blhe', q, S) * q_decay[None, :, :, None]
    k_decay = jnp.exp((L - 1.0 - idx)[:, None] * lg[None, :])
    S_new = (jnp.exp(L * lg)[None, :, None, None] * S
             + jnp.einsum('blhd,blhe->bhde', k * k_decay[None, :, :, None], v))
    return S_new, inner + cross


def retention_group(zq, zk, zv, zg, S0, start_pos):
    Bn, L, _ = zq.shape
    shp = (Bn, L, RET_HEADS, RET_HEAD_DIM)
    pos = start_pos + jnp.arange(L, dtype=jnp.float32)
    q = rotary(zq.astype(jnp.float32).reshape(shp), pos)
    k = rotary(zk.astype(jnp.float32).reshape(shp), pos) * (RET_HEAD_DIM ** -0.5)
    v = zv.astype(jnp.float32).reshape(shp)
    lg = jnp.array(RET_LOG_GAMMA, dtype=jnp.float32)
    S0 = S0.astype(jnp.float32)
    c = min(L, RET_CHUNK)
    n = L // c
    if n == 1:
        S, o = retention_chunk(S0, q, k, v, lg)
    else:
        def to_chunks(t):
            return jnp.moveaxis(t.reshape(Bn, n, c, RET_HEADS, RET_HEAD_DIM), 1, 0)

        def step(S, xs):
            qc, kc, vc = xs
            return retention_chunk(S, qc, kc, vc, lg)

        S, oc = lax.scan(step, S0, (to_chunks(q), to_chunks(k), to_chunks(v)))
        o = jnp.moveaxis(oc, 0, 1).reshape(shp)
    mu = jnp.mean(o, axis=-1, keepdims=True)
    var = jnp.mean(jnp.square(o - mu), axis=-1, keepdims=True)
    on = ((o - mu) * lax.rsqrt(var + LN_EPS)).reshape(Bn, L, RET_WIDTH)
    y = on * jax.nn.silu(zg.astype(jnp.float32))
    return y.astype(zq.dtype), S


def mixing_sublayer(x, w_in, conv_w, pool_w, pool_scale, w_out, conv_prefix, pool_prefix, ret_state, start_pos):
    z = x @ w_in
    z_b, z_c, z_h, z_p, zq, zk, zv, zg = jnp.split(z, list(IN_SPLITS), axis=-1)
    ya, conv_new = short_conv_group(z_b, z_c, z_h, conv_prefix, conv_w)
    yb, pool_new = pool_group(z_p, pool_prefix, pool_w, pool_scale, start_pos)
    yc, ret_new = retention_group(zq, zk, zv, zg, ret_state, start_pos)
    m = jnp.concatenate([ya, yb, yc], axis=-1) @ w_out
    return m, conv_new, pool_new, ret_new


def trunk(x, p, conv_cache, pool_cache, ret_cache, start_pos,
          ln1_g, ln1_b, ffn1_w_gate, ffn1_w_up, ffn1_w_down, w_in, conv_w, pool_w, pool_scale,
          w_out, ln2_g, ln2_b, ffn2_w_gate, ffn2_w_up, ffn2_w_down, ple_gate, ple_proj, ln3_g, ln3_b):
    convs, pools, rets = [], [], []
    for i in range(DEPTH):
        h = 0.5 * swiglu(x, ffn1_w_gate[i], ffn1_w_up[i], ffn1_w_down[i])
        x = layer_norm(DEEPNORM_ALPHA * x + h, ln1_g[i], ln1_b[i])
        m, c_new, p_new, r_new = mixing_sublayer(x, w_in[i], conv_w[i], pool_w[i], pool_scale[i], w_out[i],
                                                 conv_cache[i], pool_cache[i], ret_cache[i], start_pos)
        x = layer_norm(DEEPNORM_ALPHA * x + m, ln2_g[i], ln2_b[i])
        h = (0.5 * swiglu(x, ffn2_w_gate[i], ffn2_w_up[i], ffn2_w_down[i])
             + jax.nn.sigmoid(x @ ple_gate[i]) * (p[i] @ ple_proj[i]))
        x = layer_norm(DEEPNORM_ALPHA * x + h, ln3_g[i], ln3_b[i])
        convs.append(c_new)
        pools.append(p_new)
        rets.append(r_new)
    return x, jnp.stack(convs), jnp.stack(pools), jnp.stack(rets)


def setup_inputs(seed: int = 0) -> dict:
    key = jax.random.key(seed)
    ks = jax.random.split(key, 32)
    f32 = jnp.float32

    def nrm(k, shape, scale=1.0):
        return jax.random.normal(k, shape, f32) * scale

    D, F = D_MODEL, D_FF
    return {
        "x_prompt": nrm(ks[0], (BATCH, SEQ, D)),
        "x_sample": nrm(ks[1], (DEC_BATCH, DEC_SEQ, D)),
        "p_prompt": nrm(ks[2], (DEPTH, BATCH, SEQ, PLE_DIM)),
        "p_sample": nrm(ks[3], (DEPTH, DEC_BATCH, DEC_SEQ, PLE_DIM)),
        "cache_conv": nrm(ks[4], (DEPTH, DEC_BATCH, CONV_K - 1, CONV_WIDTH)),
        "cache_pool": nrm(ks[5], (DEPTH, DEC_BATCH, POOL_HIST, POOL_WIDTH)),
        "state_ret": nrm(ks[6], (DEPTH, DEC_BATCH, RET_HEADS, RET_HEAD_DIM, RET_HEAD_DIM), RET_HEAD_DIM ** -0.5),
        "ln1_g": 1.0 + nrm(ks[7], (DEPTH, D), 0.02),
        "ln1_b": nrm(ks[8], (DEPTH, D), 0.02),
        "ffn1_w_gate": nrm(ks[9], (DEPTH, D, F), D ** -0.5),
        "ffn1_w_up": nrm(ks[10], (DEPTH, D, F), D ** -0.5),
        "ffn1_w_down": nrm(ks[11], (DEPTH, F, D), F ** -0.5 * DEEPNORM_BETA),
        "w_in": nrm(ks[12], (DEPTH, D, IN_COLS), D ** -0.5),
        "conv_w": nrm(ks[13], (DEPTH, CONV_K, CONV_WIDTH), CONV_K ** -0.5),
        "pool_w": nrm(ks[14], (DEPTH, N_POOL_GROUPS, POOL_GROUP, POOL_GROUP), POOL_GROUP ** -0.5),
        "pool_scale": 1.0 + nrm(ks[15], (DEPTH, POOL_WIDTH), 0.1),
        "w_out": nrm(ks[16], (DEPTH, MIX_WIDTH, D), MIX_WIDTH ** -0.5 * DEEPNORM_BETA),
        "ln2_g": 1.0 + nrm(ks[17], (DEPTH, D), 0.02),
        "ln2_b": nrm(ks[18], (DEPTH, D), 0.02),
        "ffn2_w_gate": nrm(ks[19], (DEPTH, D, F), D ** -0.5),
        "ffn2_w_up": nrm(ks[20], (DEPTH, D, F), D ** -0.5),
        "ffn2_w_down": nrm(ks[21], (DEPTH, F, D), F ** -0.5 * DEEPNORM_BETA),
        "ple_gate": nrm(ks[22], (DEPTH, D, D), D ** -0.5),
        "ple_proj": nrm(ks[23], (DEPTH, PLE_DIM, D), PLE_DIM ** -0.5 * DEEPNORM_BETA),
        "ln3_g": 1.0 + nrm(ks[24], (DEPTH, D), 0.02),
        "ln3_b": nrm(ks[25], (DEPTH, D), 0.02),
    }


def reference(x_prompt, x_sample, p_prompt, p_sample, cache_conv, cache_pool, state_ret,
              ln1_g, ln1_b, ffn1_w_gate, ffn1_w_up, ffn1_w_down, w_in, conv_w, pool_w, pool_scale,
              w_out, ln2_g, ln2_b, ffn2_w_gate, ffn2_w_up, ffn2_w_down, ple_gate, ple_proj, ln3_g, ln3_b):
    Bp = x_prompt.shape[0]
    zero_conv = jnp.zeros((DEPTH, Bp, CONV_K - 1, CONV_WIDTH), x_prompt.dtype)
    zero_pool = jnp.zeros((DEPTH, Bp, POOL_HIST, POOL_WIDTH), x_prompt.dtype)
    zero_ret = jnp.zeros((DEPTH, Bp, RET_HEADS, RET_HEAD_DIM, RET_HEAD_DIM), jnp.float32)
    y_prompt, conv_p, pool_p, ret_p = trunk(
        x_prompt, p_prompt, zero_conv, zero_pool, zero_ret, 0,
        ln1_g, ln1_b, ffn1_w_gate, ffn1_w_up, ffn1_w_down, w_in, conv_w, pool_w, pool_scale,
        w_out, ln2_g, ln2_b, ffn2_w_gate, ffn2_w_up, ffn2_w_down, ple_gate, ple_proj, ln3_g, ln3_b)
    y_sample, conv_s, pool_s, ret_s = trunk(
        x_sample, p_sample, cache_conv, cache_pool, state_ret, PAST_LEN,
        ln1_g, ln1_b, ffn1_w_gate, ffn1_w_up, ffn1_w_down, w_in, conv_w, pool_w, pool_scale,
        w_out, ln2_g, ln2_b, ffn2_w_gate, ffn2_w_up, ffn2_w_down, ple_gate, ple_proj, ln3_g, ln3_b)
    return (y_prompt, y_sample, conv_p, pool_p, ret_p, conv_s, pool_s, ret_s)
```

```python
import functools
import math

import jax
import jax.numpy as jnp
from jax import lax
from jax.experimental import pallas as pl
from jax.experimental.pallas import tpu as pltpu

D_MODEL = 2048
DEPTH = 2
PAST_LEN = 16384
CONV_WIDTH = D_MODEL // 4
POOL_WIDTH = D_MODEL // 4
RET_WIDTH = D_MODEL // 2
CONV_K = 3
POOL_WINDOWS = (2, 4, 8, 16)
POOL_GROUP = POOL_WIDTH // len(POOL_WINDOWS)
POOL_HIST = max(POOL_WINDOWS) - 1
RET_HEADS = 4
RET_HEAD_DIM = RET_WIDTH // RET_HEADS
RET_CHUNK = 128
RET_LOG_GAMMA = tuple(math.log(1.0 - 2.0 ** (-5 - h)) for h in range(RET_HEADS))
ROPE_BASE = 10000.0
IN_COLS = 3 * CONV_WIDTH + POOL_WIDTH + 4 * RET_WIDTH
DEEPNORM_ALPHA = (2 * DEPTH) ** 0.25
LN_EPS = 1e-5

COL_B = 0
COL_C = CONV_WIDTH
COL_H = 2 * CONV_WIDTH
COL_P = 3 * CONV_WIDTH
COL_Q = COL_P + POOL_WIDTH
COL_K = COL_Q + RET_WIDTH
COL_V = COL_K + RET_WIDTH
COL_G = COL_V + RET_WIDTH
OUT_POOL = CONV_WIDTH
OUT_RET = CONV_WIDTH + POOL_WIDTH

SUBLANES = 8
CONV_PAD = SUBLANES
POOL_PAD = 16

F32 = jnp.float32
BF16 = jnp.bfloat16
MIB = 1024 * 1024


def _params(semantics, vmem_mib):
    return pltpu.CompilerParams(dimension_semantics=semantics, vmem_limit_bytes=vmem_mib * MIB)


def _layer_norm(r, g, b):
    mu = jnp.mean(r, axis=-1, keepdims=True)
    c = r - mu
    var = jnp.mean(c * c, axis=-1, keepdims=True)
    return c * lax.rsqrt(var + LN_EPS) * g + b


def _silu(x):
    return x * jax.nn.sigmoid(x)


def _dot(a, b):
    return jnp.dot(a, b, preferred_element_type=F32)


def _ffn_ln_kernel(*refs, nf, has_extra, emit_bf16):
    x_ref, wg_ref, wu_ref, wd_ref, g_ref, b_ref = refs[:6]
    rest = list(refs[6:])
    e_ref = rest.pop(0) if has_extra else None
    o_ref = rest.pop(0)
    ob_ref = rest.pop(0) if emit_bf16 else None
    xb_ref = rest.pop(0)
    f = pl.program_id(1)

    @pl.when(f == 0)
    def _():
        xb_ref[...] = x_ref[...].astype(BF16)

    xb = xb_ref[...]
    h = (_silu(_dot(xb, wg_ref[...])) * _dot(xb, wu_ref[...])).astype(BF16)
    part = _dot(h, wd_ref[...])

    @pl.when(f == 0)
    def _():
        o_ref[...] = part

    @pl.when(f > 0)
    def _():
        o_ref[...] += part

    @pl.when(f == nf - 1)
    def _():
        r = DEEPNORM_ALPHA * x_ref[...] + 0.5 * o_ref[...]
        if has_extra:
            r = r + e_ref[...]
        y = _layer_norm(r, g_ref[...], b_ref[...])
        o_ref[...] = y
        if emit_bf16:
            ob_ref[...] = y.astype(BF16)


def _ffn_ln(x, wg, wu, wd, ln_g, ln_b, extra=None, emit_bf16=False, tm=512, tf=512):
    T, D = x.shape
    F = wg.shape[1]
    nf = F // tf
    row = lambda i, f: (i, 0)
    const = lambda i, f: (0, 0)
    in_specs = [
        pl.BlockSpec((tm, D), row),
        pl.BlockSpec((D, tf), lambda i, f: (0, f)),
        pl.BlockSpec((D, tf), lambda i, f: (0, f)),
        pl.BlockSpec((tf, D), lambda i, f: (f, 0)),
        pl.BlockSpec((1, D), const),
        pl.BlockSpec((1, D), const),
    ]
    args = [x, wg, wu, wd, ln_g, ln_b]
    if extra is not None:
        in_specs.append(pl.BlockSpec((tm, D), row))
        args.append(extra)
    out_shape = [jax.ShapeDtypeStruct((T, D), F32)]
    out_specs = [pl.BlockSpec((tm, D), row)]
    if emit_bf16:
        out_shape.append(jax.ShapeDtypeStruct((T, D), BF16))
        out_specs.append(pl.BlockSpec((tm, D), row))
    outs = pl.pallas_call(
        functools.partial(_ffn_ln_kernel, nf=nf, has_extra=extra is not None, emit_bf16=emit_bf16),
        grid=(T // tm, nf),
        in_specs=in_specs,
        out_specs=out_specs,
        out_shape=out_shape,
        scratch_shapes=[pltpu.VMEM((tm, D), BF16)],
        compiler_params=_params(("parallel", "arbitrary"), 56),
        name="ffn_ln",
    )(*args)
    return outs if emit_bf16 else outs[0]


def _proj_in_kernel(x_ref, w_ref, o_ref):
    o_ref[...] = _dot(x_ref[...], w_ref[...])


def _proj_in(xb, w, tm=1024, tn=512):
    T, D = xb.shape
    N = w.shape[1]
    return pl.pallas_call(
        _proj_in_kernel,
        grid=(T // tm, N // tn),
        in_specs=[pl.BlockSpec((tm, D), lambda i, n: (i, 0)), pl.BlockSpec((D, tn), lambda i, n: (0, n))],
        out_specs=pl.BlockSpec((tm, tn), lambda i, n: (i, n)),
        out_shape=jax.ShapeDtypeStruct((T, N), F32),
        compiler_params=_params(("parallel", "arbitrary"), 40),
        name="proj_in",
    )(xb, w)


def _rotary(x, cos, sin):
    half = RET_HEAD_DIM // 2
    x1, x2 = x[:, :half], x[:, half:]
    return jnp.concatenate([x1 * cos - x2 * sin, x1 * sin + x2 * cos], axis=-1)


def _retention_tables(L):
    row = lax.broadcasted_iota(jnp.int32, (L, L), 0)
    col = lax.broadcasted_iota(jnp.int32, (L, L), 1)
    diff = (row - col).astype(F32)
    causal = row >= col
    idx = lax.broadcasted_iota(jnp.int32, (L, 1), 0).astype(F32)
    tables = []
    for lg in RET_LOG_GAMMA:
        decay = jnp.where(causal, jnp.exp(jnp.where(causal, diff, 0.0) * lg), 0.0)
        q_decay = jnp.exp((idx + 1.0) * lg)
        k_decay = jnp.exp((L - 1.0 - idx) * lg)
        tables.append((decay, q_decay, k_decay, math.exp(L * lg)))
    return tables


def _retention_head(q, k, v, gate, S, cos, sin, table):
    decay, q_decay, k_decay, s_decay = table
    qr = _rotary(q, cos, sin)
    kr = _rotary(k, cos, sin) * (RET_HEAD_DIM ** -0.5)
    qb = qr.astype(BF16)
    vb = v.astype(BF16)
    scores = lax.dot_general(qb, kr.astype(BF16), (((1,), (1,)), ((), ())), preferred_element_type=F32)
    inner = _dot((scores * decay).astype(BF16), vb)
    cross = _dot(qb, S.astype(BF16)) * q_decay
    o = inner + cross
    kd = (kr * k_decay).astype(BF16)
    S_new = s_decay * S + lax.dot_general(kd, vb, (((0,), (0,)), ((), ())), preferred_element_type=F32)
    mu = jnp.mean(o, axis=-1, keepdims=True)
    c = o - mu
    var = jnp.mean(c * c, axis=-1, keepdims=True)
    y = c * lax.rsqrt(var + LN_EPS) * _silu(gate)
    return y, S_new


def _window_count(w, pos):
    return jnp.minimum(jnp.float32(w), pos + 1.0)


def _mixer_prompt_kernel(z_ref, cos_ref, sin_ref, convw_ref, poolw_ref, pscale_ref,
                         y_ref, convn_ref, pooln_ref, retn_ref,
                         s_ref, ubuf_ref, pbuf_ref, *, tl, nl):
    l = pl.program_id(1)

    @pl.when(l == 0)
    def _():
        s_ref[...] = jnp.zeros_like(s_ref)
        ubuf_ref[0:CONV_PAD, :] = jnp.zeros((CONV_PAD, CONV_WIDTH), F32)
        pbuf_ref[0:POOL_PAD, :] = jnp.zeros((POOL_PAD, POOL_WIDTH), F32)

    u = z_ref[:, COL_C:COL_C + CONV_WIDTH] * z_ref[:, COL_H:COL_H + CONV_WIDTH]
    ubuf_ref[CONV_PAD:CONV_PAD + tl, :] = u
    cw = convw_ref[...]
    conv = (ubuf_ref[CONV_PAD - 2:CONV_PAD - 2 + tl, :] * cw[0:1, :]
            + ubuf_ref[CONV_PAD - 1:CONV_PAD - 1 + tl, :] * cw[1:2, :]
            + u * cw[2:3, :])
    y_ref[:, 0:CONV_WIDTH] = (z_ref[:, COL_B:COL_B + CONV_WIDTH] * conv).astype(y_ref.dtype)

    pbuf_ref[POOL_PAD:POOL_PAD + tl, :] = z_ref[:, COL_P:COL_P + POOL_WIDTH]
    pos = (l * tl + lax.broadcasted_iota(jnp.int32, (tl, 1), 0)).astype(F32)
    for gi, w in enumerate(POOL_WINDOWS):
        c0 = gi * POOL_GROUP
        tok = pbuf_ref[POOL_PAD:POOL_PAD + tl, c0:c0 + POOL_GROUP]
        win = tok
        for j in range(1, w):
            win = win + pbuf_ref[POOL_PAD - j:POOL_PAD - j + tl, c0:c0 + POOL_GROUP]
        d = win / _window_count(w, pos) - tok
        yb = _dot(d.astype(BF16), poolw_ref[gi]) * pscale_ref[:, c0:c0 + POOL_GROUP]
        y_ref[:, OUT_POOL + c0:OUT_POOL + c0 + POOL_GROUP] = yb.astype(y_ref.dtype)

    tables = _retention_tables(RET_CHUNK)
    for c in range(tl // RET_CHUNK):
        r0 = c * RET_CHUNK
        cos = cos_ref[r0:r0 + RET_CHUNK, :]
        sin = sin_ref[r0:r0 + RET_CHUNK, :]
        for h in range(RET_HEADS):
            h0 = h * RET_HEAD_DIM
            y, s_new = _retention_head(
                z_ref[r0:r0 + RET_CHUNK, COL_Q + h0:COL_Q + h0 + RET_HEAD_DIM],
                z_ref[r0:r0 + RET_CHUNK, COL_K + h0:COL_K + h0 + RET_HEAD_DIM],
                z_ref[r0:r0 + RET_CHUNK, COL_V + h0:COL_V + h0 + RET_HEAD_DIM],
                z_ref[r0:r0 + RET_CHUNK, COL_G + h0:COL_G + h0 + RET_HEAD_DIM],
                s_ref[h], cos, sin, tables[h])
            s_ref[h] = s_new
            y_ref[r0:r0 + RET_CHUNK, OUT_RET + h0:OUT_RET + h0 + RET_HEAD_DIM] = y.astype(y_ref.dtype)

    @pl.when(l == nl - 1)
    def _():
        convn_ref[0] = ubuf_ref[CONV_PAD + tl - (CONV_K - 1):CONV_PAD + tl, :]
        pooln_ref[0] = pbuf_ref[POOL_PAD + tl - POOL_HIST:POOL_PAD + tl, :]
        retn_ref[0] = s_ref[...]

    ubuf_ref[0:CONV_PAD, :] = ubuf_ref[tl:tl + CONV_PAD, :]
    pbuf_ref[0:POOL_PAD, :] = pbuf_ref[tl:tl + POOL_PAD, :]


def _mixer_prompt(z, cos, sin, conv_w, pool_w, pool_scale, batch, tl=256):
    T = z.shape[0]
    L = T // batch
    nl = L // tl
    row = lambda b, l: (b * nl + l, 0)
    const2 = lambda b, l: (0, 0)
    return pl.pallas_call(
        functools.partial(_mixer_prompt_kernel, tl=tl, nl=nl),
        grid=(batch, nl),
        in_specs=[
            pl.BlockSpec((tl, IN_COLS), row),
            pl.BlockSpec((tl, RET_HEAD_DIM // 2), lambda b, l: (l, 0)),
            pl.BlockSpec((tl, RET_HEAD_DIM // 2), lambda b, l: (l, 0)),
            pl.BlockSpec((CONV_K, CONV_WIDTH), const2),
            pl.BlockSpec((len(POOL_WINDOWS), POOL_GROUP, POOL_GROUP), lambda b, l: (0, 0, 0)),
            pl.BlockSpec((1, POOL_WIDTH), const2),
        ],
        out_specs=[
            pl.BlockSpec((tl, D_MODEL), row),
            pl.BlockSpec((1, CONV_K - 1, CONV_WIDTH), lambda b, l: (b, 0, 0)),
            pl.BlockSpec((1, POOL_HIST, POOL_WIDTH), lambda b, l: (b, 0, 0)),
            pl.BlockSpec((1, RET_HEADS, RET_HEAD_DIM, RET_HEAD_DIM), lambda b, l: (b, 0, 0, 0)),
        ],
        out_shape=[
            jax.ShapeDtypeStruct((T, D_MODEL), BF16),
            jax.ShapeDtypeStruct((batch, CONV_K - 1, CONV_WIDTH), F32),
            jax.ShapeDtypeStruct((batch, POOL_HIST, POOL_WIDTH), F32),
            jax.ShapeDtypeStruct((batch, RET_HEADS, RET_HEAD_DIM, RET_HEAD_DIM), F32),
        ],
        scratch_shapes=[
            pltpu.VMEM((RET_HEADS, RET_HEAD_DIM, RET_HEAD_DIM), F32),
            pltpu.VMEM((CONV_PAD + tl, CONV_WIDTH), F32),
            pltpu.VMEM((POOL_PAD + tl, POOL_WIDTH), F32),
        ],
        compiler_params=_params(("parallel", "arbitrary"), 48),
        name="mixer_prompt",
    )(z, cos, sin, conv_w, pool_w, pool_scale)


def _mixer_sample_kernel(z_ref, cos_ref, sin_ref, convw_ref, poolw_ref, pscale_ref,
                         convc_ref, poolc_ref, state_ref,
                         y_ref, convn_ref, pooln_ref, staten_ref,
                         ubuf_ref, pbuf_ref, *, nb, ls, start_pos):
    u = z_ref[:, :, COL_C:COL_C + CONV_WIDTH] * z_ref[:, :, COL_H:COL_H + CONV_WIDTH]
    ubuf_ref[:, CONV_PAD - (CONV_K - 1):CONV_PAD, :] = convc_ref[...]
    ubuf_ref[:, CONV_PAD:CONV_PAD + ls, :] = u
    cw = convw_ref[...]
    conv = (ubuf_ref[:, CONV_PAD - 2:CONV_PAD - 2 + ls, :] * cw[0:1, :]
            + ubuf_ref[:, CONV_PAD - 1:CONV_PAD - 1 + ls, :] * cw[1:2, :]
            + u * cw[2:3, :])
    y_ref[:, :, 0:CONV_WIDTH] = z_ref[:, :, COL_B:COL_B + CONV_WIDTH] * conv
    convn_ref[...] = ubuf_ref[:, CONV_PAD + ls - (CONV_K - 1):CONV_PAD + ls, :]

    pbuf_ref[:, POOL_PAD - POOL_HIST:POOL_PAD, :] = poolc_ref[...]
    pbuf_ref[:, POOL_PAD:POOL_PAD + ls, :] = z_ref[:, :, COL_P:COL_P + POOL_WIDTH]
    pos = (start_pos + lax.broadcasted_iota(jnp.int32, (1, ls, 1), 1)).astype(F32)
    for gi, w in enumerate(POOL_WINDOWS):
        c0 = gi * POOL_GROUP
        tok = pbuf_ref[:, POOL_PAD:POOL_PAD + ls, c0:c0 + POOL_GROUP]
        win = tok
        for j in range(1, w):
            win = win + pbuf_ref[:, POOL_PAD - j:POOL_PAD - j + ls, c0:c0 + POOL_GROUP]
        d = win / _window_count(w, pos) - tok
        yb = _dot(d.reshape(nb * ls, POOL_GROUP).astype(BF16), poolw_ref[gi]).reshape(nb, ls, POOL_GROUP)
        y_ref[:, :, OUT_POOL + c0:OUT_POOL + c0 + POOL_GROUP] = yb * pscale_ref[:, c0:c0 + POOL_GROUP]
    pooln_ref[...] = pbuf_ref[:, POOL_PAD + ls - POOL_HIST:POOL_PAD + ls, :]

    tables = _retention_tables(ls)
    cos = cos_ref[...]
    sin = sin_ref[...]

    def per_sequence(b, carry):
        for h in range(RET_HEADS):
            h0 = h * RET_HEAD_DIM
            y, s_new = _retention_head(
                z_ref[b, :, COL_Q + h0:COL_Q + h0 + RET_HEAD_DIM],
                z_ref[b, :, COL_K + h0:COL_K + h0 + RET_HEAD_DIM],
                z_ref[b, :, COL_V + h0:COL_V + h0 + RET_HEAD_DIM],
                z_ref[b, :, COL_G + h0:COL_G + h0 + RET_HEAD_DIM],
                state_ref[b, h], cos, sin, tables[h])
            staten_ref[b, h] = s_new
            y_ref[b, :, OUT_RET + h0:OUT_RET + h0 + RET_HEAD_DIM] = y
        return carry

    lax.fori_loop(0, nb, per_sequence, 0)


def _mixer_sample(z, cos, sin, conv_w, pool_w, pool_scale, conv_cache, pool_cache, state, start_pos, nb=8):
    B, ls, _ = z.shape
    seq3 = lambda i: (i, 0, 0)
    const2 = lambda i: (0, 0)
    return pl.pallas_call(
        functools.partial(_mixer_sample_kernel, nb=nb, ls=ls, start_pos=start_pos),
        grid=(B // nb,),
        in_specs=[
            pl.BlockSpec((nb, ls, IN_COLS), seq3),
            pl.BlockSpec((ls, RET_HEAD_DIM // 2), const2),
            pl.BlockSpec((ls, RET_HEAD_DIM // 2), const2),
            pl.BlockSpec((CONV_K, CONV_WIDTH), const2),
            pl.BlockSpec((len(POOL_WINDOWS), POOL_GROUP, POOL_GROUP), lambda i: (0, 0, 0)),
            pl.BlockSpec((1, POOL_WIDTH), const2),
            pl.BlockSpec((nb, CONV_K - 1, CONV_WIDTH), seq3),
            pl.BlockSpec((nb, POOL_HIST, POOL_WIDTH), seq3),
            pl.BlockSpec((nb, RET_HEADS, RET_HEAD_DIM, RET_HEAD_DIM), lambda i: (i, 0, 0, 0)),
        ],
        out_specs=[
            pl.BlockSpec((nb, ls, D_MODEL), seq3),
            pl.BlockSpec((nb, CONV_K - 1, CONV_WIDTH), seq3),
            pl.BlockSpec((nb, POOL_HIST, POOL_WIDTH), seq3),
            pl.BlockSpec((nb, RET_HEADS, RET_HEAD_DIM, RET_HEAD_DIM), lambda i: (i, 0, 0, 0)),
        ],
        out_shape=[
            jax.ShapeDtypeStruct((B, ls, D_MODEL), F32),
            jax.ShapeDtypeStruct((B, CONV_K - 1, CONV_WIDTH), F32),
            jax.ShapeDtypeStruct((B, POOL_HIST, POOL_WIDTH), F32),
            jax.ShapeDtypeStruct((B, RET_HEADS, RET_HEAD_DIM, RET_HEAD_DIM), F32),
        ],
        scratch_shapes=[
            pltpu.VMEM((nb, CONV_PAD + ls, CONV_WIDTH), F32),
            pltpu.VMEM((nb, POOL_PAD + ls, POOL_WIDTH), F32),
        ],
        compiler_params=_params(("parallel",), 48),
        name="mixer_sample",
    )(z, cos, sin, conv_w, pool_w, pool_scale, conv_cache, pool_cache, state)


def _out_ln_kernel(y_ref, x_ref, w_ref, g_ref, b_ref, o_ref, ob_ref):
    m = _dot(y_ref[...].astype(BF16), w_ref[...])
    x2 = _layer_norm(DEEPNORM_ALPHA * x_ref[...] + m, g_ref[...], b_ref[...])
    o_ref[...] = x2
    ob_ref[...] = x2.astype(BF16)


def _out_ln(y, x, w, ln_g, ln_b, tm=256):
    T, D = x.shape
    row = lambda i: (i, 0)
    const = lambda i: (0, 0)
    return pl.pallas_call(
        _out_ln_kernel,
        grid=(T // tm,),
        in_specs=[
            pl.BlockSpec((tm, y.shape[1]), row),
            pl.BlockSpec((tm, D), row),
            pl.BlockSpec(w.shape, const),
            pl.BlockSpec((1, D), const),
            pl.BlockSpec((1, D), const),
        ],
        out_specs=[pl.BlockSpec((tm, D), row), pl.BlockSpec((tm, D), row)],
        out_shape=[jax.ShapeDtypeStruct((T, D), F32), jax.ShapeDtypeStruct((T, D), BF16)],
        compiler_params=_params(("parallel",), 48),
        name="out_ln",
    )(y, x, w, ln_g, ln_b)


def _ple_kernel(x_ref, gate_ref, p_ref, proj_ref, o_ref):
    gate = jax.nn.sigmoid(_dot(x_ref[...], gate_ref[...]))
    o_ref[...] = gate * _dot(p_ref[...].astype(BF16), proj_ref[...])


def _ple(xb, gate_w, p, proj_w, tm=512, tn=1024):
    T, D = xb.shape
    N = gate_w.shape[1]
    P = p.shape[1]
    return pl.pallas_call(
        _ple_kernel,
        grid=(N // tn, T // tm),
        in_specs=[
            pl.BlockSpec((tm, D), lambda n, i: (i, 0)),
            pl.BlockSpec((D, tn), lambda n, i: (0, n)),
            pl.BlockSpec((tm, P), lambda n, i: (i, 0)),
            pl.BlockSpec((P, tn), lambda n, i: (0, n)),
        ],
        out_specs=pl.BlockSpec((tm, tn), lambda n, i: (i, n)),
        out_shape=jax.ShapeDtypeStruct((T, N), F32),
        compiler_params=_params(("parallel", "arbitrary"), 40),
        name="ple",
    )(xb, gate_w, p, proj_w)


def _rope_tables(start_pos, length):
    half = RET_HEAD_DIM // 2
    inv = ROPE_BASE ** (-jnp.arange(half, dtype=F32) / half)
    pos = start_pos + jnp.arange(length, dtype=F32)
    ang = pos[:, None] * inv[None, :]
    return jnp.cos(ang), jnp.sin(ang)


def _trunk(x, p, caches, start_pos, weights):
    B, L, D = x.shape
    T = B * L
    x = x.reshape(T, D)
    cos, sin = _rope_tables(start_pos, L)
    convs, pools, rets = [], [], []
    for i in range(DEPTH):
        w = weights[i]
        x1, x1b = _ffn_ln(x, w["ffn1_w_gate"], w["ffn1_w_up"], w["ffn1_w_down"], w["ln1_g"], w["ln1_b"],
                          emit_bf16=True)
        z = _proj_in(x1b, w["w_in"])
        if caches is None:
            y, c_new, p_new, r_new = _mixer_prompt(z, cos, sin, w["conv_w"], w["pool_w"], w["pool_scale"], B)
        else:
            y, c_new, p_new, r_new = _mixer_sample(
                z.reshape(B, L, IN_COLS), cos, sin, w["conv_w"], w["pool_w"], w["pool_scale"],
                caches[0][i], caches[1][i], caches[2][i], start_pos)
            y = y.reshape(T, D)
        x2, x2b = _out_ln(y, x1, w["w_out"], w["ln2_g"], w["ln2_b"])
        e = _ple(x2b, w["ple_gate"], p[i].reshape(T, -1), w["ple_proj"])
        x = _ffn_ln(x2, w["ffn2_w_gate"], w["ffn2_w_up"], w["ffn2_w_down"], w["ln3_g"], w["ln3_b"], extra=e)
        convs.append(c_new)
        pools.append(p_new)
        rets.append(r_new)
    return x.reshape(B, L, D), jnp.stack(convs), jnp.stack(pools), jnp.stack(rets)


def kernel(x_prompt, x_sample, p_prompt, p_sample, cache_conv, cache_pool, state_ret, ln1_g, ln1_b, ffn1_w_gate, ffn1_w_up, ffn1_w_down, w_in, conv_w, pool_w, pool_scale, w_out, ln2_g, ln2_b, ffn2_w_gate, ffn2_w_up, ffn2_w_down, ple_gate, ple_proj, ln3_g, ln3_b):
    matmul_weights = dict(ffn1_w_gate=ffn1_w_gate, ffn1_w_up=ffn1_w_up, ffn1_w_down=ffn1_w_down, w_in=w_in,
                          pool_w=pool_w, w_out=w_out, ffn2_w_gate=ffn2_w_gate, ffn2_w_up=ffn2_w_up,
                          ffn2_w_down=ffn2_w_down, ple_gate=ple_gate, ple_proj=ple_proj)
    row_params = dict(ln1_g=ln1_g, ln1_b=ln1_b, ln2_g=ln2_g, ln2_b=ln2_b, ln3_g=ln3_g, ln3_b=ln3_b,
                      pool_scale=pool_scale)
    weights = []
    for i in range(DEPTH):
        w = {k: v[i].astype(BF16) for k, v in matmul_weights.items()}
        w.update({k: v[i].reshape(1, -1) for k, v in row_params.items()})
        w["conv_w"] = conv_w[i]
        weights.append(w)
    y_prompt, conv_p, pool_p, ret_p = _trunk(x_prompt, p_prompt, None, 0, weights)
    y_sample, conv_s, pool_s, ret_s = _trunk(
        x_sample, p_sample, (cache_conv, cache_pool, state_ret), PAST_LEN, weights)
    return (y_prompt, y_sample, conv_p, pool_p, ret_p, conv_s, pool_s, ret_s)
```

```python
import functools
import math

import jax
import jax.numpy as jnp
from jax import lax
from jax.experimental import pallas as pl
from jax.experimental.pallas import tpu as pltpu

D_MODEL = 2048
DEPTH = 2
PAST_LEN = 16384
CONV_WIDTH = D_MODEL // 4
POOL_WIDTH = D_MODEL // 4
RET_WIDTH = D_MODEL // 2
CONV_K = 3
POOL_WINDOWS = (2, 4, 8, 16)
POOL_GROUP = POOL_WIDTH // len(POOL_WINDOWS)
POOL_HIST = max(POOL_WINDOWS) - 1
RET_HEADS = 4
RET_HEAD_DIM = RET_WIDTH // RET_HEADS
RET_CHUNK = 128
RET_LOG_GAMMA = tuple(math.log(1.0 - 2.0 ** (-5 - h)) for h in range(RET_HEADS))
ROPE_BASE = 10000.0
IN_COLS = 3 * CONV_WIDTH + POOL_WIDTH + 4 * RET_WIDTH
DEEPNORM_ALPHA = (2 * DEPTH) ** 0.25
LN_EPS = 1e-5

COL_B = 0
COL_C = CONV_WIDTH
COL_H = 2 * CONV_WIDTH
COL_P = 3 * CONV_WIDTH
COL_Q = COL_P + POOL_WIDTH
COL_K = COL_Q + RET_WIDTH
COL_V = COL_K + RET_WIDTH
COL_G = COL_V + RET_WIDTH
OUT_POOL = CONV_WIDTH
OUT_RET = CONV_WIDTH + POOL_WIDTH

SUBLANES = 8
CONV_PAD = SUBLANES
POOL_PAD = 16

F32 = jnp.float32
BF16 = jnp.bfloat16
MIB = 1024 * 1024


def _params(semantics, vmem_mib):
    return pltpu.CompilerParams(dimension_semantics=semantics, vmem_limit_bytes=vmem_mib * MIB)


def _layer_norm(r, g, b):
    mu = jnp.mean(r, axis=-1, keepdims=True)
    c = r - mu
    var = jnp.mean(c * c, axis=-1, keepdims=True)
    return c * lax.rsqrt(var + LN_EPS) * g + b


def _silu(x):
    return x * jax.nn.sigmoid(x)


def _dot(a, b):
    return jnp.dot(a, b, preferred_element_type=F32)


def _cast_kernel(x_ref, o_ref):
    o_ref[...] = x_ref[...].astype(o_ref.dtype)


def _to_bf16(w, block_bytes=6 * MIB):
    shape = w.shape
    w = w.reshape(shape[0], -1, shape[-1])
    _, R, C = w.shape
    tr = R
    while tr * C * 4 > block_bytes and tr % 2 == 0 and (tr // 2) % 16 == 0:
        tr //= 2
    out = pl.pallas_call(
        _cast_kernel,
        grid=(shape[0], R // tr),
        in_specs=[pl.BlockSpec((None, tr, C), lambda d, r: (d, r, 0))],
        out_specs=pl.BlockSpec((None, tr, C), lambda d, r: (d, r, 0)),
        out_shape=jax.ShapeDtypeStruct(w.shape, BF16),
        compiler_params=_params(("parallel", "parallel"), 40),
        name="to_bf16",
    )(w)
    return out.reshape(shape)


def _ffn_ln_kernel(*refs, nf, has_extra, emit_bf16):
    x_ref, wg_ref, wu_ref, wd_ref, g_ref, b_ref = refs[:6]
    rest = list(refs[6:])
    e_ref = rest.pop(0) if has_extra else None
    o_ref = rest.pop(0)
    ob_ref = rest.pop(0) if emit_bf16 else None
    xb_ref = rest.pop(0)
    f = pl.program_id(1)

    @pl.when(f == 0)
    def _():
        xb_ref[...] = x_ref[...].astype(BF16)
        o_ref[...] = jnp.zeros_like(o_ref)

    xb = xb_ref[...]
    h = (_silu(_dot(xb, wg_ref[...])) * _dot(xb, wu_ref[...])).astype(BF16)
    o_ref[...] += _dot(h, wd_ref[...])

    @pl.when(f == nf - 1)
    def _():
        r = DEEPNORM_ALPHA * x_ref[...] + 0.5 * o_ref[...]
        if has_extra:
            r = r + e_ref[...]
        y = _layer_norm(r, g_ref[...], b_ref[...])
        o_ref[...] = y
        if emit_bf16:
            ob_ref[...] = y.astype(BF16)


def _ffn_ln(x, wg, wu, wd, ln_g, ln_b, layer, extra=None, emit_bf16=False, tm=512, tf=512):
    T, D = x.shape
    F = wg.shape[2]
    nf = F // tf
    row = lambda i, f: (i, 0)
    const = lambda i, f: (layer, 0, 0)
    in_specs = [
        pl.BlockSpec((tm, D), row),
        pl.BlockSpec((None, D, tf), lambda i, f: (layer, 0, f)),
        pl.BlockSpec((None, D, tf), lambda i, f: (layer, 0, f)),
        pl.BlockSpec((None, tf, D), lambda i, f: (layer, f, 0)),
        pl.BlockSpec((None, 1, D), const),
        pl.BlockSpec((None, 1, D), const),
    ]
    args = [x, wg, wu, wd, ln_g, ln_b]
    if extra is not None:
        in_specs.append(pl.BlockSpec((tm, D), row))
        args.append(extra)
    out_shape = [jax.ShapeDtypeStruct((T, D), F32)]
    out_specs = [pl.BlockSpec((tm, D), row)]
    if emit_bf16:
        out_shape.append(jax.ShapeDtypeStruct((T, D), BF16))
        out_specs.append(pl.BlockSpec((tm, D), row))
    outs = pl.pallas_call(
        functools.partial(_ffn_ln_kernel, nf=nf, has_extra=extra is not None, emit_bf16=emit_bf16),
        grid=(T // tm, nf),
        in_specs=in_specs,
        out_specs=out_specs,
        out_shape=out_shape,
        scratch_shapes=[pltpu.VMEM((tm, D), BF16)],
        compiler_params=_params(("parallel", "arbitrary"), 56),
        name="ffn_ln",
    )(*args)
    return outs if emit_bf16 else outs[0]


def _proj_in_kernel(x_ref, w_ref, o_ref):
    o_ref[...] = _dot(x_ref[...], w_ref[...])


def _proj_in(xb, w, layer, tm=1024, tn=512):
    T, D = xb.shape
    N = w.shape[2]
    return pl.pallas_call(
        _proj_in_kernel,
        grid=(T // tm, N // tn),
        in_specs=[pl.BlockSpec((tm, D), lambda i, n: (i, 0)),
                  pl.BlockSpec((None, D, tn), lambda i, n: (layer, 0, n))],
        out_specs=pl.BlockSpec((tm, tn), lambda i, n: (i, n)),
        out_shape=jax.ShapeDtypeStruct((T, N), F32),
        compiler_params=_params(("parallel", "arbitrary"), 40),
        name="proj_in",
    )(xb, w)


def _rotary(x, cos, sin):
    half = RET_HEAD_DIM // 2
    x1, x2 = x[:, :half], x[:, half:]
    return jnp.concatenate([x1 * cos - x2 * sin, x1 * sin + x2 * cos], axis=-1)


def _retention_tables(L):
    row = lax.broadcasted_iota(jnp.int32, (L, L), 0)
    col = lax.broadcasted_iota(jnp.int32, (L, L), 1)
    diff = (row - col).astype(F32)
    causal = row >= col
    idx = lax.broadcasted_iota(jnp.int32, (L, 1), 0).astype(F32)
    tables = []
    for lg in RET_LOG_GAMMA:
        decay = jnp.where(causal, jnp.exp(jnp.where(causal, diff, 0.0) * lg), 0.0)
        q_decay = jnp.exp((idx + 1.0) * lg)
        k_decay = jnp.exp((L - 1.0 - idx) * lg)
        tables.append((decay, q_decay, k_decay, math.exp(L * lg)))
    return tables


def _retention_head(q, k, v, gate, S, cos, sin, table):
    decay, q_decay, k_decay, s_decay = table
    qr = _rotary(q, cos, sin)
    kr = _rotary(k, cos, sin) * (RET_HEAD_DIM ** -0.5)
    qb = qr.astype(BF16)
    vb = v.astype(BF16)
    scores = lax.dot_general(qb, kr.astype(BF16), (((1,), (1,)), ((), ())), preferred_element_type=F32)
    inner = _dot((scores * decay).astype(BF16), vb)
    cross = _dot(qb, S.astype(BF16)) * q_decay
    o = inner + cross
    kd = (kr * k_decay).astype(BF16)
    S_new = s_decay * S + lax.dot_general(kd, vb, (((0,), (0,)), ((), ())), preferred_element_type=F32)
    mu = jnp.mean(o, axis=-1, keepdims=True)
    c = o - mu
    var = jnp.mean(c * c, axis=-1, keepdims=True)
    y = c * lax.rsqrt(var + LN_EPS) * _silu(gate)
    return y, S_new


def _window_count(w, pos):
    return jnp.minimum(jnp.float32(w), pos + 1.0)


def _mixer_prompt_kernel(z_ref, cos_ref, sin_ref, convw_ref, poolw_ref, pscale_ref,
                         y_ref, convn_ref, pooln_ref, retn_ref,
                         s_ref, ubuf_ref, pbuf_ref, *, tl, nl):
    l = pl.program_id(1)

    @pl.when(l == 0)
    def _():
        s_ref[...] = jnp.zeros_like(s_ref)
        ubuf_ref[0:CONV_PAD, :] = jnp.zeros((CONV_PAD, CONV_WIDTH), F32)
        pbuf_ref[0:POOL_PAD, :] = jnp.zeros((POOL_PAD, POOL_WIDTH), F32)

    u = z_ref[:, COL_C:COL_C + CONV_WIDTH] * z_ref[:, COL_H:COL_H + CONV_WIDTH]
    ubuf_ref[CONV_PAD:CONV_PAD + tl, :] = u
    cw = convw_ref[...]
    conv = (ubuf_ref[CONV_PAD - 2:CONV_PAD - 2 + tl, :] * cw[0:1, :]
            + ubuf_ref[CONV_PAD - 1:CONV_PAD - 1 + tl, :] * cw[1:2, :]
            + u * cw[2:3, :])
    y_ref[:, 0:CONV_WIDTH] = (z_ref[:, COL_B:COL_B + CONV_WIDTH] * conv).astype(y_ref.dtype)

    pbuf_ref[POOL_PAD:POOL_PAD + tl, :] = z_ref[:, COL_P:COL_P + POOL_WIDTH]
    pos = (l * tl + lax.broadcasted_iota(jnp.int32, (tl, 1), 0)).astype(F32)
    for gi, w in enumerate(POOL_WINDOWS):
        c0 = gi * POOL_GROUP
        tok = pbuf_ref[POOL_PAD:POOL_PAD + tl, c0:c0 + POOL_GROUP]
        win = tok
        for j in range(1, w):
            win = win + pbuf_ref[POOL_PAD - j:POOL_PAD - j + tl, c0:c0 + POOL_GROUP]
        d = win / _window_count(w, pos) - tok
        yb = _dot(d.astype(BF16), poolw_ref[gi]) * pscale_ref[:, c0:c0 + POOL_GROUP]
        y_ref[:, OUT_POOL + c0:OUT_POOL + c0 + POOL_GROUP] = yb.astype(y_ref.dtype)

    tables = _retention_tables(RET_CHUNK)
    for c in range(tl // RET_CHUNK):
        r0 = c * RET_CHUNK
        cos = cos_ref[r0:r0 + RET_CHUNK, :]
        sin = sin_ref[r0:r0 + RET_CHUNK, :]
        for h in range(RET_HEADS):
            h0 = h * RET_HEAD_DIM
            y, s_new = _retention_head(
                z_ref[r0:r0 + RET_CHUNK, COL_Q + h0:COL_Q + h0 + RET_HEAD_DIM],
                z_ref[r0:r0 + RET_CHUNK, COL_K + h0:COL_K + h0 + RET_HEAD_DIM],
                z_ref[r0:r0 + RET_CHUNK, COL_V + h0:COL_V + h0 + RET_HEAD_DIM],
                z_ref[r0:r0 + RET_CHUNK, COL_G + h0:COL_G + h0 + RET_HEAD_DIM],
                s_ref[h], cos, sin, tables[h])
            s_ref[h] = s_new
            y_ref[r0:r0 + RET_CHUNK, OUT_RET + h0:OUT_RET + h0 + RET_HEAD_DIM] = y.astype(y_ref.dtype)

    @pl.when(l == nl - 1)
    def _():
        convn_ref[0] = ubuf_ref[CONV_PAD + tl - (CONV_K - 1):CONV_PAD + tl, :]
        pooln_ref[0] = pbuf_ref[POOL_PAD + tl - POOL_HIST:POOL_PAD + tl, :]
        retn_ref[0] = s_ref[...]

    ubuf_ref[0:CONV_PAD, :] = ubuf_ref[tl:tl + CONV_PAD, :]
    pbuf_ref[0:POOL_PAD, :] = pbuf_ref[tl:tl + POOL_PAD, :]


def _mixer_prompt(z, cos, sin, conv_w, pool_w, pool_scale, layer, batch, tl=256):
    T = z.shape[0]
    L = T // batch
    nl = L // tl
    row = lambda b, l: (b * nl + l, 0)
    return pl.pallas_call(
        functools.partial(_mixer_prompt_kernel, tl=tl, nl=nl),
        grid=(batch, nl),
        in_specs=[
            pl.BlockSpec((tl, IN_COLS), row),
            pl.BlockSpec((tl, RET_HEAD_DIM // 2), lambda b, l: (l, 0)),
            pl.BlockSpec((tl, RET_HEAD_DIM // 2), lambda b, l: (l, 0)),
            pl.BlockSpec((None, CONV_K, CONV_WIDTH), lambda b, l: (layer, 0, 0)),
            pl.BlockSpec((None, len(POOL_WINDOWS), POOL_GROUP, POOL_GROUP), lambda b, l: (layer, 0, 0, 0)),
            pl.BlockSpec((None, 1, POOL_WIDTH), lambda b, l: (layer, 0, 0)),
        ],
        out_specs=[
            pl.BlockSpec((tl, D_MODEL), row),
            pl.BlockSpec((1, CONV_K - 1, CONV_WIDTH), lambda b, l: (b, 0, 0)),
            pl.BlockSpec((1, POOL_HIST, POOL_WIDTH), lambda b, l: (b, 0, 0)),
            pl.BlockSpec((1, RET_HEADS, RET_HEAD_DIM, RET_HEAD_DIM), lambda b, l: (b, 0, 0, 0)),
        ],
        out_shape=[
            jax.ShapeDtypeStruct((T, D_MODEL), BF16),
            jax.ShapeDtypeStruct((batch, CONV_K - 1, CONV_WIDTH), F32),
            jax.ShapeDtypeStruct((batch, POOL_HIST, POOL_WIDTH), F32),
            jax.ShapeDtypeStruct((batch, RET_HEADS, RET_HEAD_DIM, RET_HEAD_DIM), F32),
        ],
        scratch_shapes=[
            pltpu.VMEM((RET_HEADS, RET_HEAD_DIM, RET_HEAD_DIM), F32),
            pltpu.VMEM((CONV_PAD + tl, CONV_WIDTH), F32),
            pltpu.VMEM((POOL_PAD + tl, POOL_WIDTH), F32),
        ],
        compiler_params=_params(("parallel", "arbitrary"), 48),
        name="mixer_prompt",
    )(z, cos, sin, conv_w, pool_w, pool_scale)


def _mixer_sample_kernel(z_ref, cos_ref, sin_ref, convw_ref, poolw_ref, pscale_ref,
                         convc_ref, poolc_ref, state_ref, *rest, nb, ls, start_pos, aliased):
    if aliased:
        rest = rest[1:]
    y_ref, convn_ref, pooln_ref, staten_ref, ubuf_ref, pbuf_ref = rest
    u = z_ref[:, :, COL_C:COL_C + CONV_WIDTH] * z_ref[:, :, COL_H:COL_H + CONV_WIDTH]
    ubuf_ref[:, CONV_PAD - (CONV_K - 1):CONV_PAD, :] = convc_ref[...]
    ubuf_ref[:, CONV_PAD:CONV_PAD + ls, :] = u
    cw = convw_ref[...]
    conv = (ubuf_ref[:, CONV_PAD - 2:CONV_PAD - 2 + ls, :] * cw[0:1, :]
            + ubuf_ref[:, CONV_PAD - 1:CONV_PAD - 1 + ls, :] * cw[1:2, :]
            + u * cw[2:3, :])
    y_ref[:, :, 0:CONV_WIDTH] = z_ref[:, :, COL_B:COL_B + CONV_WIDTH] * conv
    convn_ref[...] = ubuf_ref[:, CONV_PAD + ls - (CONV_K - 1):CONV_PAD + ls, :]

    pbuf_ref[:, POOL_PAD - POOL_HIST:POOL_PAD, :] = poolc_ref[...]
    pbuf_ref[:, POOL_PAD:POOL_PAD + ls, :] = z_ref[:, :, COL_P:COL_P + POOL_WIDTH]
    pos = (start_pos + lax.broadcasted_iota(jnp.int32, (1, ls, 1), 1)).astype(F32)
    for gi, w in enumerate(POOL_WINDOWS):
        c0 = gi * POOL_GROUP
        tok = pbuf_ref[:, POOL_PAD:POOL_PAD + ls, c0:c0 + POOL_GROUP]
        win = tok
        for j in range(1, w):
            win = win + pbuf_ref[:, POOL_PAD - j:POOL_PAD - j + ls, c0:c0 + POOL_GROUP]
        d = win / _window_count(w, pos) - tok
        yb = _dot(d.reshape(nb * ls, POOL_GROUP).astype(BF16), poolw_ref[gi]).reshape(nb, ls, POOL_GROUP)
        y_ref[:, :, OUT_POOL + c0:OUT_POOL + c0 + POOL_GROUP] = yb * pscale_ref[:, c0:c0 + POOL_GROUP]
    pooln_ref[...] = pbuf_ref[:, POOL_PAD + ls - POOL_HIST:POOL_PAD + ls, :]

    tables = _retention_tables(ls)
    cos = cos_ref[...]
    sin = sin_ref[...]

    def per_sequence(b, carry):
        for h in range(RET_HEADS):
            h0 = h * RET_HEAD_DIM
            y, s_new = _retention_head(
                z_ref[b, :, COL_Q + h0:COL_Q + h0 + RET_HEAD_DIM],
                z_ref[b, :, COL_K + h0:COL_K + h0 + RET_HEAD_DIM],
                z_ref[b, :, COL_V + h0:COL_V + h0 + RET_HEAD_DIM],
                z_ref[b, :, COL_G + h0:COL_G + h0 + RET_HEAD_DIM],
                state_ref[b, h], cos, sin, tables[h])
            staten_ref[b, h] = s_new
            y_ref[b, :, OUT_RET + h0:OUT_RET + h0 + RET_HEAD_DIM] = y
        return carry

    lax.fori_loop(0, nb, per_sequence, 0)


def _mixer_sample(z, cos, sin, conv_w, pool_w, pool_scale, conv_cache, pool_cache, state, state_out,
                  layer, start_pos, nb=8):
    B, ls, _ = z.shape
    seq3 = lambda i: (i, 0, 0)
    lay3 = lambda i: (layer, 0, 0)
    lseq4 = lambda i: (layer, i, 0, 0)
    lseq5 = lambda i: (layer, i, 0, 0, 0)
    in_specs = [
        pl.BlockSpec((nb, ls, IN_COLS), seq3),
        pl.BlockSpec((ls, RET_HEAD_DIM // 2), lambda i: (0, 0)),
        pl.BlockSpec((ls, RET_HEAD_DIM // 2), lambda i: (0, 0)),
        pl.BlockSpec((None, CONV_K, CONV_WIDTH), lay3),
        pl.BlockSpec((None, len(POOL_WINDOWS), POOL_GROUP, POOL_GROUP), lambda i: (layer, 0, 0, 0)),
        pl.BlockSpec((None, 1, POOL_WIDTH), lay3),
        pl.BlockSpec((None, nb, CONV_K - 1, CONV_WIDTH), lseq4),
        pl.BlockSpec((None, nb, POOL_HIST, POOL_WIDTH), lseq4),
        pl.BlockSpec((None, nb, RET_HEADS, RET_HEAD_DIM, RET_HEAD_DIM), lseq5),
    ]
    args = [z, cos, sin, conv_w, pool_w, pool_scale, conv_cache, pool_cache, state]
    aliases = {}
    if state_out is not None:
        in_specs.append(pl.BlockSpec(memory_space=pl.ANY))
        args.append(state_out)
        aliases = {len(args) - 1: 3}
    return pl.pallas_call(
        functools.partial(_mixer_sample_kernel, nb=nb, ls=ls, start_pos=start_pos,
                          aliased=state_out is not None),
        grid=(B // nb,),
        in_specs=in_specs,
        out_specs=[
            pl.BlockSpec((nb, ls, D_MODEL), seq3),
            pl.BlockSpec((nb, CONV_K - 1, CONV_WIDTH), seq3),
            pl.BlockSpec((nb, POOL_HIST, POOL_WIDTH), seq3),
            pl.BlockSpec((None, nb, RET_HEADS, RET_HEAD_DIM, RET_HEAD_DIM), lseq5),
        ],
        out_shape=[
            jax.ShapeDtypeStruct((B, ls, D_MODEL), F32),
            jax.ShapeDtypeStruct((B, CONV_K - 1, CONV_WIDTH), F32),
            jax.ShapeDtypeStruct((B, POOL_HIST, POOL_WIDTH), F32),
            jax.ShapeDtypeStruct(state.shape, F32),
        ],
        scratch_shapes=[
            pltpu.VMEM((nb, CONV_PAD + ls, CONV_WIDTH), F32),
            pltpu.VMEM((nb, POOL_PAD + ls, POOL_WIDTH), F32),
        ],
        input_output_aliases=aliases,
        compiler_params=_params(("parallel",), 48),
        name="mixer_sample",
    )(*args)


def _out_ln_kernel(y_ref, x_ref, w_ref, g_ref, b_ref, o_ref, ob_ref):
    m = _dot(y_ref[...].astype(BF16), w_ref[...])
    x2 = _layer_norm(DEEPNORM_ALPHA * x_ref[...] + m, g_ref[...], b_ref[...])
    o_ref[...] = x2
    ob_ref[...] = x2.astype(BF16)


def _out_ln(y, x, w, ln_g, ln_b, layer, tm=256):
    T, D = x.shape
    row = lambda i: (i, 0)
    const = lambda i: (layer, 0, 0)
    return pl.pallas_call(
        _out_ln_kernel,
        grid=(T // tm,),
        in_specs=[
            pl.BlockSpec((tm, y.shape[1]), row),
            pl.BlockSpec((tm, D), row),
            pl.BlockSpec((None,) + w.shape[1:], const),
            pl.BlockSpec((None, 1, D), const),
            pl.BlockSpec((None, 1, D), const),
        ],
        out_specs=[pl.BlockSpec((tm, D), row), pl.BlockSpec((tm, D), row)],
        out_shape=[jax.ShapeDtypeStruct((T, D), F32), jax.ShapeDtypeStruct((T, D), BF16)],
        compiler_params=_params(("parallel",), 48),
        name="out_ln",
    )(y, x, w, ln_g, ln_b)


def _ple_kernel(x_ref, gate_ref, p_ref, proj_ref, o_ref):
    gate = jax.nn.sigmoid(_dot(x_ref[...], gate_ref[...]))
    o_ref[...] = gate * _dot(p_ref[...].astype(BF16), proj_ref[...])


def _ple(xb, gate_w, p, proj_w, layer, tm=512, tn=1024):
    T, D = xb.shape
    N = gate_w.shape[2]
    P = p.shape[2]
    return pl.pallas_call(
        _ple_kernel,
        grid=(N // tn, T // tm),
        in_specs=[
            pl.BlockSpec((tm, D), lambda n, i: (i, 0)),
            pl.BlockSpec((None, D, tn), lambda n, i: (layer, 0, n)),
            pl.BlockSpec((None, tm, P), lambda n, i: (layer, i, 0)),
            pl.BlockSpec((None, P, tn), lambda n, i: (layer, 0, n)),
        ],
        out_specs=pl.BlockSpec((tm, tn), lambda n, i: (i, n)),
        out_shape=jax.ShapeDtypeStruct((T, N), F32),
        compiler_params=_params(("parallel", "arbitrary"), 40),
        name="ple",
    )(xb, gate_w, p, proj_w)


def _rope_tables(start_pos, length):
    half = RET_HEAD_DIM // 2
    inv = ROPE_BASE ** (-jnp.arange(half, dtype=F32) / half)
    pos = start_pos + jnp.arange(length, dtype=F32)
    ang = pos[:, None] * inv[None, :]
    return jnp.cos(ang), jnp.sin(ang)


def _trunk(x, p, caches, start_pos, w):
    B, L, D = x.shape
    T = B * L
    x = x.reshape(T, D)
    p = p.reshape(DEPTH, T, -1)
    cos, sin = _rope_tables(start_pos, L)
    convs, pools, rets = [], [], []
    state_out = None
    for i in range(DEPTH):
        x1, x1b = _ffn_ln(x, w["ffn1_w_gate"], w["ffn1_w_up"], w["ffn1_w_down"], w["ln1_g"], w["ln1_b"], i,
                          emit_bf16=True)
        z = _proj_in(x1b, w["w_in"], i)
        if caches is None:
            y, c_new, p_new, r_new = _mixer_prompt(z, cos, sin, w["conv_w"], w["pool_w"], w["pool_scale"], i, B)
            rets.append(r_new)
        else:
            y, c_new, p_new, state_out = _mixer_sample(
                z.reshape(B, L, IN_COLS), cos, sin, w["conv_w"], w["pool_w"], w["pool_scale"],
                caches[0], caches[1], caches[2], state_out, i, start_pos)
            y = y.reshape(T, D)
        x2, x2b = _out_ln(y, x1, w["w_out"], w["ln2_g"], w["ln2_b"], i)
        e = _ple(x2b, w["ple_gate"], p, w["ple_proj"], i)
        x = _ffn_ln(x2, w["ffn2_w_gate"], w["ffn2_w_up"], w["ffn2_w_down"], w["ln3_g"], w["ln3_b"], i, extra=e)
        convs.append(c_new)
        pools.append(p_new)
    ret = jnp.stack(rets) if caches is None else state_out
    return x.reshape(B, L, D), jnp.stack(convs), jnp.stack(pools), ret


def kernel(x_prompt, x_sample, p_prompt, p_sample, cache_conv, cache_pool, state_ret, ln1_g, ln1_b, ffn1_w_gate, ffn1_w_up, ffn1_w_down, w_in, conv_w, pool_w, pool_scale, w_out, ln2_g, ln2_b, ffn2_w_gate, ffn2_w_up, ffn2_w_down, ple_gate, ple_proj, ln3_g, ln3_b):
    matmul_weights = dict(ffn1_w_gate=ffn1_w_gate, ffn1_w_up=ffn1_w_up, ffn1_w_down=ffn1_w_down, w_in=w_in,
                          pool_w=pool_w, w_out=w_out, ffn2_w_gate=ffn2_w_gate, ffn2_w_up=ffn2_w_up,
                          ffn2_w_down=ffn2_w_down, ple_gate=ple_gate, ple_proj=ple_proj)
    row_params = dict(ln1_g=ln1_g, ln1_b=ln1_b, ln2_g=ln2_g, ln2_b=ln2_b, ln3_g=ln3_g, ln3_b=ln3_b,
                      pool_scale=pool_scale)
    w = {k: _to_bf16(v) for k, v in matmul_weights.items()}
    w.update({k: v.reshape(DEPTH, 1, -1) for k, v in row_params.items()})
    w["conv_w"] = conv_w
    y_prompt, conv_p, pool_p, ret_p = _trunk(x_prompt, p_prompt, None, 0, w)
    y_sample, conv_s, pool_s, ret_s = _trunk(
        x_sample, p_sample, (cache_conv, cache_pool, state_ret), PAST_LEN, w)
    return (y_prompt, y_sample, conv_p, pool_p, ret_p, conv_s, pool_s, ret_s)
```

```python
import functools
import math

import jax
import jax.numpy as jnp
from jax import lax
from jax.experimental import pallas as pl
from jax.experimental.pallas import tpu as pltpu

D_MODEL = 2048
DEPTH = 2
PAST_LEN = 16384
CONV_WIDTH = D_MODEL // 4
POOL_WIDTH = D_MODEL // 4
RET_WIDTH = D_MODEL // 2
CONV_K = 3
POOL_WINDOWS = (2, 4, 8, 16)
POOL_GROUP = POOL_WIDTH // len(POOL_WINDOWS)
POOL_HIST = max(POOL_WINDOWS) - 1
RET_HEADS = 4
RET_HEAD_DIM = RET_WIDTH // RET_HEADS
RET_CHUNK = 128
RET_LOG_GAMMA = tuple(math.log(1.0 - 2.0 ** (-5 - h)) for h in range(RET_HEADS))
ROPE_BASE = 10000.0
IN_COLS = 3 * CONV_WIDTH + POOL_WIDTH + 4 * RET_WIDTH
DEEPNORM_ALPHA = (2 * DEPTH) ** 0.25
LN_EPS = 1e-5

COL_B = 0
COL_C = CONV_WIDTH
COL_H = 2 * CONV_WIDTH
COL_P = 3 * CONV_WIDTH
COL_Q = COL_P + POOL_WIDTH
COL_K = COL_Q + RET_WIDTH
COL_V = COL_K + RET_WIDTH
COL_G = COL_V + RET_WIDTH
OUT_POOL = CONV_WIDTH
OUT_RET = CONV_WIDTH + POOL_WIDTH

SUBLANES = 8
CONV_PAD = SUBLANES
POOL_PAD = 16

F32 = jnp.float32
BF16 = jnp.bfloat16
MIB = 1024 * 1024


def _params(semantics, vmem_mib):
    return pltpu.CompilerParams(dimension_semantics=semantics, vmem_limit_bytes=vmem_mib * MIB)


def _layer_norm(r, g, b):
    mu = jnp.mean(r, axis=-1, keepdims=True)
    c = r - mu
    var = jnp.mean(c * c, axis=-1, keepdims=True)
    return c * lax.rsqrt(var + LN_EPS) * g + b


def _silu(x):
    return x * jax.nn.sigmoid(x)


def _dot(a, b):
    return jnp.dot(a, b, preferred_element_type=F32)


def _cast_kernel(x_ref, o_ref):
    o_ref[...] = x_ref[...].astype(o_ref.dtype)


def _to_bf16(w, block_bytes=6 * MIB):
    shape = w.shape
    w = w.reshape(shape[0], -1, shape[-1])
    _, R, C = w.shape
    tr = R
    while tr * C * 4 > block_bytes and tr % 2 == 0 and (tr // 2) % 16 == 0:
        tr //= 2
    out = pl.pallas_call(
        _cast_kernel,
        grid=(shape[0], R // tr),
        in_specs=[pl.BlockSpec((None, tr, C), lambda d, r: (d, r, 0))],
        out_specs=pl.BlockSpec((None, tr, C), lambda d, r: (d, r, 0)),
        out_shape=jax.ShapeDtypeStruct(w.shape, BF16),
        compiler_params=_params(("parallel", "parallel"), 40),
        name="to_bf16",
    )(w)
    return out.reshape(shape)


def _ffn_ln_kernel(*refs, nf, ple_cols, emit_bf16):
    x_ref, wg_ref, wu_ref, wd_ref, g_ref, b_ref = refs[:6]
    rest = list(refs[6:])
    if ple_cols:
        gate_ref, p_ref, proj_ref = rest[:3]
        rest = rest[3:]
    o_ref = rest.pop(0)
    ob_ref = rest.pop(0) if emit_bf16 else None
    xb_ref = rest.pop(0)
    f = pl.program_id(1)
    n_ple = o_ref.shape[1] // ple_cols if ple_cols else 0

    @pl.when(f == 0)
    def _():
        x = x_ref[...]
        xb_ref[...] = x.astype(BF16)
        o_ref[...] = DEEPNORM_ALPHA * x

    @pl.when(f < nf)
    def _():
        xb = xb_ref[...]
        gate = _dot(xb, wg_ref[...].astype(BF16))
        up = _dot(xb, wu_ref[...].astype(BF16))
        h = (_silu(gate) * up).astype(BF16)
        o_ref[...] += 0.5 * _dot(h, wd_ref[...].astype(BF16))

    for j in range(n_ple):
        @pl.when(f == nf + j)
        def _():
            gate = jax.nn.sigmoid(_dot(xb_ref[...], gate_ref[...]))
            o_ref[:, j * ple_cols:(j + 1) * ple_cols] += gate * _dot(p_ref[...].astype(BF16), proj_ref[...])

    @pl.when(f == nf + n_ple - 1)
    def _():
        y = _layer_norm(o_ref[...], g_ref[...], b_ref[...])
        o_ref[...] = y
        if emit_bf16:
            ob_ref[...] = y.astype(BF16)


def _ffn_ln(x, wg, wu, wd, ln_g, ln_b, layer, ple=None, emit_bf16=False, tm=1024, tf=256, ple_cols=512):
    T, D = x.shape
    F = wg.shape[2]
    nf = F // tf
    n_ple = D // ple_cols if ple is not None else 0
    row = lambda i, f: (i, 0)
    const = lambda i, f: (layer, 0, 0)
    fcol = lambda i, f: (layer, 0, jnp.minimum(f, nf - 1))
    frow = lambda i, f: (layer, jnp.minimum(f, nf - 1), 0)
    in_specs = [
        pl.BlockSpec((tm, D), row, pipeline_mode=pl.Buffered(1)),
        pl.BlockSpec((None, D, tf), fcol),
        pl.BlockSpec((None, D, tf), fcol),
        pl.BlockSpec((None, tf, D), frow),
        pl.BlockSpec((None, 1, D), const),
        pl.BlockSpec((None, 1, D), const),
    ]
    args = [x, wg, wu, wd, ln_g, ln_b]
    if ple is not None:
        gate_w, p, proj_w = ple
        P = p.shape[2]
        pcol = lambda i, f: (layer, 0, jnp.maximum(f - nf, 0))
        in_specs += [
            pl.BlockSpec((None, D, ple_cols), pcol),
            pl.BlockSpec((None, tm, P), lambda i, f: (layer, i, 0)),
            pl.BlockSpec((None, P, ple_cols), pcol),
        ]
        args += [gate_w, p, proj_w]
    out_shape = [jax.ShapeDtypeStruct((T, D), F32)]
    out_specs = [pl.BlockSpec((tm, D), row)]
    if emit_bf16:
        out_shape.append(jax.ShapeDtypeStruct((T, D), BF16))
        out_specs.append(pl.BlockSpec((tm, D), row))
    outs = pl.pallas_call(
        functools.partial(_ffn_ln_kernel, nf=nf, ple_cols=ple_cols if ple is not None else 0,
                          emit_bf16=emit_bf16),
        grid=(T // tm, nf + n_ple),
        in_specs=in_specs,
        out_specs=out_specs,
        out_shape=out_shape,
        scratch_shapes=[pltpu.VMEM((tm, D), BF16)],
        compiler_params=_params(("parallel", "arbitrary"), 58),
        name="ffn_ln",
    )(*args)
    return outs if emit_bf16 else outs[0]


def _proj_in_kernel(x_ref, w_ref, o_ref):
    o_ref[...] = _dot(x_ref[...], w_ref[...])


def _proj_in(xb, w, layer, tm=1024, tn=512):
    T, D = xb.shape
    N = w.shape[2]
    return pl.pallas_call(
        _proj_in_kernel,
        grid=(T // tm, N // tn),
        in_specs=[pl.BlockSpec((tm, D), lambda i, n: (i, 0)),
                  pl.BlockSpec((None, D, tn), lambda i, n: (layer, 0, n))],
        out_specs=pl.BlockSpec((tm, tn), lambda i, n: (i, n)),
        out_shape=jax.ShapeDtypeStruct((T, N), F32),
        compiler_params=_params(("parallel", "arbitrary"), 40),
        name="proj_in",
    )(xb, w)


def _rotary(x, cos, sin):
    half = RET_HEAD_DIM // 2
    x1, x2 = x[:, :half], x[:, half:]
    return jnp.concatenate([x1 * cos - x2 * sin, x1 * sin + x2 * cos], axis=-1)


def _retention_tables(L):
    row = lax.broadcasted_iota(jnp.int32, (L, L), 0)
    col = lax.broadcasted_iota(jnp.int32, (L, L), 1)
    diff = (row - col).astype(F32)
    causal = row >= col
    idx = lax.broadcasted_iota(jnp.int32, (L, 1), 0).astype(F32)
    tables = []
    for lg in RET_LOG_GAMMA:
        decay = jnp.where(causal, jnp.exp(jnp.where(causal, diff, 0.0) * lg), 0.0)
        q_decay = jnp.exp((idx + 1.0) * lg)
        k_decay = jnp.exp((L - 1.0 - idx) * lg)
        tables.append((decay, q_decay, k_decay, math.exp(L * lg)))
    return tables


def _retention_head(q, k, v, gate, S, cos, sin, table):
    decay, q_decay, k_decay, s_decay = table
    qr = _rotary(q, cos, sin)
    kr = _rotary(k, cos, sin) * (RET_HEAD_DIM ** -0.5)
    qb = qr.astype(BF16)
    vb = v.astype(BF16)
    scores = lax.dot_general(qb, kr.astype(BF16), (((1,), (1,)), ((), ())), preferred_element_type=F32)
    inner = _dot((scores * decay).astype(BF16), vb)
    cross = _dot(qb, S.astype(BF16)) * q_decay
    o = inner + cross
    kd = (kr * k_decay).astype(BF16)
    S_new = s_decay * S + lax.dot_general(kd, vb, (((0,), (0,)), ((), ())), preferred_element_type=F32)
    mu = jnp.mean(o, axis=-1, keepdims=True)
    c = o - mu
    var = jnp.mean(c * c, axis=-1, keepdims=True)
    y = c * lax.rsqrt(var + LN_EPS) * _silu(gate)
    return y, S_new


def _window_count(w, pos):
    return jnp.minimum(jnp.float32(w), pos + 1.0)


def _mixer_prompt_kernel(z_ref, cos_ref, sin_ref, convw_ref, poolw_ref, pscale_ref,
                         y_ref, convn_ref, pooln_ref, retn_ref,
                         s_ref, ubuf_ref, pbuf_ref, *, tl, nl):
    l = pl.program_id(1)

    @pl.when(l == 0)
    def _():
        s_ref[...] = jnp.zeros_like(s_ref)
        ubuf_ref[0:CONV_PAD, :] = jnp.zeros((CONV_PAD, CONV_WIDTH), F32)
        pbuf_ref[0:POOL_PAD, :] = jnp.zeros((POOL_PAD, POOL_WIDTH), F32)

    u = z_ref[:, COL_C:COL_C + CONV_WIDTH] * z_ref[:, COL_H:COL_H + CONV_WIDTH]
    ubuf_ref[CONV_PAD:CONV_PAD + tl, :] = u
    cw = convw_ref[...]
    conv = (ubuf_ref[CONV_PAD - 2:CONV_PAD - 2 + tl, :] * cw[0:1, :]
            + ubuf_ref[CONV_PAD - 1:CONV_PAD - 1 + tl, :] * cw[1:2, :]
            + u * cw[2:3, :])
    y_ref[:, 0:CONV_WIDTH] = (z_ref[:, COL_B:COL_B + CONV_WIDTH] * conv).astype(y_ref.dtype)

    pbuf_ref[POOL_PAD:POOL_PAD + tl, :] = z_ref[:, COL_P:COL_P + POOL_WIDTH]
    pos = (l * tl + lax.broadcasted_iota(jnp.int32, (tl, 1), 0)).astype(F32)
    for gi, w in enumerate(POOL_WINDOWS):
        c0 = gi * POOL_GROUP
        tok = pbuf_ref[POOL_PAD:POOL_PAD + tl, c0:c0 + POOL_GROUP]
        win = tok
        for j in range(1, w):
            win = win + pbuf_ref[POOL_PAD - j:POOL_PAD - j + tl, c0:c0 + POOL_GROUP]
        d = win / _window_count(w, pos) - tok
        yb = _dot(d.astype(BF16), poolw_ref[gi]) * pscale_ref[:, c0:c0 + POOL_GROUP]
        y_ref[:, OUT_POOL + c0:OUT_POOL + c0 + POOL_GROUP] = yb.astype(y_ref.dtype)

    tables = _retention_tables(RET_CHUNK)
    for c in range(tl // RET_CHUNK):
        r0 = c * RET_CHUNK
        cos = cos_ref[r0:r0 + RET_CHUNK, :]
        sin = sin_ref[r0:r0 + RET_CHUNK, :]
        for h in range(RET_HEADS):
            h0 = h * RET_HEAD_DIM
            y, s_new = _retention_head(
                z_ref[r0:r0 + RET_CHUNK, COL_Q + h0:COL_Q + h0 + RET_HEAD_DIM],
                z_ref[r0:r0 + RET_CHUNK, COL_K + h0:COL_K + h0 + RET_HEAD_DIM],
                z_ref[r0:r0 + RET_CHUNK, COL_V + h0:COL_V + h0 + RET_HEAD_DIM],
                z_ref[r0:r0 + RET_CHUNK, COL_G + h0:COL_G + h0 + RET_HEAD_DIM],
                s_ref[h], cos, sin, tables[h])
            s_ref[h] = s_new
            y_ref[r0:r0 + RET_CHUNK, OUT_RET + h0:OUT_RET + h0 + RET_HEAD_DIM] = y.astype(y_ref.dtype)

    @pl.when(l == nl - 1)
    def _():
        convn_ref[0] = ubuf_ref[CONV_PAD + tl - (CONV_K - 1):CONV_PAD + tl, :]
        pooln_ref[0] = pbuf_ref[POOL_PAD + tl - POOL_HIST:POOL_PAD + tl, :]
        retn_ref[0] = s_ref[...]

    ubuf_ref[0:CONV_PAD, :] = ubuf_ref[tl:tl + CONV_PAD, :]
    pbuf_ref[0:POOL_PAD, :] = pbuf_ref[tl:tl + POOL_PAD, :]


def _mixer_prompt(z, cos, sin, conv_w, pool_w, pool_scale, layer, batch, tl=256):
    T = z.shape[0]
    L = T // batch
    nl = L // tl
    row = lambda b, l: (b * nl + l, 0)
    return pl.pallas_call(
        functools.partial(_mixer_prompt_kernel, tl=tl, nl=nl),
        grid=(batch, nl),
        in_specs=[
            pl.BlockSpec((tl, IN_COLS), row),
            pl.BlockSpec((tl, RET_HEAD_DIM // 2), lambda b, l: (l, 0)),
            pl.BlockSpec((tl, RET_HEAD_DIM // 2), lambda b, l: (l, 0)),
            pl.BlockSpec((None, CONV_K, CONV_WIDTH), lambda b, l: (layer, 0, 0)),
            pl.BlockSpec((None, len(POOL_WINDOWS), POOL_GROUP, POOL_GROUP), lambda b, l: (layer, 0, 0, 0)),
            pl.BlockSpec((None, 1, POOL_WIDTH), lambda b, l: (layer, 0, 0)),
        ],
        out_specs=[
            pl.BlockSpec((tl, D_MODEL), row),
            pl.BlockSpec((1, CONV_K - 1, CONV_WIDTH), lambda b, l: (b, 0, 0)),
            pl.BlockSpec((1, POOL_HIST, POOL_WIDTH), lambda b, l: (b, 0, 0)),
            pl.BlockSpec((1, RET_HEADS, RET_HEAD_DIM, RET_HEAD_DIM), lambda b, l: (b, 0, 0, 0)),
        ],
        out_shape=[
            jax.ShapeDtypeStruct((T, D_MODEL), BF16),
            jax.ShapeDtypeStruct((batch, CONV_K - 1, CONV_WIDTH), F32),
            jax.ShapeDtypeStruct((batch, POOL_HIST, POOL_WIDTH), F32),
            jax.ShapeDtypeStruct((batch, RET_HEADS, RET_HEAD_DIM, RET_HEAD_DIM), F32),
        ],
        scratch_shapes=[
            pltpu.VMEM((RET_HEADS, RET_HEAD_DIM, RET_HEAD_DIM), F32),
            pltpu.VMEM((CONV_PAD + tl, CONV_WIDTH), F32),
            pltpu.VMEM((POOL_PAD + tl, POOL_WIDTH), F32),
        ],
        compiler_params=_params(("parallel", "arbitrary"), 48),
        name="mixer_prompt",
    )(z, cos, sin, conv_w, pool_w, pool_scale)


def _mixer_sample_kernel(z_ref, cos_ref, sin_ref, convw_ref, poolw_ref, pscale_ref,
                         convc_ref, poolc_ref, state_ref, *rest, nb, ls, start_pos, aliased):
    if aliased:
        rest = rest[1:]
    y_ref, convn_ref, pooln_ref, staten_ref, ubuf_ref, pbuf_ref = rest
    u = z_ref[:, :, COL_C:COL_C + CONV_WIDTH] * z_ref[:, :, COL_H:COL_H + CONV_WIDTH]
    ubuf_ref[:, CONV_PAD - (CONV_K - 1):CONV_PAD, :] = convc_ref[...]
    ubuf_ref[:, CONV_PAD:CONV_PAD + ls, :] = u
    cw = convw_ref[...]
    conv = (ubuf_ref[:, CONV_PAD - 2:CONV_PAD - 2 + ls, :] * cw[0:1, :]
            + ubuf_ref[:, CONV_PAD - 1:CONV_PAD - 1 + ls, :] * cw[1:2, :]
            + u * cw[2:3, :])
    y_ref[:, :, 0:CONV_WIDTH] = z_ref[:, :, COL_B:COL_B + CONV_WIDTH] * conv
    convn_ref[...] = ubuf_ref[:, CONV_PAD + ls - (CONV_K - 1):CONV_PAD + ls, :]

    pbuf_ref[:, POOL_PAD - POOL_HIST:POOL_PAD, :] = poolc_ref[...]
    pbuf_ref[:, POOL_PAD:POOL_PAD + ls, :] = z_ref[:, :, COL_P:COL_P + POOL_WIDTH]
    pos = (start_pos + lax.broadcasted_iota(jnp.int32, (1, ls, 1), 1)).astype(F32)
    for gi, w in enumerate(POOL_WINDOWS):
        c0 = gi * POOL_GROUP
        tok = pbuf_ref[:, POOL_PAD:POOL_PAD + ls, c0:c0 + POOL_GROUP]
        win = tok
        for j in range(1, w):
            win = win + pbuf_ref[:, POOL_PAD - j:POOL_PAD - j + ls, c0:c0 + POOL_GROUP]
        d = win / _window_count(w, pos) - tok
        yb = _dot(d.reshape(nb * ls, POOL_GROUP).astype(BF16), poolw_ref[gi]).reshape(nb, ls, POOL_GROUP)
        y_ref[:, :, OUT_POOL + c0:OUT_POOL + c0 + POOL_GROUP] = yb * pscale_ref[:, c0:c0 + POOL_GROUP]
    pooln_ref[...] = pbuf_ref[:, POOL_PAD + ls - POOL_HIST:POOL_PAD + ls, :]

    tables = _retention_tables(ls)
    cos = cos_ref[...]
    sin = sin_ref[...]

    def per_sequence(b, carry):
        for h in range(RET_HEADS):
            h0 = h * RET_HEAD_DIM
            y, s_new = _retention_head(
                z_ref[b, :, COL_Q + h0:COL_Q + h0 + RET_HEAD_DIM],
                z_ref[b, :, COL_K + h0:COL_K + h0 + RET_HEAD_DIM],
                z_ref[b, :, COL_V + h0:COL_V + h0 + RET_HEAD_DIM],
                z_ref[b, :, COL_G + h0:COL_G + h0 + RET_HEAD_DIM],
                state_ref[b, h], cos, sin, tables[h])
            staten_ref[b, h] = s_new
            y_ref[b, :, OUT_RET + h0:OUT_RET + h0 + RET_HEAD_DIM] = y
        return carry

    lax.fori_loop(0, nb, per_sequence, 0)


def _mixer_sample(z, cos, sin, conv_w, pool_w, pool_scale, conv_cache, pool_cache, state, state_out,
                  layer, start_pos, nb=8):
    B, ls, _ = z.shape
    seq3 = lambda i: (i, 0, 0)
    lay3 = lambda i: (layer, 0, 0)
    lseq4 = lambda i: (layer, i, 0, 0)
    lseq5 = lambda i: (layer, i, 0, 0, 0)
    in_specs = [
        pl.BlockSpec((nb, ls, IN_COLS), seq3),
        pl.BlockSpec((ls, RET_HEAD_DIM // 2), lambda i: (0, 0)),
        pl.BlockSpec((ls, RET_HEAD_DIM // 2), lambda i: (0, 0)),
        pl.BlockSpec((None, CONV_K, CONV_WIDTH), lay3),
        pl.BlockSpec((None, len(POOL_WINDOWS), POOL_GROUP, POOL_GROUP), lambda i: (layer, 0, 0, 0)),
        pl.BlockSpec((None, 1, POOL_WIDTH), lay3),
        pl.BlockSpec((None, nb, CONV_K - 1, CONV_WIDTH), lseq4),
        pl.BlockSpec((None, nb, POOL_HIST, POOL_WIDTH), lseq4),
        pl.BlockSpec((None, nb, RET_HEADS, RET_HEAD_DIM, RET_HEAD_DIM), lseq5),
    ]
    args = [z, cos, sin, conv_w, pool_w, pool_scale, conv_cache, pool_cache, state]
    aliases = {}
    if state_out is not None:
        in_specs.append(pl.BlockSpec(memory_space=pl.ANY))
        args.append(state_out)
        aliases = {len(args) - 1: 3}
    return pl.pallas_call(
        functools.partial(_mixer_sample_kernel, nb=nb, ls=ls, start_pos=start_pos,
                          aliased=state_out is not None),
        grid=(B // nb,),
        in_specs=in_specs,
        out_specs=[
            pl.BlockSpec((nb, ls, D_MODEL), seq3),
            pl.BlockSpec((nb, CONV_K - 1, CONV_WIDTH), seq3),
            pl.BlockSpec((nb, POOL_HIST, POOL_WIDTH), seq3),
            pl.BlockSpec((None, nb, RET_HEADS, RET_HEAD_DIM, RET_HEAD_DIM), lseq5),
        ],
        out_shape=[
            jax.ShapeDtypeStruct((B, ls, D_MODEL), F32),
            jax.ShapeDtypeStruct((B, CONV_K - 1, CONV_WIDTH), F32),
            jax.ShapeDtypeStruct((B, POOL_HIST, POOL_WIDTH), F32),
            jax.ShapeDtypeStruct(state.shape, F32),
        ],
        scratch_shapes=[
            pltpu.VMEM((nb, CONV_PAD + ls, CONV_WIDTH), F32),
            pltpu.VMEM((nb, POOL_PAD + ls, POOL_WIDTH), F32),
        ],
        input_output_aliases=aliases,
        compiler_params=_params(("parallel",), 48),
        name="mixer_sample",
    )(*args)


def _out_ln_kernel(y_ref, x_ref, w_ref, g_ref, b_ref, o_ref):
    m = _dot(y_ref[...].astype(BF16), w_ref[...])
    o_ref[...] = _layer_norm(DEEPNORM_ALPHA * x_ref[...] + m, g_ref[...], b_ref[...])


def _out_ln(y, x, w, ln_g, ln_b, layer, tm=256):
    T, D = x.shape
    row = lambda i: (i, 0)
    const = lambda i: (layer, 0, 0)
    return pl.pallas_call(
        _out_ln_kernel,
        grid=(T // tm,),
        in_specs=[
            pl.BlockSpec((tm, y.shape[1]), row),
            pl.BlockSpec((tm, D), row),
            pl.BlockSpec((None,) + w.shape[1:], const),
            pl.BlockSpec((None, 1, D), const),
            pl.BlockSpec((None, 1, D), const),
        ],
        out_specs=pl.BlockSpec((tm, D), row),
        out_shape=jax.ShapeDtypeStruct((T, D), F32),
        compiler_params=_params(("parallel",), 48),
        name="out_ln",
    )(y, x, w, ln_g, ln_b)


def _rope_tables(start_pos, length):
    half = RET_HEAD_DIM // 2
    inv = ROPE_BASE ** (-jnp.arange(half, dtype=F32) / half)
    pos = start_pos + jnp.arange(length, dtype=F32)
    ang = pos[:, None] * inv[None, :]
    return jnp.cos(ang), jnp.sin(ang)


def _trunk(x, p, caches, start_pos, w):
    B, L, D = x.shape
    T = B * L
    x = x.reshape(T, D)
    p = p.reshape(DEPTH, T, -1)
    cos, sin = _rope_tables(start_pos, L)
    convs, pools, rets = [], [], []
    state_out = None
    for i in range(DEPTH):
        x1, x1b = _ffn_ln(x, w["ffn1_w_gate"], w["ffn1_w_up"], w["ffn1_w_down"], w["ln1_g"], w["ln1_b"], i,
                          emit_bf16=True)
        z = _proj_in(x1b, w["w_in"], i)
        if caches is None:
            y, c_new, p_new, r_new = _mixer_prompt(z, cos, sin, w["conv_w"], w["pool_w"], w["pool_scale"], i, B)
            rets.append(r_new)
        else:
            y, c_new, p_new, state_out = _mixer_sample(
                z.reshape(B, L, IN_COLS), cos, sin, w["conv_w"], w["pool_w"], w["pool_scale"],
                caches[0], caches[1], caches[2], state_out, i, start_pos)
            y = y.reshape(T, D)
        x2 = _out_ln(y, x1, w["w_out"], w["ln2_g"], w["ln2_b"], i)
        x = _ffn_ln(x2, w["ffn2_w_gate"], w["ffn2_w_up"], w["ffn2_w_down"], w["ln3_g"], w["ln3_b"], i,
                    ple=(w["ple_gate"], p, w["ple_proj"]))
        convs.append(c_new)
        pools.append(p_new)
    ret = jnp.stack(rets) if caches is None else state_out
    return x.reshape(B, L, D), jnp.stack(convs), jnp.stack(pools), ret


def kernel(x_prompt, x_sample, p_prompt, p_sample, cache_conv, cache_pool, state_ret, ln1_g, ln1_b, ffn1_w_gate, ffn1_w_up, ffn1_w_down, w_in, conv_w, pool_w, pool_scale, w_out, ln2_g, ln2_b, ffn2_w_gate, ffn2_w_up, ffn2_w_down, ple_gate, ple_proj, ln3_g, ln3_b):
    precast = dict(w_in=w_in, pool_w=pool_w, w_out=w_out, ple_gate=ple_gate, ple_proj=ple_proj)
    row_params = dict(ln1_g=ln1_g, ln1_b=ln1_b, ln2_g=ln2_g, ln2_b=ln2_b, ln3_g=ln3_g, ln3_b=ln3_b,
                      pool_scale=pool_scale)
    w = {k: _to_bf16(v) for k, v in precast.items()}
    w.update({k: v.reshape(DEPTH, 1, -1) for k, v in row_params.items()})
    w.update(conv_w=conv_w, ffn1_w_gate=ffn1_w_gate, ffn1_w_up=ffn1_w_up, ffn1_w_down=ffn1_w_down,
             ffn2_w_gate=ffn2_w_gate, ffn2_w_up=ffn2_w_up, ffn2_w_down=ffn2_w_down)
    y_prompt, conv_p, pool_p, ret_p = _trunk(x_prompt, p_prompt, None, 0, w)
    y_sample, conv_s, pool_s, ret_s = _trunk(
        x_sample, p_sample, (cache_conv, cache_pool, state_ret), PAST_LEN, w)
    return (y_prompt, y_sample, conv_p, pool_p, ret_p, conv_s, pool_s, ret_s)
```

```python
import functools
import math

import jax
import jax.numpy as jnp
from jax import lax
from jax.experimental import pallas as pl
from jax.experimental.pallas import tpu as pltpu

D_MODEL = 2048
DEPTH = 2
PAST_LEN = 16384
CONV_WIDTH = D_MODEL // 4
POOL_WIDTH = D_MODEL // 4
RET_WIDTH = D_MODEL // 2
CONV_K = 3
POOL_WINDOWS = (2, 4, 8, 16)
POOL_GROUP = POOL_WIDTH // len(POOL_WINDOWS)
POOL_HIST = max(POOL_WINDOWS) - 1
RET_HEADS = 4
RET_HEAD_DIM = RET_WIDTH // RET_HEADS
RET_CHUNK = 128
RET_LOG_GAMMA = tuple(math.log(1.0 - 2.0 ** (-5 - h)) for h in range(RET_HEADS))
ROPE_BASE = 10000.0
IN_COLS = 3 * CONV_WIDTH + POOL_WIDTH + 4 * RET_WIDTH
DEEPNORM_ALPHA = (2 * DEPTH) ** 0.25
LN_EPS = 1e-5

COL_B = 0
COL_C = CONV_WIDTH
COL_H = 2 * CONV_WIDTH
COL_P = 3 * CONV_WIDTH
COL_Q = COL_P + POOL_WIDTH
COL_K = COL_Q + RET_WIDTH
COL_V = COL_K + RET_WIDTH
COL_G = COL_V + RET_WIDTH
OUT_POOL = CONV_WIDTH
OUT_RET = CONV_WIDTH + POOL_WIDTH

SUBLANES = 8
CONV_PAD = SUBLANES
POOL_PAD = 16

F32 = jnp.float32
BF16 = jnp.bfloat16
MIB = 1024 * 1024


def _params(semantics, vmem_mib):
    return pltpu.CompilerParams(dimension_semantics=semantics, vmem_limit_bytes=vmem_mib * MIB)


def _layer_norm(r, g, b):
    mu = jnp.mean(r, axis=-1, keepdims=True)
    c = r - mu
    var = jnp.mean(c * c, axis=-1, keepdims=True)
    return c * lax.rsqrt(var + LN_EPS) * g + b


def _silu(x):
    return x * jax.nn.sigmoid(x)


def _dot(a, b):
    return jnp.dot(a, b, preferred_element_type=F32)


def _cast_kernel(x_ref, o_ref):
    o_ref[...] = x_ref[...].astype(o_ref.dtype)


def _to_bf16(w, block_bytes=6 * MIB):
    shape = w.shape
    w = w.reshape(shape[0], -1, shape[-1])
    _, R, C = w.shape
    tr = R
    while tr * C * 4 > block_bytes and tr % 2 == 0 and (tr // 2) % 16 == 0:
        tr //= 2
    out = pl.pallas_call(
        _cast_kernel,
        grid=(shape[0], R // tr),
        in_specs=[pl.BlockSpec((None, tr, C), lambda d, r: (d, r, 0))],
        out_specs=pl.BlockSpec((None, tr, C), lambda d, r: (d, r, 0)),
        out_shape=jax.ShapeDtypeStruct(w.shape, BF16),
        compiler_params=_params(("parallel", "parallel"), 40),
        name="to_bf16",
    )(w)
    return out.reshape(shape)


def _ffn_ln_kernel(*refs, nf, ple_cols, emit_bf16):
    x_ref, wg_ref, wu_ref, wd_ref, g_ref, b_ref = refs[:6]
    rest = list(refs[6:])
    if ple_cols:
        gate_ref, p_ref, proj_ref = rest[:3]
        rest = rest[3:]
    o_ref = rest.pop(0)
    ob_ref = rest.pop(0) if emit_bf16 else None
    xb_ref = rest.pop(0)
    f = pl.program_id(1)
    n_ple = o_ref.shape[1] // ple_cols if ple_cols else 0

    @pl.when(f == 0)
    def _():
        x = x_ref[...]
        xb_ref[...] = x.astype(BF16)
        o_ref[...] = DEEPNORM_ALPHA * x

    @pl.when(f < nf)
    def _():
        xb = xb_ref[...]
        gate = _dot(xb, wg_ref[...].astype(BF16))
        up = _dot(xb, wu_ref[...].astype(BF16))
        h = (_silu(gate) * up).astype(BF16)
        o_ref[...] += 0.5 * _dot(h, wd_ref[...].astype(BF16))

    for j in range(n_ple):
        @pl.when(f == nf + j)
        def _():
            gate = jax.nn.sigmoid(_dot(xb_ref[...], gate_ref[...]))
            o_ref[:, j * ple_cols:(j + 1) * ple_cols] += gate * _dot(p_ref[...].astype(BF16), proj_ref[...])

    @pl.when(f == nf + n_ple - 1)
    def _():
        y = _layer_norm(o_ref[...], g_ref[...], b_ref[...])
        o_ref[...] = y
        if emit_bf16:
            ob_ref[...] = y.astype(BF16)


def _ffn_ln(x, wg, wu, wd, ln_g, ln_b, layer, ple=None, emit_bf16=False, tm=1024, tf=256, ple_cols=512):
    T, D = x.shape
    F = wg.shape[2]
    nf = F // tf
    n_ple = D // ple_cols if ple is not None else 0
    row = lambda i, f: (i, 0)
    const = lambda i, f: (layer, 0, 0)
    fcol = lambda i, f: (layer, 0, jnp.minimum(f, nf - 1))
    frow = lambda i, f: (layer, jnp.minimum(f, nf - 1), 0)
    in_specs = [
        pl.BlockSpec((tm, D), row, pipeline_mode=pl.Buffered(1)),
        pl.BlockSpec((None, D, tf), fcol),
        pl.BlockSpec((None, D, tf), fcol),
        pl.BlockSpec((None, tf, D), frow),
        pl.BlockSpec((None, 1, D), const),
        pl.BlockSpec((None, 1, D), const),
    ]
    args = [x, wg, wu, wd, ln_g, ln_b]
    if ple is not None:
        gate_w, p, proj_w = ple
        P = p.shape[2]
        pcol = lambda i, f: (layer, 0, jnp.maximum(f - nf, 0))
        in_specs += [
            pl.BlockSpec((None, D, ple_cols), pcol),
            pl.BlockSpec((None, tm, P), lambda i, f: (layer, i, 0)),
            pl.BlockSpec((None, P, ple_cols), pcol),
        ]
        args += [gate_w, p, proj_w]
    out_shape = [jax.ShapeDtypeStruct((T, D), F32)]
    out_specs = [pl.BlockSpec((tm, D), row)]
    if emit_bf16:
        out_shape.append(jax.ShapeDtypeStruct((T, D), BF16))
        out_specs.append(pl.BlockSpec((tm, D), row))
    outs = pl.pallas_call(
        functools.partial(_ffn_ln_kernel, nf=nf, ple_cols=ple_cols if ple is not None else 0,
                          emit_bf16=emit_bf16),
        grid=(T // tm, nf + n_ple),
        in_specs=in_specs,
        out_specs=out_specs,
        out_shape=out_shape,
        scratch_shapes=[pltpu.VMEM((tm, D), BF16)],
        compiler_params=_params(("parallel", "arbitrary"), 58),
        name="ffn_ln",
    )(*args)
    return outs if emit_bf16 else outs[0]


def _proj_in_kernel(x_ref, w_ref, o_ref):
    o_ref[...] = _dot(x_ref[...], w_ref[...])


def _proj_in(xb, w, layer, tm=1024, tn=1024):
    T, D = xb.shape
    N = w.shape[2]
    return pl.pallas_call(
        _proj_in_kernel,
        grid=(T // tm, N // tn),
        in_specs=[pl.BlockSpec((tm, D), lambda i, n: (i, 0)),
                  pl.BlockSpec((None, D, tn), lambda i, n: (layer, 0, n))],
        out_specs=pl.BlockSpec((tm, tn), lambda i, n: (i, n)),
        out_shape=jax.ShapeDtypeStruct((T, N), F32),
        compiler_params=_params(("parallel", "arbitrary"), 40),
        name="proj_in",
    )(xb, w)


def _rotary(x, cos, sin):
    half = RET_HEAD_DIM // 2
    x1, x2 = x[:, :half], x[:, half:]
    return jnp.concatenate([x1 * cos - x2 * sin, x1 * sin + x2 * cos], axis=-1)


def _retention_tables(L):
    row = lax.broadcasted_iota(jnp.int32, (L, L), 0)
    col = lax.broadcasted_iota(jnp.int32, (L, L), 1)
    diff = (row - col).astype(F32)
    causal = row >= col
    idx = lax.broadcasted_iota(jnp.int32, (L, 1), 0).astype(F32)
    tables = []
    for lg in RET_LOG_GAMMA:
        decay = jnp.where(causal, jnp.exp(jnp.where(causal, diff, 0.0) * lg), 0.0)
        q_decay = jnp.exp((idx + 1.0) * lg)
        k_decay = jnp.exp((L - 1.0 - idx) * lg)
        tables.append((decay, q_decay, k_decay, math.exp(L * lg)))
    return tables


def _retention_head(q, k, v, gate, S, cos, sin, table):
    decay, q_decay, k_decay, s_decay = table
    qr = _rotary(q, cos, sin)
    kr = _rotary(k, cos, sin) * (RET_HEAD_DIM ** -0.5)
    qb = qr.astype(BF16)
    vb = v.astype(BF16)
    scores = lax.dot_general(qb, kr.astype(BF16), (((1,), (1,)), ((), ())), preferred_element_type=F32)
    inner = _dot((scores * decay).astype(BF16), vb)
    cross = _dot(qb, S.astype(BF16)) * q_decay
    o = inner + cross
    kd = (kr * k_decay).astype(BF16)
    S_new = s_decay * S + lax.dot_general(kd, vb, (((0,), (0,)), ((), ())), preferred_element_type=F32)
    mu = jnp.mean(o, axis=-1, keepdims=True)
    c = o - mu
    var = jnp.mean(c * c, axis=-1, keepdims=True)
    y = c * lax.rsqrt(var + LN_EPS) * _silu(gate)
    return y, S_new


def _window_count(w, pos):
    return jnp.minimum(jnp.float32(w), pos + 1.0)


def _mixer_prompt_kernel(z_ref, cos_ref, sin_ref, convw_ref, poolw_ref, pscale_ref,
                         x_ref, wout_ref, g_ref, b_ref,
                         o_ref, convn_ref, pooln_ref, retn_ref,
                         y_ref, s_ref, ubuf_ref, pbuf_ref, *, tl, nl):
    l = pl.program_id(1)

    @pl.when(l == 0)
    def _():
        s_ref[...] = jnp.zeros_like(s_ref)
        ubuf_ref[0:CONV_PAD, :] = jnp.zeros((CONV_PAD, CONV_WIDTH), F32)
        pbuf_ref[0:POOL_PAD, :] = jnp.zeros((POOL_PAD, POOL_WIDTH), F32)

    u = z_ref[:, COL_C:COL_C + CONV_WIDTH] * z_ref[:, COL_H:COL_H + CONV_WIDTH]
    ubuf_ref[CONV_PAD:CONV_PAD + tl, :] = u
    cw = convw_ref[...]
    conv = (ubuf_ref[CONV_PAD - 2:CONV_PAD - 2 + tl, :] * cw[0:1, :]
            + ubuf_ref[CONV_PAD - 1:CONV_PAD - 1 + tl, :] * cw[1:2, :]
            + u * cw[2:3, :])
    y_ref[:, 0:CONV_WIDTH] = (z_ref[:, COL_B:COL_B + CONV_WIDTH] * conv).astype(y_ref.dtype)

    pbuf_ref[POOL_PAD:POOL_PAD + tl, :] = z_ref[:, COL_P:COL_P + POOL_WIDTH]
    pos = (l * tl + lax.broadcasted_iota(jnp.int32, (tl, 1), 0)).astype(F32)
    for gi, w in enumerate(POOL_WINDOWS):
        c0 = gi * POOL_GROUP
        tok = pbuf_ref[POOL_PAD:POOL_PAD + tl, c0:c0 + POOL_GROUP]
        win = tok
        for j in range(1, w):
            win = win + pbuf_ref[POOL_PAD - j:POOL_PAD - j + tl, c0:c0 + POOL_GROUP]
        d = win / _window_count(w, pos) - tok
        yb = _dot(d.astype(BF16), poolw_ref[gi]) * pscale_ref[:, c0:c0 + POOL_GROUP]
        y_ref[:, OUT_POOL + c0:OUT_POOL + c0 + POOL_GROUP] = yb.astype(y_ref.dtype)

    tables = _retention_tables(RET_CHUNK)
    for c in range(tl // RET_CHUNK):
        r0 = c * RET_CHUNK
        cos = cos_ref[r0:r0 + RET_CHUNK, :]
        sin = sin_ref[r0:r0 + RET_CHUNK, :]
        for h in range(RET_HEADS):
            h0 = h * RET_HEAD_DIM
            y, s_new = _retention_head(
                z_ref[r0:r0 + RET_CHUNK, COL_Q + h0:COL_Q + h0 + RET_HEAD_DIM],
                z_ref[r0:r0 + RET_CHUNK, COL_K + h0:COL_K + h0 + RET_HEAD_DIM],
                z_ref[r0:r0 + RET_CHUNK, COL_V + h0:COL_V + h0 + RET_HEAD_DIM],
                z_ref[r0:r0 + RET_CHUNK, COL_G + h0:COL_G + h0 + RET_HEAD_DIM],
                s_ref[h], cos, sin, tables[h])
            s_ref[h] = s_new
            y_ref[r0:r0 + RET_CHUNK, OUT_RET + h0:OUT_RET + h0 + RET_HEAD_DIM] = y.astype(y_ref.dtype)
        m = _dot(y_ref[r0:r0 + RET_CHUNK, :], wout_ref[...])
        o_ref[r0:r0 + RET_CHUNK, :] = _layer_norm(
            DEEPNORM_ALPHA * x_ref[r0:r0 + RET_CHUNK, :] + m, g_ref[...], b_ref[...])

    @pl.when(l == nl - 1)
    def _():
        convn_ref[0] = ubuf_ref[CONV_PAD + tl - (CONV_K - 1):CONV_PAD + tl, :]
        pooln_ref[0] = pbuf_ref[POOL_PAD + tl - POOL_HIST:POOL_PAD + tl, :]
        retn_ref[0] = s_ref[...]

    ubuf_ref[0:CONV_PAD, :] = ubuf_ref[tl:tl + CONV_PAD, :]
    pbuf_ref[0:POOL_PAD, :] = pbuf_ref[tl:tl + POOL_PAD, :]


def _mixer_prompt(z, x1, cos, sin, conv_w, pool_w, pool_scale, w_out, ln_g, ln_b, layer, batch, tl=256):
    T = z.shape[0]
    L = T // batch
    nl = L // tl
    row = lambda b, l: (b * nl + l, 0)
    lay3 = lambda b, l: (layer, 0, 0)
    return pl.pallas_call(
        functools.partial(_mixer_prompt_kernel, tl=tl, nl=nl),
        grid=(batch, nl),
        in_specs=[
            pl.BlockSpec((tl, IN_COLS), row),
            pl.BlockSpec((tl, RET_HEAD_DIM // 2), lambda b, l: (l, 0)),
            pl.BlockSpec((tl, RET_HEAD_DIM // 2), lambda b, l: (l, 0)),
            pl.BlockSpec((None, CONV_K, CONV_WIDTH), lay3),
            pl.BlockSpec((None, len(POOL_WINDOWS), POOL_GROUP, POOL_GROUP), lambda b, l: (layer, 0, 0, 0)),
            pl.BlockSpec((None, 1, POOL_WIDTH), lay3),
            pl.BlockSpec((tl, D_MODEL), row),
            pl.BlockSpec((None, D_MODEL, D_MODEL), lay3, pipeline_mode=pl.Buffered(1)),
            pl.BlockSpec((None, 1, D_MODEL), lay3),
            pl.BlockSpec((None, 1, D_MODEL), lay3),
        ],
        out_specs=[
            pl.BlockSpec((tl, D_MODEL), row),
            pl.BlockSpec((1, CONV_K - 1, CONV_WIDTH), lambda b, l: (b, 0, 0)),
            pl.BlockSpec((1, POOL_HIST, POOL_WIDTH), lambda b, l: (b, 0, 0)),
            pl.BlockSpec((1, RET_HEADS, RET_HEAD_DIM, RET_HEAD_DIM), lambda b, l: (b, 0, 0, 0)),
        ],
        out_shape=[
            jax.ShapeDtypeStruct((T, D_MODEL), F32),
            jax.ShapeDtypeStruct((batch, CONV_K - 1, CONV_WIDTH), F32),
            jax.ShapeDtypeStruct((batch, POOL_HIST, POOL_WIDTH), F32),
            jax.ShapeDtypeStruct((batch, RET_HEADS, RET_HEAD_DIM, RET_HEAD_DIM), F32),
        ],
        scratch_shapes=[
            pltpu.VMEM((tl, D_MODEL), BF16),
            pltpu.VMEM((RET_HEADS, RET_HEAD_DIM, RET_HEAD_DIM), F32),
            pltpu.VMEM((CONV_PAD + tl, CONV_WIDTH), F32),
            pltpu.VMEM((POOL_PAD + tl, POOL_WIDTH), F32),
        ],
        compiler_params=_params(("parallel", "arbitrary"), 48),
        name="mixer_prompt",
    )(z, cos, sin, conv_w, pool_w, pool_scale, x1, w_out, ln_g, ln_b)


def _mixer_sample_kernel(z_ref, cos_ref, sin_ref, convw_ref, poolw_ref, pscale_ref,
                         convc_ref, poolc_ref, state_ref, *rest, nb, ls, start_pos, aliased):
    if aliased:
        rest = rest[1:]
    y_ref, convn_ref, pooln_ref, staten_ref, ubuf_ref, pbuf_ref = rest
    u = z_ref[:, :, COL_C:COL_C + CONV_WIDTH] * z_ref[:, :, COL_H:COL_H + CONV_WIDTH]
    ubuf_ref[:, CONV_PAD - (CONV_K - 1):CONV_PAD, :] = convc_ref[...]
    ubuf_ref[:, CONV_PAD:CONV_PAD + ls, :] = u
    cw = convw_ref[...]
    conv = (ubuf_ref[:, CONV_PAD - 2:CONV_PAD - 2 + ls, :] * cw[0:1, :]
            + ubuf_ref[:, CONV_PAD - 1:CONV_PAD - 1 + ls, :] * cw[1:2, :]
            + u * cw[2:3, :])
    y_ref[:, :, 0:CONV_WIDTH] = z_ref[:, :, COL_B:COL_B + CONV_WIDTH] * conv
    convn_ref[...] = ubuf_ref[:, CONV_PAD + ls - (CONV_K - 1):CONV_PAD + ls, :]

    pbuf_ref[:, POOL_PAD - POOL_HIST:POOL_PAD, :] = poolc_ref[...]
    pbuf_ref[:, POOL_PAD:POOL_PAD + ls, :] = z_ref[:, :, COL_P:COL_P + POOL_WIDTH]
    pos = (start_pos + lax.broadcasted_iota(jnp.int32, (1, ls, 1), 1)).astype(F32)
    for gi, w in enumerate(POOL_WINDOWS):
        c0 = gi * POOL_GROUP
        tok = pbuf_ref[:, POOL_PAD:POOL_PAD + ls, c0:c0 + POOL_GROUP]
        win = tok
        for j in range(1, w):
            win = win + pbuf_ref[:, POOL_PAD - j:POOL_PAD - j + ls, c0:c0 + POOL_GROUP]
        d = win / _window_count(w, pos) - tok
        yb = _dot(d.reshape(nb * ls, POOL_GROUP).astype(BF16), poolw_ref[gi]).reshape(nb, ls, POOL_GROUP)
        y_ref[:, :, OUT_POOL + c0:OUT_POOL + c0 + POOL_GROUP] = yb * pscale_ref[:, c0:c0 + POOL_GROUP]
    pooln_ref[...] = pbuf_ref[:, POOL_PAD + ls - POOL_HIST:POOL_PAD + ls, :]

    tables = _retention_tables(ls)
    cos = cos_ref[...]
    sin = sin_ref[...]

    def per_sequence(b, carry):
        for h in range(RET_HEADS):
            h0 = h * RET_HEAD_DIM
            y, s_new = _retention_head(
                z_ref[b, :, COL_Q + h0:COL_Q + h0 + RET_HEAD_DIM],
                z_ref[b, :, COL_K + h0:COL_K + h0 + RET_HEAD_DIM],
                z_ref[b, :, COL_V + h0:COL_V + h0 + RET_HEAD_DIM],
                z_ref[b, :, COL_G + h0:COL_G + h0 + RET_HEAD_DIM],
                state_ref[b, h], cos, sin, tables[h])
            staten_ref[b, h] = s_new
            y_ref[b, :, OUT_RET + h0:OUT_RET + h0 + RET_HEAD_DIM] = y
        return carry

    lax.fori_loop(0, nb, per_sequence, 0)


def _mixer_sample(z, cos, sin, conv_w, pool_w, pool_scale, conv_cache, pool_cache, state, state_out,
                  layer, start_pos, nb=8):
    B, ls, _ = z.shape
    seq3 = lambda i: (i, 0, 0)
    lay3 = lambda i: (layer, 0, 0)
    lseq4 = lambda i: (layer, i, 0, 0)
    lseq5 = lambda i: (layer, i, 0, 0, 0)
    in_specs = [
        pl.BlockSpec((nb, ls, IN_COLS), seq3),
        pl.BlockSpec((ls, RET_HEAD_DIM // 2), lambda i: (0, 0)),
        pl.BlockSpec((ls, RET_HEAD_DIM // 2), lambda i: (0, 0)),
        pl.BlockSpec((None, CONV_K, CONV_WIDTH), lay3),
        pl.BlockSpec((None, len(POOL_WINDOWS), POOL_GROUP, POOL_GROUP), lambda i: (layer, 0, 0, 0)),
        pl.BlockSpec((None, 1, POOL_WIDTH), lay3),
        pl.BlockSpec((None, nb, CONV_K - 1, CONV_WIDTH), lseq4),
        pl.BlockSpec((None, nb, POOL_HIST, POOL_WIDTH), lseq4),
        pl.BlockSpec((None, nb, RET_HEADS, RET_HEAD_DIM, RET_HEAD_DIM), lseq5),
    ]
    args = [z, cos, sin, conv_w, pool_w, pool_scale, conv_cache, pool_cache, state]
    aliases = {}
    if state_out is not None:
        in_specs.append(pl.BlockSpec(memory_space=pl.ANY))
        args.append(state_out)
        aliases = {len(args) - 1: 3}
    return pl.pallas_call(
        functools.partial(_mixer_sample_kernel, nb=nb, ls=ls, start_pos=start_pos,
                          aliased=state_out is not None),
        grid=(B // nb,),
        in_specs=in_specs,
        out_specs=[
            pl.BlockSpec((nb, ls, D_MODEL), seq3),
            pl.BlockSpec((nb, CONV_K - 1, CONV_WIDTH), seq3),
            pl.BlockSpec((nb, POOL_HIST, POOL_WIDTH), seq3),
            pl.BlockSpec((None, nb, RET_HEADS, RET_HEAD_DIM, RET_HEAD_DIM), lseq5),
        ],
        out_shape=[
            jax.ShapeDtypeStruct((B, ls, D_MODEL), F32),
            jax.ShapeDtypeStruct((B, CONV_K - 1, CONV_WIDTH), F32),
            jax.ShapeDtypeStruct((B, POOL_HIST, POOL_WIDTH), F32),
            jax.ShapeDtypeStruct(state.shape, F32),
        ],
        scratch_shapes=[
            pltpu.VMEM((nb, CONV_PAD + ls, CONV_WIDTH), F32),
            pltpu.VMEM((nb, POOL_PAD + ls, POOL_WIDTH), F32),
        ],
        input_output_aliases=aliases,
        compiler_params=_params(("parallel",), 48),
        name="mixer_sample",
    )(*args)


def _out_ln_kernel(y_ref, x_ref, w_ref, g_ref, b_ref, o_ref):
    m = _dot(y_ref[...].astype(BF16), w_ref[...])
    o_ref[...] = _layer_norm(DEEPNORM_ALPHA * x_ref[...] + m, g_ref[...], b_ref[...])


def _out_ln(y, x, w, ln_g, ln_b, layer, tm=256):
    T, D = x.shape
    row = lambda i: (i, 0)
    const = lambda i: (layer, 0, 0)
    return pl.pallas_call(
        _out_ln_kernel,
        grid=(T // tm,),
        in_specs=[
            pl.BlockSpec((tm, y.shape[1]), row),
            pl.BlockSpec((tm, D), row),
            pl.BlockSpec((None,) + w.shape[1:], const),
            pl.BlockSpec((None, 1, D), const),
            pl.BlockSpec((None, 1, D), const),
        ],
        out_specs=pl.BlockSpec((tm, D), row),
        out_shape=jax.ShapeDtypeStruct((T, D), F32),
        compiler_params=_params(("parallel",), 48),
        name="out_ln",
    )(y, x, w, ln_g, ln_b)


def _rope_tables(start_pos, length):
    half = RET_HEAD_DIM // 2
    inv = ROPE_BASE ** (-jnp.arange(half, dtype=F32) / half)
    pos = start_pos + jnp.arange(length, dtype=F32)
    ang = pos[:, None] * inv[None, :]
    return jnp.cos(ang), jnp.sin(ang)


def _trunk(x, p, caches, start_pos, w):
    B, L, D = x.shape
    T = B * L
    x = x.reshape(T, D)
    p = p.reshape(DEPTH, T, -1)
    cos, sin = _rope_tables(start_pos, L)
    convs, pools, rets = [], [], []
    state_out = None
    for i in range(DEPTH):
        x1, x1b = _ffn_ln(x, w["ffn1_w_gate"], w["ffn1_w_up"], w["ffn1_w_down"], w["ln1_g"], w["ln1_b"], i,
                          emit_bf16=True)
        z = _proj_in(x1b, w["w_in"], i)
        if caches is None:
            x2, c_new, p_new, r_new = _mixer_prompt(
                z, x1, cos, sin, w["conv_w"], w["pool_w"], w["pool_scale"], w["w_out"], w["ln2_g"], w["ln2_b"],
                i, B)
            rets.append(r_new)
        else:
            y, c_new, p_new, state_out = _mixer_sample(
                z.reshape(B, L, IN_COLS), cos, sin, w["conv_w"], w["pool_w"], w["pool_scale"],
                caches[0], caches[1], caches[2], state_out, i, start_pos)
            x2 = _out_ln(y.reshape(T, D), x1, w["w_out"], w["ln2_g"], w["ln2_b"], i)
        x = _ffn_ln(x2, w["ffn2_w_gate"], w["ffn2_w_up"], w["ffn2_w_down"], w["ln3_g"], w["ln3_b"], i,
                    ple=(w["ple_gate"], p, w["ple_proj"]))
        convs.append(c_new)
        pools.append(p_new)
    ret = jnp.stack(rets) if caches is None else state_out
    return x.reshape(B, L, D), jnp.stack(convs), jnp.stack(pools), ret


def kernel(x_prompt, x_sample, p_prompt, p_sample, cache_conv, cache_pool, state_ret, ln1_g, ln1_b, ffn1_w_gate, ffn1_w_up, ffn1_w_down, w_in, conv_w, pool_w, pool_scale, w_out, ln2_g, ln2_b, ffn2_w_gate, ffn2_w_up, ffn2_w_down, ple_gate, ple_proj, ln3_g, ln3_b):
    precast = dict(w_in=w_in, pool_w=pool_w, w_out=w_out, ple_gate=ple_gate, ple_proj=ple_proj)
    row_params = dict(ln1_g=ln1_g, ln1_b=ln1_b, ln2_g=ln2_g, ln2_b=ln2_b, ln3_g=ln3_g, ln3_b=ln3_b,
                      pool_scale=pool_scale)
    w = {k: _to_bf16(v) for k, v in precast.items()}
    w.update({k: v.reshape(DEPTH, 1, -1) for k, v in row_params.items()})
    w.update(conv_w=conv_w, ffn1_w_gate=ffn1_w_gate, ffn1_w_up=ffn1_w_up, ffn1_w_down=ffn1_w_down,
             ffn2_w_gate=ffn2_w_gate, ffn2_w_up=ffn2_w_up, ffn2_w_down=ffn2_w_down)
    y_prompt, conv_p, pool_p, ret_p = _trunk(x_prompt, p_prompt, None, 0, w)
    y_sample, conv_s, pool_s, ret_s = _trunk(
        x_sample, p_sample, (cache_conv, cache_pool, state_ret), PAST_LEN, w)
    return (y_prompt, y_sample, conv_p, pool_p, ret_p, conv_s, pool_s, ret_s)
```

```python
import functools
import math

import jax
import jax.numpy as jnp
from jax import lax
from jax.experimental import pallas as pl
from jax.experimental.pallas import tpu as pltpu

D_MODEL = 2048
DEPTH = 2
PAST_LEN = 16384
CONV_WIDTH = D_MODEL // 4
POOL_WIDTH = D_MODEL // 4
RET_WIDTH = D_MODEL // 2
CONV_K = 3
POOL_WINDOWS = (2, 4, 8, 16)
POOL_GROUP = POOL_WIDTH // len(POOL_WINDOWS)
POOL_HIST = max(POOL_WINDOWS) - 1
RET_HEADS = 4
RET_HEAD_DIM = RET_WIDTH // RET_HEADS
RET_CHUNK = 128
RET_LOG_GAMMA = tuple(math.log(1.0 - 2.0 ** (-5 - h)) for h in range(RET_HEADS))
ROPE_BASE = 10000.0
IN_COLS = 3 * CONV_WIDTH + POOL_WIDTH + 4 * RET_WIDTH
DEEPNORM_ALPHA = (2 * DEPTH) ** 0.25
LN_EPS = 1e-5

COL_B = 0
COL_C = CONV_WIDTH
COL_H = 2 * CONV_WIDTH
COL_P = 3 * CONV_WIDTH
COL_Q = COL_P + POOL_WIDTH
COL_K = COL_Q + RET_WIDTH
COL_V = COL_K + RET_WIDTH
COL_G = COL_V + RET_WIDTH
OUT_POOL = CONV_WIDTH
OUT_RET = CONV_WIDTH + POOL_WIDTH

SUBLANES = 8
CONV_PAD = SUBLANES
POOL_PAD = 16

F32 = jnp.float32
BF16 = jnp.bfloat16
MIB = 1024 * 1024


def _params(semantics, vmem_mib):
    return pltpu.CompilerParams(dimension_semantics=semantics, vmem_limit_bytes=vmem_mib * MIB)


def _layer_norm(r, g, b):
    mu = jnp.mean(r, axis=-1, keepdims=True)
    c = r - mu
    var = jnp.mean(c * c, axis=-1, keepdims=True)
    return c * lax.rsqrt(var + LN_EPS) * g + b


def _silu(x):
    return x * jax.nn.sigmoid(x)


def _dot(a, b):
    return jnp.dot(a, b, preferred_element_type=F32)


def _cast_kernel(x_ref, o_ref):
    o_ref[...] = x_ref[...].astype(o_ref.dtype)


def _to_bf16(w, block_bytes=6 * MIB):
    shape = w.shape
    w = w.reshape(shape[0], -1, shape[-1])
    _, R, C = w.shape
    tr = R
    while tr * C * 4 > block_bytes and tr % 2 == 0 and (tr // 2) % 16 == 0:
        tr //= 2
    out = pl.pallas_call(
        _cast_kernel,
        grid=(shape[0], R // tr),
        in_specs=[pl.BlockSpec((None, tr, C), lambda d, r: (d, r, 0))],
        out_specs=pl.BlockSpec((None, tr, C), lambda d, r: (d, r, 0)),
        out_shape=jax.ShapeDtypeStruct(w.shape, BF16),
        compiler_params=_params(("parallel", "parallel"), 40),
        name="to_bf16",
    )(w)
    return out.reshape(shape)


FFN_PACK = 256


def _ffn_ln_kernel(*refs, nf, tf, packed, ple_cols, emit_bf16):
    refs = list(refs)
    x_ref = refs.pop(0)
    if packed:
        wgu_ref, wd_ref = refs[:2]
        refs = refs[2:]
    else:
        wg_ref, wu_ref, wd_ref = refs[:3]
        refs = refs[3:]
    g_ref, b_ref = refs[:2]
    refs = refs[2:]
    if ple_cols:
        gate_ref, p_ref, proj_ref = refs[:3]
        refs = refs[3:]
    o_ref = refs.pop(0)
    ob_ref = refs.pop(0) if emit_bf16 else None
    if not packed:
        wgu_out_ref, wd_out_ref = refs[:2]
        refs = refs[2:]
    xb_ref = refs.pop(0)
    f = pl.program_id(1)
    n_ple = o_ref.shape[1] // ple_cols if ple_cols else 0

    @pl.when(f == 0)
    def _():
        x = x_ref[...]
        xb_ref[...] = x.astype(BF16)
        o_ref[...] = DEEPNORM_ALPHA * x

    @pl.when(f < nf)
    def _():
        xb = xb_ref[...]
        if packed:
            wd = wd_ref[...]
            hs = []
            for s in range(tf // FFN_PACK):
                c0 = 2 * FFN_PACK * s
                gate = _dot(xb, wgu_ref[:, c0:c0 + FFN_PACK])
                up = _dot(xb, wgu_ref[:, c0 + FFN_PACK:c0 + 2 * FFN_PACK])
                hs.append((_silu(gate) * up).astype(BF16))
            h = jnp.concatenate(hs, axis=1)
        else:
            wg = wg_ref[...].astype(BF16)
            wu = wu_ref[...].astype(BF16)
            wd = wd_ref[...].astype(BF16)
            wgu_out_ref[...] = jnp.concatenate([wg, wu], axis=1)
            wd_out_ref[...] = wd
            h = (_silu(_dot(xb, wg)) * _dot(xb, wu)).astype(BF16)
        o_ref[...] += 0.5 * _dot(h, wd)

    for j in range(n_ple):
        @pl.when(f == nf + j)
        def _():
            gate = jax.nn.sigmoid(_dot(xb_ref[...], gate_ref[...]))
            o_ref[:, j * ple_cols:(j + 1) * ple_cols] += gate * _dot(p_ref[...].astype(BF16), proj_ref[...])

    @pl.when(f == nf + n_ple - 1)
    def _():
        y = _layer_norm(o_ref[...], g_ref[...], b_ref[...])
        o_ref[...] = y
        if emit_bf16:
            ob_ref[...] = y.astype(BF16)


def _ffn_ln(x, weights, ln_g, ln_b, layer, ple=None, emit_bf16=False, tm=1024, packed_tf=512, ple_cols=512):
    T, D = x.shape
    packed = len(weights) == 2
    tf = packed_tf if packed else FFN_PACK
    F = weights[1].shape[0] if packed else weights[0].shape[2]
    nf = F // tf
    n_ple = D // ple_cols if ple is not None else 0
    row = lambda i, f: (i, 0)
    const = lambda i, f: (layer, 0, 0)
    fclamp = lambda f: jnp.minimum(f, nf - 1)
    if packed:
        w_specs = [pl.BlockSpec((D, 2 * tf), lambda i, f: (0, fclamp(f))),
                   pl.BlockSpec((tf, D), lambda i, f: (fclamp(f), 0))]
        single = None
    else:
        assert T == tm
        w_specs = [pl.BlockSpec((None, D, tf), lambda i, f: (layer, 0, fclamp(f))),
                   pl.BlockSpec((None, D, tf), lambda i, f: (layer, 0, fclamp(f))),
                   pl.BlockSpec((None, tf, D), lambda i, f: (layer, fclamp(f), 0))]
        single = pl.Buffered(1)
    in_specs = [pl.BlockSpec((tm, D), row, pipeline_mode=pl.Buffered(1))] + w_specs + [
        pl.BlockSpec((None, 1, D), const),
        pl.BlockSpec((None, 1, D), const),
    ]
    args = [x, *weights, ln_g, ln_b]
    if ple is not None:
        gate_w, p, proj_w = ple
        P = p.shape[2]
        pcol = lambda i, f: (layer, 0, jnp.maximum(f - nf, 0))
        in_specs += [
            pl.BlockSpec((None, D, ple_cols), pcol),
            pl.BlockSpec((None, tm, P), lambda i, f: (layer, i, 0)),
            pl.BlockSpec((None, P, ple_cols), pcol),
        ]
        args += [gate_w, p, proj_w]
    out_shape = [jax.ShapeDtypeStruct((T, D), F32)]
    out_specs = [pl.BlockSpec((tm, D), row, pipeline_mode=single)]
    if emit_bf16:
        out_shape.append(jax.ShapeDtypeStruct((T, D), BF16))
        out_specs.append(pl.BlockSpec((tm, D), row, pipeline_mode=single))
    if not packed:
        out_shape += [jax.ShapeDtypeStruct((D, 2 * F), BF16), jax.ShapeDtypeStruct((F, D), BF16)]
        out_specs += [pl.BlockSpec((D, 2 * tf), lambda i, f: (0, fclamp(f))),
                      pl.BlockSpec((tf, D), lambda i, f: (fclamp(f), 0))]
    return pl.pallas_call(
        functools.partial(_ffn_ln_kernel, nf=nf, tf=tf, packed=packed,
                          ple_cols=ple_cols if ple is not None else 0, emit_bf16=emit_bf16),
        grid=(T // tm, nf + n_ple),
        in_specs=in_specs,
        out_specs=out_specs,
        out_shape=out_shape,
        scratch_shapes=[pltpu.VMEM((tm, D), BF16)],
        compiler_params=_params(("parallel", "arbitrary"), 58),
        name="ffn_ln",
    )(*args)


def _proj_in_kernel(x_ref, w_ref, o_ref):
    o_ref[...] = _dot(x_ref[...], w_ref[...])


def _proj_in(xb, w, layer, tm=1024, tn=1024):
    T, D = xb.shape
    N = w.shape[2]
    return pl.pallas_call(
        _proj_in_kernel,
        grid=(T // tm, N // tn),
        in_specs=[pl.BlockSpec((tm, D), lambda i, n: (i, 0)),
                  pl.BlockSpec((None, D, tn), lambda i, n: (layer, 0, n))],
        out_specs=pl.BlockSpec((tm, tn), lambda i, n: (i, n)),
        out_shape=jax.ShapeDtypeStruct((T, N), F32),
        compiler_params=_params(("parallel", "arbitrary"), 40),
        name="proj_in",
    )(xb, w)


def _rotary(x, cos, sin):
    half = RET_HEAD_DIM // 2
    x1, x2 = x[:, :half], x[:, half:]
    return jnp.concatenate([x1 * cos - x2 * sin, x1 * sin + x2 * cos], axis=-1)


def _retention_tables(L):
    row = lax.broadcasted_iota(jnp.int32, (L, L), 0)
    col = lax.broadcasted_iota(jnp.int32, (L, L), 1)
    diff = (row - col).astype(F32)
    causal = row >= col
    idx = lax.broadcasted_iota(jnp.int32, (L, 1), 0).astype(F32)
    tables = []
    for lg in RET_LOG_GAMMA:
        decay = jnp.where(causal, jnp.exp(jnp.where(causal, diff, 0.0) * lg), 0.0)
        q_decay = jnp.exp((idx + 1.0) * lg)
        k_decay = jnp.exp((L - 1.0 - idx) * lg)
        tables.append((decay, q_decay, k_decay, math.exp(L * lg)))
    return tables


def _retention_head(q, k, v, gate, S, cos, sin, table):
    decay, q_decay, k_decay, s_decay = table
    qr = _rotary(q, cos, sin)
    kr = _rotary(k, cos, sin) * (RET_HEAD_DIM ** -0.5)
    qb = qr.astype(BF16)
    vb = v.astype(BF16)
    scores = lax.dot_general(qb, kr.astype(BF16), (((1,), (1,)), ((), ())), preferred_element_type=F32)
    inner = _dot((scores * decay).astype(BF16), vb)
    cross = _dot(qb, S.astype(BF16)) * q_decay
    o = inner + cross
    kd = (kr * k_decay).astype(BF16)
    S_new = s_decay * S + lax.dot_general(kd, vb, (((0,), (0,)), ((), ())), preferred_element_type=F32)
    mu = jnp.mean(o, axis=-1, keepdims=True)
    c = o - mu
    var = jnp.mean(c * c, axis=-1, keepdims=True)
    y = c * lax.rsqrt(var + LN_EPS) * _silu(gate)
    return y, S_new


def _window_count(w, pos):
    return jnp.minimum(jnp.float32(w), pos + 1.0)


def _mixer_prompt_kernel(z_ref, cos_ref, sin_ref, convw_ref, poolw_ref, pscale_ref,
                         x_ref, wout_ref, g_ref, b_ref,
                         o_ref, convn_ref, pooln_ref, retn_ref,
                         y_ref, s_ref, ubuf_ref, pbuf_ref, *, tl, nl):
    l = pl.program_id(1)

    @pl.when(l == 0)
    def _():
        s_ref[...] = jnp.zeros_like(s_ref)
        ubuf_ref[0:CONV_PAD, :] = jnp.zeros((CONV_PAD, CONV_WIDTH), F32)
        pbuf_ref[0:POOL_PAD, :] = jnp.zeros((POOL_PAD, POOL_WIDTH), F32)

    u = z_ref[:, COL_C:COL_C + CONV_WIDTH] * z_ref[:, COL_H:COL_H + CONV_WIDTH]
    ubuf_ref[CONV_PAD:CONV_PAD + tl, :] = u
    cw = convw_ref[...]
    conv = (ubuf_ref[CONV_PAD - 2:CONV_PAD - 2 + tl, :] * cw[0:1, :]
            + ubuf_ref[CONV_PAD - 1:CONV_PAD - 1 + tl, :] * cw[1:2, :]
            + u * cw[2:3, :])
    y_ref[:, 0:CONV_WIDTH] = (z_ref[:, COL_B:COL_B + CONV_WIDTH] * conv).astype(y_ref.dtype)

    pbuf_ref[POOL_PAD:POOL_PAD + tl, :] = z_ref[:, COL_P:COL_P + POOL_WIDTH]
    pos = (l * tl + lax.broadcasted_iota(jnp.int32, (tl, 1), 0)).astype(F32)
    for gi, w in enumerate(POOL_WINDOWS):
        c0 = gi * POOL_GROUP
        tok = pbuf_ref[POOL_PAD:POOL_PAD + tl, c0:c0 + POOL_GROUP]
        win = tok
        for j in range(1, w):
            win = win + pbuf_ref[POOL_PAD - j:POOL_PAD - j + tl, c0:c0 + POOL_GROUP]
        d = win / _window_count(w, pos) - tok
        yb = _dot(d.astype(BF16), poolw_ref[gi]) * pscale_ref[:, c0:c0 + POOL_GROUP]
        y_ref[:, OUT_POOL + c0:OUT_POOL + c0 + POOL_GROUP] = yb.astype(y_ref.dtype)

    tables = _retention_tables(RET_CHUNK)
    for c in range(tl // RET_CHUNK):
        r0 = c * RET_CHUNK
        cos = cos_ref[r0:r0 + RET_CHUNK, :]
        sin = sin_ref[r0:r0 + RET_CHUNK, :]
        for h in range(RET_HEADS):
            h0 = h * RET_HEAD_DIM
            y, s_new = _retention_head(
                z_ref[r0:r0 + RET_CHUNK, COL_Q + h0:COL_Q + h0 + RET_HEAD_DIM],
                z_ref[r0:r0 + RET_CHUNK, COL_K + h0:COL_K + h0 + RET_HEAD_DIM],
                z_ref[r0:r0 + RET_CHUNK, COL_V + h0:COL_V + h0 + RET_HEAD_DIM],
                z_ref[r0:r0 + RET_CHUNK, COL_G + h0:COL_G + h0 + RET_HEAD_DIM],
                s_ref[h], cos, sin, tables[h])
            s_ref[h] = s_new
            y_ref[r0:r0 + RET_CHUNK, OUT_RET + h0:OUT_RET + h0 + RET_HEAD_DIM] = y.astype(y_ref.dtype)
        m = _dot(y_ref[r0:r0 + RET_CHUNK, :], wout_ref[...])
        o_ref[r0:r0 + RET_CHUNK, :] = _layer_norm(
            DEEPNORM_ALPHA * x_ref[r0:r0 + RET_CHUNK, :] + m, g_ref[...], b_ref[...])

    @pl.when(l == nl - 1)
    def _():
        convn_ref[0] = ubuf_ref[CONV_PAD + tl - (CONV_K - 1):CONV_PAD + tl, :]
        pooln_ref[0] = pbuf_ref[POOL_PAD + tl - POOL_HIST:POOL_PAD + tl, :]
        retn_ref[0] = s_ref[...]

    ubuf_ref[0:CONV_PAD, :] = ubuf_ref[tl:tl + CONV_PAD, :]
    pbuf_ref[0:POOL_PAD, :] = pbuf_ref[tl:tl + POOL_PAD, :]


def _mixer_prompt(z, x1, cos, sin, conv_w, pool_w, pool_scale, w_out, ln_g, ln_b, layer, batch, tl=256):
    T = z.shape[0]
    L = T // batch
    nl = L // tl
    row = lambda b, l: (b * nl + l, 0)
    lay3 = lambda b, l: (layer, 0, 0)
    return pl.pallas_call(
        functools.partial(_mixer_prompt_kernel, tl=tl, nl=nl),
        grid=(batch, nl),
        in_specs=[
            pl.BlockSpec((tl, IN_COLS), row),
            pl.BlockSpec((tl, RET_HEAD_DIM // 2), lambda b, l: (l, 0)),
            pl.BlockSpec((tl, RET_HEAD_DIM // 2), lambda b, l: (l, 0)),
            pl.BlockSpec((None, CONV_K, CONV_WIDTH), lay3),
            pl.BlockSpec((None, len(POOL_WINDOWS), POOL_GROUP, POOL_GROUP), lambda b, l: (layer, 0, 0, 0)),
            pl.BlockSpec((None, 1, POOL_WIDTH), lay3),
            pl.BlockSpec((tl, D_MODEL), row),
            pl.BlockSpec((None, D_MODEL, D_MODEL), lay3, pipeline_mode=pl.Buffered(1)),
            pl.BlockSpec((None, 1, D_MODEL), lay3),
            pl.BlockSpec((None, 1, D_MODEL), lay3),
        ],
        out_specs=[
            pl.BlockSpec((tl, D_MODEL), row),
            pl.BlockSpec((1, CONV_K - 1, CONV_WIDTH), lambda b, l: (b, 0, 0)),
            pl.BlockSpec((1, POOL_HIST, POOL_WIDTH), lambda b, l: (b, 0, 0)),
            pl.BlockSpec((1, RET_HEADS, RET_HEAD_DIM, RET_HEAD_DIM), lambda b, l: (b, 0, 0, 0)),
        ],
        out_shape=[
            jax.ShapeDtypeStruct((T, D_MODEL), F32),
            jax.ShapeDtypeStruct((batch, CONV_K - 1, CONV_WIDTH), F32),
            jax.ShapeDtypeStruct((batch, POOL_HIST, POOL_WIDTH), F32),
            jax.ShapeDtypeStruct((batch, RET_HEADS, RET_HEAD_DIM, RET_HEAD_DIM), F32),
        ],
        scratch_shapes=[
            pltpu.VMEM((tl, D_MODEL), BF16),
            pltpu.VMEM((RET_HEADS, RET_HEAD_DIM, RET_HEAD_DIM), F32),
            pltpu.VMEM((CONV_PAD + tl, CONV_WIDTH), F32),
            pltpu.VMEM((POOL_PAD + tl, POOL_WIDTH), F32),
        ],
        compiler_params=_params(("parallel", "arbitrary"), 48),
        name="mixer_prompt",
    )(z, cos, sin, conv_w, pool_w, pool_scale, x1, w_out, ln_g, ln_b)


def _mixer_sample_kernel(z_ref, cos_ref, sin_ref, convw_ref, poolw_ref, pscale_ref,
                         convc_ref, poolc_ref, state_ref, *rest, nb, ls, start_pos, aliased):
    if aliased:
        rest = rest[1:]
    y_ref, convn_ref, pooln_ref, staten_ref, ubuf_ref, pbuf_ref = rest
    u = z_ref[:, :, COL_C:COL_C + CONV_WIDTH] * z_ref[:, :, COL_H:COL_H + CONV_WIDTH]
    ubuf_ref[:, CONV_PAD - (CONV_K - 1):CONV_PAD, :] = convc_ref[...]
    ubuf_ref[:, CONV_PAD:CONV_PAD + ls, :] = u
    cw = convw_ref[...]
    conv = (ubuf_ref[:, CONV_PAD - 2:CONV_PAD - 2 + ls, :] * cw[0:1, :]
            + ubuf_ref[:, CONV_PAD - 1:CONV_PAD - 1 + ls, :] * cw[1:2, :]
            + u * cw[2:3, :])
    y_ref[:, :, 0:CONV_WIDTH] = z_ref[:, :, COL_B:COL_B + CONV_WIDTH] * conv
    convn_ref[...] = ubuf_ref[:, CONV_PAD + ls - (CONV_K - 1):CONV_PAD + ls, :]

    pbuf_ref[:, POOL_PAD - POOL_HIST:POOL_PAD, :] = poolc_ref[...]
    pbuf_ref[:, POOL_PAD:POOL_PAD + ls, :] = z_ref[:, :, COL_P:COL_P + POOL_WIDTH]
    pos = (start_pos + lax.broadcasted_iota(jnp.int32, (1, ls, 1), 1)).astype(F32)
    for gi, w in enumerate(POOL_WINDOWS):
        c0 = gi * POOL_GROUP
        tok = pbuf_ref[:, POOL_PAD:POOL_PAD + ls, c0:c0 + POOL_GROUP]
        win = tok
        for j in range(1, w):
            win = win + pbuf_ref[:, POOL_PAD - j:POOL_PAD - j + ls, c0:c0 + POOL_GROUP]
        d = win / _window_count(w, pos) - tok
        yb = _dot(d.reshape(nb * ls, POOL_GROUP).astype(BF16), poolw_ref[gi]).reshape(nb, ls, POOL_GROUP)
        y_ref[:, :, OUT_POOL + c0:OUT_POOL + c0 + POOL_GROUP] = yb * pscale_ref[:, c0:c0 + POOL_GROUP]
    pooln_ref[...] = pbuf_ref[:, POOL_PAD + ls - POOL_HIST:POOL_PAD + ls, :]

    tables = _retention_tables(ls)
    cos = cos_ref[...]
    sin = sin_ref[...]

    def per_sequence(b, carry):
        for h in range(RET_HEADS):
            h0 = h * RET_HEAD_DIM
            y, s_new = _retention_head(
                z_ref[b, :, COL_Q + h0:COL_Q + h0 + RET_HEAD_DIM],
                z_ref[b, :, COL_K + h0:COL_K + h0 + RET_HEAD_DIM],
                z_ref[b, :, COL_V + h0:COL_V + h0 + RET_HEAD_DIM],
                z_ref[b, :, COL_G + h0:COL_G + h0 + RET_HEAD_DIM],
                state_ref[b, h], cos, sin, tables[h])
            staten_ref[b, h] = s_new
            y_ref[b, :, OUT_RET + h0:OUT_RET + h0 + RET_HEAD_DIM] = y
        return carry

    lax.fori_loop(0, nb, per_sequence, 0)


def _mixer_sample(z, cos, sin, conv_w, pool_w, pool_scale, conv_cache, pool_cache, state, state_out,
                  layer, start_pos, nb=8):
    B, ls, _ = z.shape
    seq3 = lambda i: (i, 0, 0)
    lay3 = lambda i: (layer, 0, 0)
    lseq4 = lambda i: (layer, i, 0, 0)
    lseq5 = lambda i: (layer, i, 0, 0, 0)
    in_specs = [
        pl.BlockSpec((nb, ls, IN_COLS), seq3),
        pl.BlockSpec((ls, RET_HEAD_DIM // 2), lambda i: (0, 0)),
        pl.BlockSpec((ls, RET_HEAD_DIM // 2), lambda i: (0, 0)),
        pl.BlockSpec((None, CONV_K, CONV_WIDTH), lay3),
        pl.BlockSpec((None, len(POOL_WINDOWS), POOL_GROUP, POOL_GROUP), lambda i: (layer, 0, 0, 0)),
        pl.BlockSpec((None, 1, POOL_WIDTH), lay3),
        pl.BlockSpec((None, nb, CONV_K - 1, CONV_WIDTH), lseq4),
        pl.BlockSpec((None, nb, POOL_HIST, POOL_WIDTH), lseq4),
        pl.BlockSpec((None, nb, RET_HEADS, RET_HEAD_DIM, RET_HEAD_DIM), lseq5),
    ]
    args = [z, cos, sin, conv_w, pool_w, pool_scale, conv_cache, pool_cache, state]
    aliases = {}
    if state_out is not None:
        in_specs.append(pl.BlockSpec(memory_space=pl.ANY))
        args.append(state_out)
        aliases = {len(args) - 1: 3}
    return pl.pallas_call(
        functools.partial(_mixer_sample_kernel, nb=nb, ls=ls, start_pos=start_pos,
                          aliased=state_out is not None),
        grid=(B // nb,),
        in_specs=in_specs,
        out_specs=[
            pl.BlockSpec((nb, ls, D_MODEL), seq3),
            pl.BlockSpec((nb, CONV_K - 1, CONV_WIDTH), seq3),
            pl.BlockSpec((nb, POOL_HIST, POOL_WIDTH), seq3),
            pl.BlockSpec((None, nb, RET_HEADS, RET_HEAD_DIM, RET_HEAD_DIM), lseq5),
        ],
        out_shape=[
            jax.ShapeDtypeStruct((B, ls, D_MODEL), F32),
            jax.ShapeDtypeStruct((B, CONV_K - 1, CONV_WIDTH), F32),
            jax.ShapeDtypeStruct((B, POOL_HIST, POOL_WIDTH), F32),
            jax.ShapeDtypeStruct(state.shape, F32),
        ],
        scratch_shapes=[
            pltpu.VMEM((nb, CONV_PAD + ls, CONV_WIDTH), F32),
            pltpu.VMEM((nb, POOL_PAD + ls, POOL_WIDTH), F32),
        ],
        input_output_aliases=aliases,
        compiler_params=_params(("parallel",), 48),
        name="mixer_sample",
    )(*args)


def _out_ln_kernel(y_ref, x_ref, w_ref, g_ref, b_ref, o_ref):
    m = _dot(y_ref[...].astype(BF16), w_ref[...])
    o_ref[...] = _layer_norm(DEEPNORM_ALPHA * x_ref[...] + m, g_ref[...], b_ref[...])


def _out_ln(y, x, w, ln_g, ln_b, layer, tm=256):
    T, D = x.shape
    row = lambda i: (i, 0)
    const = lambda i: (layer, 0, 0)
    return pl.pallas_call(
        _out_ln_kernel,
        grid=(T // tm,),
        in_specs=[
            pl.BlockSpec((tm, y.shape[1]), row),
            pl.BlockSpec((tm, D), row),
            pl.BlockSpec((None,) + w.shape[1:], const),
            pl.BlockSpec((None, 1, D), const),
            pl.BlockSpec((None, 1, D), const),
        ],
        out_specs=pl.BlockSpec((tm, D), row),
        out_shape=jax.ShapeDtypeStruct((T, D), F32),
        compiler_params=_params(("parallel",), 48),
        name="out_ln",
    )(y, x, w, ln_g, ln_b)


def _rope_tables(start_pos, length):
    half = RET_HEAD_DIM // 2
    inv = ROPE_BASE ** (-jnp.arange(half, dtype=F32) / half)
    pos = start_pos + jnp.arange(length, dtype=F32)
    ang = pos[:, None] * inv[None, :]
    return jnp.cos(ang), jnp.sin(ang)


def _trunk(x, p, caches, start_pos, w, packed):
    B, L, D = x.shape
    T = B * L
    x = x.reshape(T, D)
    p = p.reshape(DEPTH, T, -1)
    cos, sin = _rope_tables(start_pos, L)
    convs, pools, rets = [], [], []
    state_out = None
    collect = not packed

    def ffn(x, name, ln, layer, **kw):
        if collect:
            *outs, wgu, wd = _ffn_ln(x, (w[name + "_w_gate"], w[name + "_w_up"], w[name + "_w_down"]),
                                     w[ln + "_g"], w[ln + "_b"], layer, **kw)
            packed[name, layer] = (wgu, wd)
            return outs
        return _ffn_ln(x, packed[name, layer], w[ln + "_g"], w[ln + "_b"], layer, **kw)

    for i in range(DEPTH):
        x1, x1b = ffn(x, "ffn1", "ln1", i, emit_bf16=True)
        z = _proj_in(x1b, w["w_in"], i)
        if caches is None:
            x2, c_new, p_new, r_new = _mixer_prompt(
                z, x1, cos, sin, w["conv_w"], w["pool_w"], w["pool_scale"], w["w_out"], w["ln2_g"], w["ln2_b"],
                i, B)
            rets.append(r_new)
        else:
            y, c_new, p_new, state_out = _mixer_sample(
                z.reshape(B, L, IN_COLS), cos, sin, w["conv_w"], w["pool_w"], w["pool_scale"],
                caches[0], caches[1], caches[2], state_out, i, start_pos)
            x2 = _out_ln(y.reshape(T, D), x1, w["w_out"], w["ln2_g"], w["ln2_b"], i)
        x, = ffn(x2, "ffn2", "ln3", i, ple=(w["ple_gate"], p, w["ple_proj"]))
        convs.append(c_new)
        pools.append(p_new)
    ret = jnp.stack(rets) if caches is None else state_out
    return x.reshape(B, L, D), jnp.stack(convs), jnp.stack(pools), ret


def kernel(x_prompt, x_sample, p_prompt, p_sample, cache_conv, cache_pool, state_ret, ln1_g, ln1_b, ffn1_w_gate, ffn1_w_up, ffn1_w_down, w_in, conv_w, pool_w, pool_scale, w_out, ln2_g, ln2_b, ffn2_w_gate, ffn2_w_up, ffn2_w_down, ple_gate, ple_proj, ln3_g, ln3_b):
    precast = dict(w_in=w_in, pool_w=pool_w, w_out=w_out, ple_gate=ple_gate, ple_proj=ple_proj)
    row_params = dict(ln1_g=ln1_g, ln1_b=ln1_b, ln2_g=ln2_g, ln2_b=ln2_b, ln3_g=ln3_g, ln3_b=ln3_b,
                      pool_scale=pool_scale)
    w = {k: _to_bf16(v) for k, v in precast.items()}
    w.update({k: v.reshape(DEPTH, 1, -1) for k, v in row_params.items()})
    w.update(conv_w=conv_w, ffn1_w_gate=ffn1_w_gate, ffn1_w_up=ffn1_w_up, ffn1_w_down=ffn1_w_down,
             ffn2_w_gate=ffn2_w_gate, ffn2_w_up=ffn2_w_up, ffn2_w_down=ffn2_w_down)
    packed = {}
    y_sample, conv_s, pool_s, ret_s = _trunk(
        x_sample, p_sample, (cache_conv, cache_pool, state_ret), PAST_LEN, w, packed)
    y_prompt, conv_p, pool_p, ret_p = _trunk(x_prompt, p_prompt, None, 0, w, packed)
    return (y_prompt, y_sample, conv_p, pool_p, ret_p, conv_s, pool_s, ret_s)
```

```python
import functools
import math

import jax
import jax.numpy as jnp
from jax import lax
from jax.experimental import pallas as pl
from jax.experimental.pallas import tpu as pltpu

D_MODEL = 2048
DEPTH = 2
PAST_LEN = 16384
CONV_WIDTH = D_MODEL // 4
POOL_WIDTH = D_MODEL // 4
RET_WIDTH = D_MODEL // 2
CONV_K = 3
POOL_WINDOWS = (2, 4, 8, 16)
POOL_GROUP = POOL_WIDTH // len(POOL_WINDOWS)
POOL_HIST = max(POOL_WINDOWS) - 1
RET_HEADS = 4
RET_HEAD_DIM = RET_WIDTH // RET_HEADS
RET_CHUNK = 128
RET_LOG_GAMMA = tuple(math.log(1.0 - 2.0 ** (-5 - h)) for h in range(RET_HEADS))
ROPE_BASE = 10000.0
IN_COLS = 3 * CONV_WIDTH + POOL_WIDTH + 4 * RET_WIDTH
DEEPNORM_ALPHA = (2 * DEPTH) ** 0.25
LN_EPS = 1e-5

COL_B = 0
COL_C = CONV_WIDTH
COL_H = 2 * CONV_WIDTH
COL_P = 3 * CONV_WIDTH
COL_Q = COL_P + POOL_WIDTH
COL_K = COL_Q + RET_WIDTH
COL_V = COL_K + RET_WIDTH
COL_G = COL_V + RET_WIDTH
OUT_POOL = CONV_WIDTH
OUT_RET = CONV_WIDTH + POOL_WIDTH

SUBLANES = 8
CONV_PAD = SUBLANES
POOL_PAD = 16

F32 = jnp.float32
BF16 = jnp.bfloat16
MIB = 1024 * 1024


def _params(semantics, vmem_mib):
    return pltpu.CompilerParams(dimension_semantics=semantics, vmem_limit_bytes=vmem_mib * MIB)


def _layer_norm(r, g, b):
    mu = jnp.mean(r, axis=-1, keepdims=True)
    c = r - mu
    var = jnp.mean(c * c, axis=-1, keepdims=True)
    return c * lax.rsqrt(var + LN_EPS) * g + b


def _silu(x):
    return x * jax.nn.sigmoid(x)


def _dot(a, b):
    return jnp.dot(a, b, preferred_element_type=F32)


def _cast_kernel(x_ref, o_ref):
    o_ref[...] = x_ref[...].astype(o_ref.dtype)


def _to_bf16(w, block_bytes=6 * MIB):
    shape = w.shape
    w = w.reshape(shape[0], -1, shape[-1])
    _, R, C = w.shape
    tr = R
    while tr * C * 4 > block_bytes and tr % 2 == 0 and (tr // 2) % 16 == 0:
        tr //= 2
    out = pl.pallas_call(
        _cast_kernel,
        grid=(shape[0], R // tr),
        in_specs=[pl.BlockSpec((None, tr, C), lambda d, r: (d, r, 0))],
        out_specs=pl.BlockSpec((None, tr, C), lambda d, r: (d, r, 0)),
        out_shape=jax.ShapeDtypeStruct(w.shape, BF16),
        compiler_params=_params(("parallel", "parallel"), 40),
        name="to_bf16",
    )(w)
    return out.reshape(shape)


FFN_PACK = 256
FFN_ROWS = 512


def _ffn_ln_kernel(*refs, nf, tf, packed, ple_cols, emit_bf16):
    refs = list(refs)
    x_ref = refs.pop(0)
    if packed:
        wgu_ref, wd_ref = refs[:2]
        refs = refs[2:]
    else:
        wg_ref, wu_ref, wd_ref = refs[:3]
        refs = refs[3:]
    g_ref, b_ref = refs[:2]
    refs = refs[2:]
    if ple_cols:
        gate_ref, p_ref, proj_ref = refs[:3]
        refs = refs[3:]
    o_ref = refs.pop(0)
    ob_ref = refs.pop(0) if emit_bf16 else None
    if not packed:
        wgu_out_ref, wd_out_ref = refs[:2]
        refs = refs[2:]
    xb_ref = refs.pop(0)
    f = pl.program_id(1)
    n_ple = o_ref.shape[1] // ple_cols if ple_cols else 0

    @pl.when(f == 0)
    def _():
        x = x_ref[...]
        xb_ref[...] = x.astype(BF16)
        o_ref[...] = DEEPNORM_ALPHA * x

    @pl.when(f < nf)
    def _():
        if packed:
            wd = wd_ref[...]
        else:
            wg = wg_ref[...].astype(BF16)
            wu = wu_ref[...].astype(BF16)
            wd = wd_ref[...].astype(BF16)
            wgu_out_ref[...] = jnp.concatenate([wg, wu], axis=1)
            wd_out_ref[...] = wd
        for r0 in range(0, o_ref.shape[0], FFN_ROWS):
            xb = xb_ref[r0:r0 + FFN_ROWS, :]
            if packed:
                hs = []
                for s in range(tf // FFN_PACK):
                    c0 = 2 * FFN_PACK * s
                    gu = _dot(xb, wgu_ref[:, c0:c0 + 2 * FFN_PACK])
                    hs.append((_silu(gu[:, :FFN_PACK]) * gu[:, FFN_PACK:]).astype(BF16))
                h = jnp.concatenate(hs, axis=1)
            else:
                h = (_silu(_dot(xb, wg)) * _dot(xb, wu)).astype(BF16)
            o_ref[r0:r0 + FFN_ROWS, :] += 0.5 * _dot(h, wd)

    for j in range(n_ple):
        @pl.when(f == nf + j)
        def _():
            gate = jax.nn.sigmoid(_dot(xb_ref[...], gate_ref[...]))
            o_ref[:, j * ple_cols:(j + 1) * ple_cols] += gate * _dot(p_ref[...].astype(BF16), proj_ref[...])

    @pl.when(f == nf + n_ple - 1)
    def _():
        y = _layer_norm(o_ref[...], g_ref[...], b_ref[...])
        o_ref[...] = y
        if emit_bf16:
            ob_ref[...] = y.astype(BF16)


def _ffn_ln(x, weights, ln_g, ln_b, layer, ple=None, emit_bf16=False, tm=1024, packed_tf=512, ple_cols=512):
    T, D = x.shape
    packed = len(weights) == 2
    tf = packed_tf if packed else FFN_PACK
    F = weights[1].shape[0] if packed else weights[0].shape[2]
    nf = F // tf
    n_ple = D // ple_cols if ple is not None else 0
    row = lambda i, f: (i, 0)
    const = lambda i, f: (layer, 0, 0)
    fclamp = lambda f: jnp.minimum(f, nf - 1)
    if packed:
        w_specs = [pl.BlockSpec((D, 2 * tf), lambda i, f: (0, fclamp(f))),
                   pl.BlockSpec((tf, D), lambda i, f: (fclamp(f), 0))]
        single = None
    else:
        assert T == tm
        w_specs = [pl.BlockSpec((None, D, tf), lambda i, f: (layer, 0, fclamp(f))),
                   pl.BlockSpec((None, D, tf), lambda i, f: (layer, 0, fclamp(f))),
                   pl.BlockSpec((None, tf, D), lambda i, f: (layer, fclamp(f), 0))]
        single = pl.Buffered(1)
    in_specs = [pl.BlockSpec((tm, D), row, pipeline_mode=pl.Buffered(1))] + w_specs + [
        pl.BlockSpec((None, 1, D), const),
        pl.BlockSpec((None, 1, D), const),
    ]
    args = [x, *weights, ln_g, ln_b]
    if ple is not None:
        gate_w, p, proj_w = ple
        P = p.shape[2]
        pcol = lambda i, f: (layer, 0, jnp.maximum(f - nf, 0))
        in_specs += [
            pl.BlockSpec((None, D, ple_cols), pcol),
            pl.BlockSpec((None, tm, P), lambda i, f: (layer, i, 0)),
            pl.BlockSpec((None, P, ple_cols), pcol),
        ]
        args += [gate_w, p, proj_w]
    out_shape = [jax.ShapeDtypeStruct((T, D), F32)]
    out_specs = [pl.BlockSpec((tm, D), row, pipeline_mode=single)]
    if emit_bf16:
        out_shape.append(jax.ShapeDtypeStruct((T, D), BF16))
        out_specs.append(pl.BlockSpec((tm, D), row, pipeline_mode=single))
    if not packed:
        out_shape += [jax.ShapeDtypeStruct((D, 2 * F), BF16), jax.ShapeDtypeStruct((F, D), BF16)]
        out_specs += [pl.BlockSpec((D, 2 * tf), lambda i, f: (0, fclamp(f))),
                      pl.BlockSpec((tf, D), lambda i, f: (fclamp(f), 0))]
    return pl.pallas_call(
        functools.partial(_ffn_ln_kernel, nf=nf, tf=tf, packed=packed,
                          ple_cols=ple_cols if ple is not None else 0, emit_bf16=emit_bf16),
        grid=(T // tm, nf + n_ple),
        in_specs=in_specs,
        out_specs=out_specs,
        out_shape=out_shape,
        scratch_shapes=[pltpu.VMEM((tm, D), BF16)],
        compiler_params=_params(("parallel", "arbitrary"), 58),
        name="ffn_ln",
    )(*args)


def _proj_in_kernel(x_ref, w_ref, o_ref):
    o_ref[...] = _dot(x_ref[...], w_ref[...])


def _proj_in(xb, w, layer, tm=1024, tn=1024):
    T, D = xb.shape
    N = w.shape[2]
    return pl.pallas_call(
        _proj_in_kernel,
        grid=(T // tm, N // tn),
        in_specs=[pl.BlockSpec((tm, D), lambda i, n: (i, 0)),
                  pl.BlockSpec((None, D, tn), lambda i, n: (layer, 0, n))],
        out_specs=pl.BlockSpec((tm, tn), lambda i, n: (i, n)),
        out_shape=jax.ShapeDtypeStruct((T, N), F32),
        compiler_params=_params(("parallel", "arbitrary"), 40),
        name="proj_in",
    )(xb, w)


def _rotary(x, cos, sin):
    half = RET_HEAD_DIM // 2
    x1, x2 = x[:, :half], x[:, half:]
    return jnp.concatenate([x1 * cos - x2 * sin, x1 * sin + x2 * cos], axis=-1)


def _retention_tables(L):
    row = lax.broadcasted_iota(jnp.int32, (L, L), 0)
    col = lax.broadcasted_iota(jnp.int32, (L, L), 1)
    diff = (row - col).astype(F32)
    causal = row >= col
    idx = lax.broadcasted_iota(jnp.int32, (L, 1), 0).astype(F32)
    tables = []
    for lg in RET_LOG_GAMMA:
        decay = jnp.where(causal, jnp.exp(jnp.where(causal, diff, 0.0) * lg), 0.0)
        q_decay = jnp.exp((idx + 1.0) * lg)
        k_decay = jnp.exp((L - 1.0 - idx) * lg)
        tables.append((decay, q_decay, k_decay, math.exp(L * lg)))
    return tables


def _retention_head(q, k, v, gate, S, cos, sin, table):
    decay, q_decay, k_decay, s_decay = table
    qr = _rotary(q, cos, sin)
    kr = _rotary(k, cos, sin) * (RET_HEAD_DIM ** -0.5)
    qb = qr.astype(BF16)
    vb = v.astype(BF16)
    scores = lax.dot_general(qb, kr.astype(BF16), (((1,), (1,)), ((), ())), preferred_element_type=F32)
    inner = _dot((scores * decay).astype(BF16), vb)
    cross = _dot(qb, S.astype(BF16)) * q_decay
    o = inner + cross
    kd = (kr * k_decay).astype(BF16)
    S_new = s_decay * S + lax.dot_general(kd, vb, (((0,), (0,)), ((), ())), preferred_element_type=F32)
    mu = jnp.mean(o, axis=-1, keepdims=True)
    c = o - mu
    var = jnp.mean(c * c, axis=-1, keepdims=True)
    y = c * lax.rsqrt(var + LN_EPS) * _silu(gate)
    return y, S_new


def _window_count(w, pos):
    return jnp.minimum(jnp.float32(w), pos + 1.0)


def _mixer_prompt_kernel(z_ref, cos_ref, sin_ref, convw_ref, poolw_ref, pscale_ref,
                         x_ref, wout_ref, g_ref, b_ref,
                         o_ref, convn_ref, pooln_ref, retn_ref,
                         y_ref, s_ref, ubuf_ref, pbuf_ref, *, tl, nl):
    l = pl.program_id(1)

    @pl.when(l == 0)
    def _():
        s_ref[...] = jnp.zeros_like(s_ref)
        ubuf_ref[0:CONV_PAD, :] = jnp.zeros((CONV_PAD, CONV_WIDTH), F32)
        pbuf_ref[0:POOL_PAD, :] = jnp.zeros((POOL_PAD, POOL_WIDTH), F32)

    u = z_ref[:, COL_C:COL_C + CONV_WIDTH] * z_ref[:, COL_H:COL_H + CONV_WIDTH]
    ubuf_ref[CONV_PAD:CONV_PAD + tl, :] = u
    cw = convw_ref[...]
    conv = (ubuf_ref[CONV_PAD - 2:CONV_PAD - 2 + tl, :] * cw[0:1, :]
            + ubuf_ref[CONV_PAD - 1:CONV_PAD - 1 + tl, :] * cw[1:2, :]
            + u * cw[2:3, :])
    y_ref[:, 0:CONV_WIDTH] = (z_ref[:, COL_B:COL_B + CONV_WIDTH] * conv).astype(y_ref.dtype)

    pbuf_ref[POOL_PAD:POOL_PAD + tl, :] = z_ref[:, COL_P:COL_P + POOL_WIDTH]
    pos = (l * tl + lax.broadcasted_iota(jnp.int32, (tl, 1), 0)).astype(F32)
    for gi, w in enumerate(POOL_WINDOWS):
        c0 = gi * POOL_GROUP
        tok = pbuf_ref[POOL_PAD:POOL_PAD + tl, c0:c0 + POOL_GROUP]
        win = tok
        for j in range(1, w):
            win = win + pbuf_ref[POOL_PAD - j:POOL_PAD - j + tl, c0:c0 + POOL_GROUP]
        d = win / _window_count(w, pos) - tok
        yb = _dot(d.astype(BF16), poolw_ref[gi]) * pscale_ref[:, c0:c0 + POOL_GROUP]
        y_ref[:, OUT_POOL + c0:OUT_POOL + c0 + POOL_GROUP] = yb.astype(y_ref.dtype)

    tables = _retention_tables(RET_CHUNK)
    for c in range(tl // RET_CHUNK):
        r0 = c * RET_CHUNK
        cos = cos_ref[r0:r0 + RET_CHUNK, :]
        sin = sin_ref[r0:r0 + RET_CHUNK, :]
        for h in range(RET_HEADS):
            h0 = h * RET_HEAD_DIM
            y, s_new = _retention_head(
                z_ref[r0:r0 + RET_CHUNK, COL_Q + h0:COL_Q + h0 + RET_HEAD_DIM],
                z_ref[r0:r0 + RET_CHUNK, COL_K + h0:COL_K + h0 + RET_HEAD_DIM],
                z_ref[r0:r0 + RET_CHUNK, COL_V + h0:COL_V + h0 + RET_HEAD_DIM],
                z_ref[r0:r0 + RET_CHUNK, COL_G + h0:COL_G + h0 + RET_HEAD_DIM],
                s_ref[h], cos, sin, tables[h])
            s_ref[h] = s_new
            y_ref[r0:r0 + RET_CHUNK, OUT_RET + h0:OUT_RET + h0 + RET_HEAD_DIM] = y.astype(y_ref.dtype)
        m = _dot(y_ref[r0:r0 + RET_CHUNK, :], wout_ref[...])
        o_ref[r0:r0 + RET_CHUNK, :] = _layer_norm(
            DEEPNORM_ALPHA * x_ref[r0:r0 + RET_CHUNK, :] + m, g_ref[...], b_ref[...])

    @pl.when(l == nl - 1)
    def _():
        convn_ref[0] = ubuf_ref[CONV_PAD + tl - (CONV_K - 1):CONV_PAD + tl, :]
        pooln_ref[0] = pbuf_ref[POOL_PAD + tl - POOL_HIST:POOL_PAD + tl, :]
        retn_ref[0] = s_ref[...]

    ubuf_ref[0:CONV_PAD, :] = ubuf_ref[tl:tl + CONV_PAD, :]
    pbuf_ref[0:POOL_PAD, :] = pbuf_ref[tl:tl + POOL_PAD, :]


def _mixer_prompt(z, x1, cos, sin, conv_w, pool_w, pool_scale, w_out, ln_g, ln_b, layer, batch, tl=256):
    T = z.shape[0]
    L = T // batch
    nl = L // tl
    row = lambda b, l: (b * nl + l, 0)
    lay3 = lambda b, l: (layer, 0, 0)
    return pl.pallas_call(
        functools.partial(_mixer_prompt_kernel, tl=tl, nl=nl),
        grid=(batch, nl),
        in_specs=[
            pl.BlockSpec((tl, IN_COLS), row),
            pl.BlockSpec((tl, RET_HEAD_DIM // 2), lambda b, l: (l, 0)),
            pl.BlockSpec((tl, RET_HEAD_DIM // 2), lambda b, l: (l, 0)),
            pl.BlockSpec((None, CONV_K, CONV_WIDTH), lay3),
            pl.BlockSpec((None, len(POOL_WINDOWS), POOL_GROUP, POOL_GROUP), lambda b, l: (layer, 0, 0, 0)),
            pl.BlockSpec((None, 1, POOL_WIDTH), lay3),
            pl.BlockSpec((tl, D_MODEL), row),
            pl.BlockSpec((None, D_MODEL, D_MODEL), lay3, pipeline_mode=pl.Buffered(1)),
            pl.BlockSpec((None, 1, D_MODEL), lay3),
            pl.BlockSpec((None, 1, D_MODEL), lay3),
        ],
        out_specs=[
            pl.BlockSpec((tl, D_MODEL), row),
            pl.BlockSpec((1, CONV_K - 1, CONV_WIDTH), lambda b, l: (b, 0, 0)),
            pl.BlockSpec((1, POOL_HIST, POOL_WIDTH), lambda b, l: (b, 0, 0)),
            pl.BlockSpec((1, RET_HEADS, RET_HEAD_DIM, RET_HEAD_DIM), lambda b, l: (b, 0, 0, 0)),
        ],
        out_shape=[
            jax.ShapeDtypeStruct((T, D_MODEL), F32),
            jax.ShapeDtypeStruct((batch, CONV_K - 1, CONV_WIDTH), F32),
            jax.ShapeDtypeStruct((batch, POOL_HIST, POOL_WIDTH), F32),
            jax.ShapeDtypeStruct((batch, RET_HEADS, RET_HEAD_DIM, RET_HEAD_DIM), F32),
        ],
        scratch_shapes=[
            pltpu.VMEM((tl, D_MODEL), BF16),
            pltpu.VMEM((RET_HEADS, RET_HEAD_DIM, RET_HEAD_DIM), F32),
            pltpu.VMEM((CONV_PAD + tl, CONV_WIDTH), F32),
            pltpu.VMEM((POOL_PAD + tl, POOL_WIDTH), F32),
        ],
        compiler_params=_params(("parallel", "arbitrary"), 48),
        name="mixer_prompt",
    )(z, cos, sin, conv_w, pool_w, pool_scale, x1, w_out, ln_g, ln_b)


def _mixer_sample_kernel(z_ref, cos_ref, sin_ref, convw_ref, poolw_ref, pscale_ref,
                         convc_ref, poolc_ref, state_ref, *rest, nb, ls, start_pos, aliased):
    if aliased:
        rest = rest[1:]
    y_ref, convn_ref, pooln_ref, staten_ref, ubuf_ref, pbuf_ref = rest
    u = z_ref[:, :, COL_C:COL_C + CONV_WIDTH] * z_ref[:, :, COL_H:COL_H + CONV_WIDTH]
    ubuf_ref[:, CONV_PAD - (CONV_K - 1):CONV_PAD, :] = convc_ref[...]
    ubuf_ref[:, CONV_PAD:CONV_PAD + ls, :] = u
    cw = convw_ref[...]
    conv = (ubuf_ref[:, CONV_PAD - 2:CONV_PAD - 2 + ls, :] * cw[0:1, :]
            + ubuf_ref[:, CONV_PAD - 1:CONV_PAD - 1 + ls, :] * cw[1:2, :]
            + u * cw[2:3, :])
    y_ref[:, :, 0:CONV_WIDTH] = z_ref[:, :, COL_B:COL_B + CONV_WIDTH] * conv
    convn_ref[...] = ubuf_ref[:, CONV_PAD + ls - (CONV_K - 1):CONV_PAD + ls, :]

    pbuf_ref[:, POOL_PAD - POOL_HIST:POOL_PAD, :] = poolc_ref[...]
    pbuf_ref[:, POOL_PAD:POOL_PAD + ls, :] = z_ref[:, :, COL_P:COL_P + POOL_WIDTH]
    pos = (start_pos + lax.broadcasted_iota(jnp.int32, (1, ls, 1), 1)).astype(F32)
    for gi, w in enumerate(POOL_WINDOWS):
        c0 = gi * POOL_GROUP
        tok = pbuf_ref[:, POOL_PAD:POOL_PAD + ls, c0:c0 + POOL_GROUP]
        win = tok
        for j in range(1, w):
            win = win + pbuf_ref[:, POOL_PAD - j:POOL_PAD - j + ls, c0:c0 + POOL_GROUP]
        d = win / _window_count(w, pos) - tok
        yb = _dot(d.reshape(nb * ls, POOL_GROUP).astype(BF16), poolw_ref[gi]).reshape(nb, ls, POOL_GROUP)
        y_ref[:, :, OUT_POOL + c0:OUT_POOL + c0 + POOL_GROUP] = yb * pscale_ref[:, c0:c0 + POOL_GROUP]
    pooln_ref[...] = pbuf_ref[:, POOL_PAD + ls - POOL_HIST:POOL_PAD + ls, :]

    tables = _retention_tables(ls)
    cos = cos_ref[...]
    sin = sin_ref[...]

    def per_sequence(b, carry):
        for h in range(RET_HEADS):
            h0 = h * RET_HEAD_DIM
            y, s_new = _retention_head(
                z_ref[b, :, COL_Q + h0:COL_Q + h0 + RET_HEAD_DIM],
                z_ref[b, :, COL_K + h0:COL_K + h0 + RET_HEAD_DIM],
                z_ref[b, :, COL_V + h0:COL_V + h0 + RET_HEAD_DIM],
                z_ref[b, :, COL_G + h0:COL_G + h0 + RET_HEAD_DIM],
                state_ref[b, h], cos, sin, tables[h])
            staten_ref[b, h] = s_new
            y_ref[b, :, OUT_RET + h0:OUT_RET + h0 + RET_HEAD_DIM] = y
        return carry

    lax.fori_loop(0, nb, per_sequence, 0)


def _mixer_sample(z, cos, sin, conv_w, pool_w, pool_scale, conv_cache, pool_cache, state, state_out,
                  layer, start_pos, nb=8):
    B, ls, _ = z.shape
    seq3 = lambda i: (i, 0, 0)
    lay3 = lambda i: (layer, 0, 0)
    lseq4 = lambda i: (layer, i, 0, 0)
    lseq5 = lambda i: (layer, i, 0, 0, 0)
    in_specs = [
        pl.BlockSpec((nb, ls, IN_COLS), seq3),
        pl.BlockSpec((ls, RET_HEAD_DIM // 2), lambda i: (0, 0)),
        pl.BlockSpec((ls, RET_HEAD_DIM // 2), lambda i: (0, 0)),
        pl.BlockSpec((None, CONV_K, CONV_WIDTH), lay3),
        pl.BlockSpec((None, len(POOL_WINDOWS), POOL_GROUP, POOL_GROUP), lambda i: (layer, 0, 0, 0)),
        pl.BlockSpec((None, 1, POOL_WIDTH), lay3),
        pl.BlockSpec((None, nb, CONV_K - 1, CONV_WIDTH), lseq4),
        pl.BlockSpec((None, nb, POOL_HIST, POOL_WIDTH), lseq4),
        pl.BlockSpec((None, nb, RET_HEADS, RET_HEAD_DIM, RET_HEAD_DIM), lseq5),
    ]
    args = [z, cos, sin, conv_w, pool_w, pool_scale, conv_cache, pool_cache, state]
    aliases = {}
    if state_out is not None:
        in_specs.append(pl.BlockSpec(memory_space=pl.ANY))
        args.append(state_out)
        aliases = {len(args) - 1: 3}
    return pl.pallas_call(
        functools.partial(_mixer_sample_kernel, nb=nb, ls=ls, start_pos=start_pos,
                          aliased=state_out is not None),
        grid=(B // nb,),
        in_specs=in_specs,
        out_specs=[
            pl.BlockSpec((nb, ls, D_MODEL), seq3),
            pl.BlockSpec((nb, CONV_K - 1, CONV_WIDTH), seq3),
            pl.BlockSpec((nb, POOL_HIST, POOL_WIDTH), seq3),
            pl.BlockSpec((None, nb, RET_HEADS, RET_HEAD_DIM, RET_HEAD_DIM), lseq5),
        ],
        out_shape=[
            jax.ShapeDtypeStruct((B, ls, D_MODEL), F32),
            jax.ShapeDtypeStruct((B, CONV_K - 1, CONV_WIDTH), F32),
            jax.ShapeDtypeStruct((B, POOL_HIST, POOL_WIDTH), F32),
            jax.ShapeDtypeStruct(state.shape, F32),
        ],
        scratch_shapes=[
            pltpu.VMEM((nb, CONV_PAD + ls, CONV_WIDTH), F32),
            pltpu.VMEM((nb, POOL_PAD + ls, POOL_WIDTH), F32),
        ],
        input_output_aliases=aliases,
        compiler_params=_params(("parallel",), 48),
        name="mixer_sample",
    )(*args)


def _out_ln_kernel(y_ref, x_ref, w_ref, g_ref, b_ref, o_ref):
    m = _dot(y_ref[...].astype(BF16), w_ref[...])
    o_ref[...] = _layer_norm(DEEPNORM_ALPHA * x_ref[...] + m, g_ref[...], b_ref[...])


def _out_ln(y, x, w, ln_g, ln_b, layer, tm=256):
    T, D = x.shape
    row = lambda i: (i, 0)
    const = lambda i: (layer, 0, 0)
    return pl.pallas_call(
        _out_ln_kernel,
        grid=(T // tm,),
        in_specs=[
            pl.BlockSpec((tm, y.shape[1]), row),
            pl.BlockSpec((tm, D), row),
            pl.BlockSpec((None,) + w.shape[1:], const),
            pl.BlockSpec((None, 1, D), const),
            pl.BlockSpec((None, 1, D), const),
        ],
        out_specs=pl.BlockSpec((tm, D), row),
        out_shape=jax.ShapeDtypeStruct((T, D), F32),
        compiler_params=_params(("parallel",), 48),
        name="out_ln",
    )(y, x, w, ln_g, ln_b)


def _rope_tables(start_pos, length):
    half = RET_HEAD_DIM // 2
    inv = ROPE_BASE ** (-jnp.arange(half, dtype=F32) / half)
    pos = start_pos + jnp.arange(length, dtype=F32)
    ang = pos[:, None] * inv[None, :]
    return jnp.cos(ang), jnp.sin(ang)


def _trunk(x, p, caches, start_pos, w, packed):
    B, L, D = x.shape
    T = B * L
    x = x.reshape(T, D)
    p = p.reshape(DEPTH, T, -1)
    cos, sin = _rope_tables(start_pos, L)
    convs, pools, rets = [], [], []
    state_out = None
    collect = not packed

    def ffn(x, name, ln, layer, **kw):
        if collect:
            *outs, wgu, wd = _ffn_ln(x, (w[name + "_w_gate"], w[name + "_w_up"], w[name + "_w_down"]),
                                     w[ln + "_g"], w[ln + "_b"], layer, **kw)
            packed[name, layer] = (wgu, wd)
            return outs
        return _ffn_ln(x, packed[name, layer], w[ln + "_g"], w[ln + "_b"], layer, **kw)

    for i in range(DEPTH):
        x1, x1b = ffn(x, "ffn1", "ln1", i, emit_bf16=True)
        z = _proj_in(x1b, w["w_in"], i)
        if caches is None:
            x2, c_new, p_new, r_new = _mixer_prompt(
                z, x1, cos, sin, w["conv_w"], w["pool_w"], w["pool_scale"], w["w_out"], w["ln2_g"], w["ln2_b"],
                i, B)
            rets.append(r_new)
        else:
            y, c_new, p_new, state_out = _mixer_sample(
                z.reshape(B, L, IN_COLS), cos, sin, w["conv_w"], w["pool_w"], w["pool_scale"],
                caches[0], caches[1], caches[2], state_out, i, start_pos)
            x2 = _out_ln(y.reshape(T, D), x1, w["w_out"], w["ln2_g"], w["ln2_b"], i)
        x, = ffn(x2, "ffn2", "ln3", i, ple=(w["ple_gate"], p, w["ple_proj"]))
        convs.append(c_new)
        pools.append(p_new)
    ret = jnp.stack(rets) if caches is None else state_out
    return x.reshape(B, L, D), jnp.stack(convs), jnp.stack(pools), ret


def kernel(x_prompt, x_sample, p_prompt, p_sample, cache_conv, cache_pool, state_ret, ln1_g, ln1_b, ffn1_w_gate, ffn1_w_up, ffn1_w_down, w_in, conv_w, pool_w, pool_scale, w_out, ln2_g, ln2_b, ffn2_w_gate, ffn2_w_up, ffn2_w_down, ple_gate, ple_proj, ln3_g, ln3_b):
    precast = dict(w_in=w_in, pool_w=pool_w, w_out=w_out, ple_gate=ple_gate, ple_proj=ple_proj)
    row_params = dict(ln1_g=ln1_g, ln1_b=ln1_b, ln2_g=ln2_g, ln2_b=ln2_b, ln3_g=ln3_g, ln3_b=ln3_b,
                      pool_scale=pool_scale)
    w = {k: _to_bf16(v) for k, v in precast.items()}
    w.update({k: v.reshape(DEPTH, 1, -1) for k, v in row_params.items()})
    w.update(conv_w=conv_w, ffn1_w_gate=ffn1_w_gate, ffn1_w_up=ffn1_w_up, ffn1_w_down=ffn1_w_down,
             ffn2_w_gate=ffn2_w_gate, ffn2_w_up=ffn2_w_up, ffn2_w_down=ffn2_w_down)
    packed = {}
    y_sample, conv_s, pool_s, ret_s = _trunk(
        x_sample, p_sample, (cache_conv, cache_pool, state_ret), PAST_LEN, w, packed)
    y_prompt, conv_p, pool_p, ret_p = _trunk(x_prompt, p_prompt, None, 0, w, packed)
    return (y_prompt, y_sample, conv_p, pool_p, ret_p, conv_s, pool_s, ret_s)
```

```python
import functools
import math

import jax
import jax.numpy as jnp
from jax import lax
from jax.experimental import pallas as pl
from jax.experimental.pallas import tpu as pltpu

D_MODEL = 2048
DEPTH = 2
PAST_LEN = 16384
CONV_WIDTH = D_MODEL // 4
POOL_WIDTH = D_MODEL // 4
RET_WIDTH = D_MODEL // 2
CONV_K = 3
POOL_WINDOWS = (2, 4, 8, 16)
POOL_GROUP = POOL_WIDTH // len(POOL_WINDOWS)
POOL_HIST = max(POOL_WINDOWS) - 1
RET_HEADS = 4
RET_HEAD_DIM = RET_WIDTH // RET_HEADS
RET_CHUNK = 128
RET_LOG_GAMMA = tuple(math.log(1.0 - 2.0 ** (-5 - h)) for h in range(RET_HEADS))
ROPE_BASE = 10000.0
IN_COLS = 3 * CONV_WIDTH + POOL_WIDTH + 4 * RET_WIDTH
DEEPNORM_ALPHA = (2 * DEPTH) ** 0.25
LN_EPS = 1e-5

COL_B = 0
COL_C = CONV_WIDTH
COL_H = 2 * CONV_WIDTH
COL_P = 3 * CONV_WIDTH
COL_Q = COL_P + POOL_WIDTH
COL_K = COL_Q + RET_WIDTH
COL_V = COL_K + RET_WIDTH
COL_G = COL_V + RET_WIDTH
OUT_POOL = CONV_WIDTH
OUT_RET = CONV_WIDTH + POOL_WIDTH

SUBLANES = 8
CONV_PAD = SUBLANES
POOL_PAD = 16

F32 = jnp.float32
BF16 = jnp.bfloat16
MIB = 1024 * 1024


def _params(semantics, vmem_mib):
    return pltpu.CompilerParams(dimension_semantics=semantics, vmem_limit_bytes=vmem_mib * MIB)


def _layer_norm(r, g, b):
    mu = jnp.mean(r, axis=-1, keepdims=True)
    c = r - mu
    var = jnp.mean(c * c, axis=-1, keepdims=True)
    return c * lax.rsqrt(var + LN_EPS) * g + b


def _silu(x):
    return x * jax.nn.sigmoid(x)


def _dot(a, b):
    return jnp.dot(a, b, preferred_element_type=F32)


def _cast_kernel(x_ref, o_ref):
    o_ref[...] = x_ref[...].astype(o_ref.dtype)


def _to_bf16(w, block_bytes=6 * MIB):
    shape = w.shape
    w = w.reshape(shape[0], -1, shape[-1])
    _, R, C = w.shape
    tr = R
    while tr * C * 4 > block_bytes and tr % 2 == 0 and (tr // 2) % 16 == 0:
        tr //= 2
    out = pl.pallas_call(
        _cast_kernel,
        grid=(shape[0], R // tr),
        in_specs=[pl.BlockSpec((None, tr, C), lambda d, r: (d, r, 0))],
        out_specs=pl.BlockSpec((None, tr, C), lambda d, r: (d, r, 0)),
        out_shape=jax.ShapeDtypeStruct(w.shape, BF16),
        compiler_params=_params(("parallel", "parallel"), 40),
        name="to_bf16",
    )(w)
    return out.reshape(shape)


FFN_PACK = 256
FFN_ROWS = 512


def _ffn_ln_kernel(*refs, nf, tf, packed, ple_cols):
    refs = list(refs)
    x_ref = refs.pop(0)
    if packed:
        wgu_ref, wd_ref = refs[:2]
        refs = refs[2:]
    else:
        wg_ref, wu_ref, wd_ref = refs[:3]
        refs = refs[3:]
    g_ref, b_ref = refs[:2]
    refs = refs[2:]
    if ple_cols:
        gate_ref, p_ref, proj_ref = refs[:3]
        refs = refs[3:]
    o_ref = refs.pop(0)
    if not packed:
        wgu_out_ref, wd_out_ref = refs[:2]
        refs = refs[2:]
    xb_ref = refs.pop(0)
    f = pl.program_id(1)
    n_ple = o_ref.shape[1] // ple_cols if ple_cols else 0

    @pl.when(f == 0)
    def _():
        x = x_ref[...]
        xb_ref[...] = x.astype(BF16)
        o_ref[...] = DEEPNORM_ALPHA * x

    @pl.when(f < nf)
    def _():
        if packed:
            wd = wd_ref[...]
        else:
            wg = wg_ref[...].astype(BF16)
            wu = wu_ref[...].astype(BF16)
            wd = wd_ref[...].astype(BF16)
            wgu_out_ref[...] = jnp.concatenate([wg, wu], axis=1)
            wd_out_ref[...] = wd
        for r0 in range(0, o_ref.shape[0], FFN_ROWS):
            xb = xb_ref[r0:r0 + FFN_ROWS, :]
            if packed:
                hs = []
                for s in range(tf // FFN_PACK):
                    c0 = 2 * FFN_PACK * s
                    gu = _dot(xb, wgu_ref[:, c0:c0 + 2 * FFN_PACK])
                    hs.append((_silu(gu[:, :FFN_PACK]) * gu[:, FFN_PACK:]).astype(BF16))
                h = jnp.concatenate(hs, axis=1)
            else:
                h = (_silu(_dot(xb, wg)) * _dot(xb, wu)).astype(BF16)
            o_ref[r0:r0 + FFN_ROWS, :] += 0.5 * _dot(h, wd)

    for j in range(n_ple):
        @pl.when(f == nf + j)
        def _():
            gate = jax.nn.sigmoid(_dot(xb_ref[...], gate_ref[...]))
            o_ref[:, j * ple_cols:(j + 1) * ple_cols] += gate * _dot(p_ref[...].astype(BF16), proj_ref[...])

    @pl.when(f == nf + n_ple - 1)
    def _():
        o_ref[...] = _layer_norm(o_ref[...], g_ref[...], b_ref[...])


def _ffn_ln(x, weights, ln_g, ln_b, layer, ple=None, tm=1024, packed_tf=512, ple_cols=512):
    T, D = x.shape
    packed = len(weights) == 2
    tf = packed_tf if packed else FFN_PACK
    F = weights[1].shape[0] if packed else weights[0].shape[2]
    nf = F // tf
    n_ple = D // ple_cols if ple is not None else 0
    row = lambda i, f: (i, 0)
    const = lambda i, f: (layer, 0, 0)
    fclamp = lambda f: jnp.minimum(f, nf - 1)
    if packed:
        w_specs = [pl.BlockSpec((D, 2 * tf), lambda i, f: (0, fclamp(f))),
                   pl.BlockSpec((tf, D), lambda i, f: (fclamp(f), 0))]
        single = None
    else:
        assert T == tm
        w_specs = [pl.BlockSpec((None, D, tf), lambda i, f: (layer, 0, fclamp(f))),
                   pl.BlockSpec((None, D, tf), lambda i, f: (layer, 0, fclamp(f))),
                   pl.BlockSpec((None, tf, D), lambda i, f: (layer, fclamp(f), 0))]
        single = pl.Buffered(1)
    in_specs = [pl.BlockSpec((tm, D), row, pipeline_mode=single)] + w_specs + [
        pl.BlockSpec((None, 1, D), const),
        pl.BlockSpec((None, 1, D), const),
    ]
    args = [x, *weights, ln_g, ln_b]
    if ple is not None:
        gate_w, p, proj_w = ple
        P = p.shape[2]
        pcol = lambda i, f: (layer, 0, jnp.maximum(f - nf, 0))
        in_specs += [
            pl.BlockSpec((None, D, ple_cols), pcol),
            pl.BlockSpec((None, tm, P), lambda i, f: (layer, i, 0)),
            pl.BlockSpec((None, P, ple_cols), pcol),
        ]
        args += [gate_w, p, proj_w]
    out_shape = [jax.ShapeDtypeStruct((T, D), F32)]
    out_specs = [pl.BlockSpec((tm, D), row, pipeline_mode=single)]
    if not packed:
        out_shape += [jax.ShapeDtypeStruct((D, 2 * F), BF16), jax.ShapeDtypeStruct((F, D), BF16)]
        out_specs += [pl.BlockSpec((D, 2 * tf), lambda i, f: (0, fclamp(f))),
                      pl.BlockSpec((tf, D), lambda i, f: (fclamp(f), 0))]
    return pl.pallas_call(
        functools.partial(_ffn_ln_kernel, nf=nf, tf=tf, packed=packed,
                          ple_cols=ple_cols if ple is not None else 0),
        grid=(T // tm, nf + n_ple),
        in_specs=in_specs,
        out_specs=out_specs,
        out_shape=out_shape,
        scratch_shapes=[pltpu.VMEM((tm, D), BF16)],
        compiler_params=_params(("parallel", "arbitrary"), 62),
        name="ffn_ln",
    )(*args)


def _proj_in_kernel(x_ref, w_ref, o_ref, xb_ref):
    @pl.when(pl.program_id(1) == 0)
    def _():
        xb_ref[...] = x_ref[...].astype(BF16)

    o_ref[...] = _dot(xb_ref[...], w_ref[...])


def _proj_in(x, w, layer, tm=1024, tn=1024):
    T, D = x.shape
    N = w.shape[2]
    return pl.pallas_call(
        _proj_in_kernel,
        grid=(T // tm, N // tn),
        in_specs=[pl.BlockSpec((tm, D), lambda i, n: (i, 0)),
                  pl.BlockSpec((None, D, tn), lambda i, n: (layer, 0, n))],
        out_specs=pl.BlockSpec((tm, tn), lambda i, n: (i, n)),
        out_shape=jax.ShapeDtypeStruct((T, N), F32),
        scratch_shapes=[pltpu.VMEM((tm, D), BF16)],
        compiler_params=_params(("parallel", "arbitrary"), 48),
        name="proj_in",
    )(x, w)


def _rotary(x, cos, sin):
    half = RET_HEAD_DIM // 2
    x1, x2 = x[:, :half], x[:, half:]
    return jnp.concatenate([x1 * cos - x2 * sin, x1 * sin + x2 * cos], axis=-1)


def _retention_tables(L):
    row = lax.broadcasted_iota(jnp.int32, (L, L), 0)
    col = lax.broadcasted_iota(jnp.int32, (L, L), 1)
    diff = (row - col).astype(F32)
    causal = row >= col
    idx = lax.broadcasted_iota(jnp.int32, (L, 1), 0).astype(F32)
    tables = []
    for lg in RET_LOG_GAMMA:
        decay = jnp.where(causal, jnp.exp(jnp.where(causal, diff, 0.0) * lg), 0.0)
        q_decay = jnp.exp((idx + 1.0) * lg)
        k_decay = jnp.exp((L - 1.0 - idx) * lg)
        tables.append((decay, q_decay, k_decay, math.exp(L * lg)))
    return tables


def _retention_head(q, k, v, gate, S, cos, sin, table):
    decay, q_decay, k_decay, s_decay = table
    qr = _rotary(q, cos, sin)
    kr = _rotary(k, cos, sin) * (RET_HEAD_DIM ** -0.5)
    qb = qr.astype(BF16)
    vb = v.astype(BF16)
    scores = lax.dot_general(qb, kr.astype(BF16), (((1,), (1,)), ((), ())), preferred_element_type=F32)
    inner = _dot((scores * decay).astype(BF16), vb)
    cross = _dot(qb, S.astype(BF16)) * q_decay
    o = inner + cross
    kd = (kr * k_decay).astype(BF16)
    S_new = s_decay * S + lax.dot_general(kd, vb, (((0,), (0,)), ((), ())), preferred_element_type=F32)
    mu = jnp.mean(o, axis=-1, keepdims=True)
    c = o - mu
    var = jnp.mean(c * c, axis=-1, keepdims=True)
    y = c * lax.rsqrt(var + LN_EPS) * _silu(gate)
    return y, S_new


def _window_count(w, pos):
    return jnp.minimum(jnp.float32(w), pos + 1.0)


def _mixer_prompt_kernel(z_ref, cos_ref, sin_ref, convw_ref, poolw_ref, pscale_ref,
                         x_ref, wout_ref, g_ref, b_ref,
                         o_ref, convn_ref, pooln_ref, retn_ref,
                         y_ref, s_ref, ubuf_ref, pbuf_ref, *, tl, nl):
    l = pl.program_id(1)

    @pl.when(l == 0)
    def _():
        s_ref[...] = jnp.zeros_like(s_ref)
        ubuf_ref[0:CONV_PAD, :] = jnp.zeros((CONV_PAD, CONV_WIDTH), F32)
        pbuf_ref[0:POOL_PAD, :] = jnp.zeros((POOL_PAD, POOL_WIDTH), F32)

    u = z_ref[:, COL_C:COL_C + CONV_WIDTH] * z_ref[:, COL_H:COL_H + CONV_WIDTH]
    ubuf_ref[CONV_PAD:CONV_PAD + tl, :] = u
    cw = convw_ref[...]
    conv = (ubuf_ref[CONV_PAD - 2:CONV_PAD - 2 + tl, :] * cw[0:1, :]
            + ubuf_ref[CONV_PAD - 1:CONV_PAD - 1 + tl, :] * cw[1:2, :]
            + u * cw[2:3, :])
    y_ref[:, 0:CONV_WIDTH] = (z_ref[:, COL_B:COL_B + CONV_WIDTH] * conv).astype(y_ref.dtype)

    pbuf_ref[POOL_PAD:POOL_PAD + tl, :] = z_ref[:, COL_P:COL_P + POOL_WIDTH]
    pos = (l * tl + lax.broadcasted_iota(jnp.int32, (tl, 1), 0)).astype(F32)
    for gi, w in enumerate(POOL_WINDOWS):
        c0 = gi * POOL_GROUP
        tok = pbuf_ref[POOL_PAD:POOL_PAD + tl, c0:c0 + POOL_GROUP]
        win = tok
        for j in range(1, w):
            win = win + pbuf_ref[POOL_PAD - j:POOL_PAD - j + tl, c0:c0 + POOL_GROUP]
        d = win / _window_count(w, pos) - tok
        yb = _dot(d.astype(BF16), poolw_ref[gi]) * pscale_ref[:, c0:c0 + POOL_GROUP]
        y_ref[:, OUT_POOL + c0:OUT_POOL + c0 + POOL_GROUP] = yb.astype(y_ref.dtype)

    tables = _retention_tables(RET_CHUNK)
    for c in range(tl // RET_CHUNK):
        r0 = c * RET_CHUNK
        cos = cos_ref[r0:r0 + RET_CHUNK, :]
        sin = sin_ref[r0:r0 + RET_CHUNK, :]
        for h in range(RET_HEADS):
            h0 = h * RET_HEAD_DIM
            y, s_new = _retention_head(
                z_ref[r0:r0 + RET_CHUNK, COL_Q + h0:COL_Q + h0 + RET_HEAD_DIM],
                z_ref[r0:r0 + RET_CHUNK, COL_K + h0:COL_K + h0 + RET_HEAD_DIM],
                z_ref[r0:r0 + RET_CHUNK, COL_V + h0:COL_V + h0 + RET_HEAD_DIM],
                z_ref[r0:r0 + RET_CHUNK, COL_G + h0:COL_G + h0 + RET_HEAD_DIM],
                s_ref[h], cos, sin, tables[h])
            s_ref[h] = s_new
            y_ref[r0:r0 + RET_CHUNK, OUT_RET + h0:OUT_RET + h0 + RET_HEAD_DIM] = y.astype(y_ref.dtype)
        m = _dot(y_ref[r0:r0 + RET_CHUNK, :], wout_ref[...])
        o_ref[r0:r0 + RET_CHUNK, :] = _layer_norm(
            DEEPNORM_ALPHA * x_ref[r0:r0 + RET_CHUNK, :] + m, g_ref[...], b_ref[...])

    @pl.when(l == nl - 1)
    def _():
        convn_ref[0] = ubuf_ref[CONV_PAD + tl - (CONV_K - 1):CONV_PAD + tl, :]
        pooln_ref[0] = pbuf_ref[POOL_PAD + tl - POOL_HIST:POOL_PAD + tl, :]
        retn_ref[0] = s_ref[...]

    ubuf_ref[0:CONV_PAD, :] = ubuf_ref[tl:tl + CONV_PAD, :]
    pbuf_ref[0:POOL_PAD, :] = pbuf_ref[tl:tl + POOL_PAD, :]


def _mixer_prompt(z, x1, cos, sin, conv_w, pool_w, pool_scale, w_out, ln_g, ln_b, layer, batch, tl=256):
    T = z.shape[0]
    L = T // batch
    nl = L // tl
    row = lambda b, l: (b * nl + l, 0)
    lay3 = lambda b, l: (layer, 0, 0)
    return pl.pallas_call(
        functools.partial(_mixer_prompt_kernel, tl=tl, nl=nl),
        grid=(batch, nl),
        in_specs=[
            pl.BlockSpec((tl, IN_COLS), row),
            pl.BlockSpec((tl, RET_HEAD_DIM // 2), lambda b, l: (l, 0)),
            pl.BlockSpec((tl, RET_HEAD_DIM // 2), lambda b, l: (l, 0)),
            pl.BlockSpec((None, CONV_K, CONV_WIDTH), lay3),
            pl.BlockSpec((None, len(POOL_WINDOWS), POOL_GROUP, POOL_GROUP), lambda b, l: (layer, 0, 0, 0)),
            pl.BlockSpec((None, 1, POOL_WIDTH), lay3),
            pl.BlockSpec((tl, D_MODEL), row),
            pl.BlockSpec((None, D_MODEL, D_MODEL), lay3, pipeline_mode=pl.Buffered(1)),
            pl.BlockSpec((None, 1, D_MODEL), lay3),
            pl.BlockSpec((None, 1, D_MODEL), lay3),
        ],
        out_specs=[
            pl.BlockSpec((tl, D_MODEL), row),
            pl.BlockSpec((1, CONV_K - 1, CONV_WIDTH), lambda b, l: (b, 0, 0)),
            pl.BlockSpec((1, POOL_HIST, POOL_WIDTH), lambda b, l: (b, 0, 0)),
            pl.BlockSpec((1, RET_HEADS, RET_HEAD_DIM, RET_HEAD_DIM), lambda b, l: (b, 0, 0, 0)),
        ],
        out_shape=[
            jax.ShapeDtypeStruct((T, D_MODEL), F32),
            jax.ShapeDtypeStruct((batch, CONV_K - 1, CONV_WIDTH), F32),
            jax.ShapeDtypeStruct((batch, POOL_HIST, POOL_WIDTH), F32),
            jax.ShapeDtypeStruct((batch, RET_HEADS, RET_HEAD_DIM, RET_HEAD_DIM), F32),
        ],
        scratch_shapes=[
            pltpu.VMEM((tl, D_MODEL), BF16),
            pltpu.VMEM((RET_HEADS, RET_HEAD_DIM, RET_HEAD_DIM), F32),
            pltpu.VMEM((CONV_PAD + tl, CONV_WIDTH), F32),
            pltpu.VMEM((POOL_PAD + tl, POOL_WIDTH), F32),
        ],
        compiler_params=_params(("parallel", "arbitrary"), 48),
        name="mixer_prompt",
    )(z, cos, sin, conv_w, pool_w, pool_scale, x1, w_out, ln_g, ln_b)


def _mixer_sample_kernel(z_ref, cos_ref, sin_ref, convw_ref, poolw_ref, pscale_ref,
                         convc_ref, poolc_ref, state_ref, *rest, nb, ls, start_pos, aliased):
    if aliased:
        rest = rest[1:]
    y_ref, convn_ref, pooln_ref, staten_ref, ubuf_ref, pbuf_ref = rest
    u = z_ref[:, :, COL_C:COL_C + CONV_WIDTH] * z_ref[:, :, COL_H:COL_H + CONV_WIDTH]
    ubuf_ref[:, CONV_PAD - (CONV_K - 1):CONV_PAD, :] = convc_ref[...]
    ubuf_ref[:, CONV_PAD:CONV_PAD + ls, :] = u
    cw = convw_ref[...]
    conv = (ubuf_ref[:, CONV_PAD - 2:CONV_PAD - 2 + ls, :] * cw[0:1, :]
            + ubuf_ref[:, CONV_PAD - 1:CONV_PAD - 1 + ls, :] * cw[1:2, :]
            + u * cw[2:3, :])
    y_ref[:, :, 0:CONV_WIDTH] = z_ref[:, :, COL_B:COL_B + CONV_WIDTH] * conv
    convn_ref[...] = ubuf_ref[:, CONV_PAD + ls - (CONV_K - 1):CONV_PAD + ls, :]

    pbuf_ref[:, POOL_PAD - POOL_HIST:POOL_PAD, :] = poolc_ref[...]
    pbuf_ref[:, POOL_PAD:POOL_PAD + ls, :] = z_ref[:, :, COL_P:COL_P + POOL_WIDTH]
    pos = (start_pos + lax.broadcasted_iota(jnp.int32, (1, ls, 1), 1)).astype(F32)
    for gi, w in enumerate(POOL_WINDOWS):
        c0 = gi * POOL_GROUP
        tok = pbuf_ref[:, POOL_PAD:POOL_PAD + ls, c0:c0 + POOL_GROUP]
        win = tok
        for j in range(1, w):
            win = win + pbuf_ref[:, POOL_PAD - j:POOL_PAD - j + ls, c0:c0 + POOL_GROUP]
        d = win / _window_count(w, pos) - tok
        yb = _dot(d.reshape(nb * ls, POOL_GROUP).astype(BF16), poolw_ref[gi]).reshape(nb, ls, POOL_GROUP)
        y_ref[:, :, OUT_POOL + c0:OUT_POOL + c0 + POOL_GROUP] = yb * pscale_ref[:, c0:c0 + POOL_GROUP]
    pooln_ref[...] = pbuf_ref[:, POOL_PAD + ls - POOL_HIST:POOL_PAD + ls, :]

    tables = _retention_tables(ls)
    cos = cos_ref[...]
    sin = sin_ref[...]

    def per_sequence(b, carry):
        for h in range(RET_HEADS):
            h0 = h * RET_HEAD_DIM
            y, s_new = _retention_head(
                z_ref[b, :, COL_Q + h0:COL_Q + h0 + RET_HEAD_DIM],
                z_ref[b, :, COL_K + h0:COL_K + h0 + RET_HEAD_DIM],
                z_ref[b, :, COL_V + h0:COL_V + h0 + RET_HEAD_DIM],
                z_ref[b, :, COL_G + h0:COL_G + h0 + RET_HEAD_DIM],
                state_ref[b, h], cos, sin, tables[h])
            staten_ref[b, h] = s_new
            y_ref[b, :, OUT_RET + h0:OUT_RET + h0 + RET_HEAD_DIM] = y
        return carry

    lax.fori_loop(0, nb, per_sequence, 0, unroll=2)


def _mixer_sample(z, cos, sin, conv_w, pool_w, pool_scale, conv_cache, pool_cache, state, state_out,
                  layer, start_pos, nb=8):
    B, ls, _ = z.shape
    seq3 = lambda i: (i, 0, 0)
    lay3 = lambda i: (layer, 0, 0)
    lseq4 = lambda i: (layer, i, 0, 0)
    lseq5 = lambda i: (layer, i, 0, 0, 0)
    in_specs = [
        pl.BlockSpec((nb, ls, IN_COLS), seq3),
        pl.BlockSpec((ls, RET_HEAD_DIM // 2), lambda i: (0, 0)),
        pl.BlockSpec((ls, RET_HEAD_DIM // 2), lambda i: (0, 0)),
        pl.BlockSpec((None, CONV_K, CONV_WIDTH), lay3),
        pl.BlockSpec((None, len(POOL_WINDOWS), POOL_GROUP, POOL_GROUP), lambda i: (layer, 0, 0, 0)),
        pl.BlockSpec((None, 1, POOL_WIDTH), lay3),
        pl.BlockSpec((None, nb, CONV_K - 1, CONV_WIDTH), lseq4),
        pl.BlockSpec((None, nb, POOL_HIST, POOL_WIDTH), lseq4),
        pl.BlockSpec((None, nb, RET_HEADS, RET_HEAD_DIM, RET_HEAD_DIM), lseq5),
    ]
    args = [z, cos, sin, conv_w, pool_w, pool_scale, conv_cache, pool_cache, state]
    aliases = {}
    if state_out is not None:
        in_specs.append(pl.BlockSpec(memory_space=pl.ANY))
        args.append(state_out)
        aliases = {len(args) - 1: 3}
    return pl.pallas_call(
        functools.partial(_mixer_sample_kernel, nb=nb, ls=ls, start_pos=start_pos,
                          aliased=state_out is not None),
        grid=(B // nb,),
        in_specs=in_specs,
        out_specs=[
            pl.BlockSpec((nb, ls, D_MODEL), seq3),
            pl.BlockSpec((nb, CONV_K - 1, CONV_WIDTH), seq3),
            pl.BlockSpec((nb, POOL_HIST, POOL_WIDTH), seq3),
            pl.BlockSpec((None, nb, RET_HEADS, RET_HEAD_DIM, RET_HEAD_DIM), lseq5),
        ],
        out_shape=[
            jax.ShapeDtypeStruct((B, ls, D_MODEL), F32),
            jax.ShapeDtypeStruct((B, CONV_K - 1, CONV_WIDTH), F32),
            jax.ShapeDtypeStruct((B, POOL_HIST, POOL_WIDTH), F32),
            jax.ShapeDtypeStruct(state.shape, F32),
        ],
        scratch_shapes=[
            pltpu.VMEM((nb, CONV_PAD + ls, CONV_WIDTH), F32),
            pltpu.VMEM((nb, POOL_PAD + ls, POOL_WIDTH), F32),
        ],
        input_output_aliases=aliases,
        compiler_params=_params(("parallel",), 48),
        name="mixer_sample",
    )(*args)


def _out_ln_kernel(y_ref, x_ref, w_ref, g_ref, b_ref, o_ref):
    m = _dot(y_ref[...].astype(BF16), w_ref[...])
    o_ref[...] = _layer_norm(DEEPNORM_ALPHA * x_ref[...] + m, g_ref[...], b_ref[...])


def _out_ln(y, x, w, ln_g, ln_b, layer, tm=256):
    T, D = x.shape
    row = lambda i: (i, 0)
    const = lambda i: (layer, 0, 0)
    return pl.pallas_call(
        _out_ln_kernel,
        grid=(T // tm,),
        in_specs=[
            pl.BlockSpec((tm, y.shape[1]), row),
            pl.BlockSpec((tm, D), row),
            pl.BlockSpec((None,) + w.shape[1:], const),
            pl.BlockSpec((None, 1, D), const),
            pl.BlockSpec((None, 1, D), const),
        ],
        out_specs=pl.BlockSpec((tm, D), row),
        out_shape=jax.ShapeDtypeStruct((T, D), F32),
        compiler_params=_params(("parallel",), 48),
        name="out_ln",
    )(y, x, w, ln_g, ln_b)


def _rope_tables(start_pos, length):
    half = RET_HEAD_DIM // 2
    inv = ROPE_BASE ** (-jnp.arange(half, dtype=F32) / half)
    pos = start_pos + jnp.arange(length, dtype=F32)
    ang = pos[:, None] * inv[None, :]
    return jnp.cos(ang), jnp.sin(ang)


def _trunk(x, p, caches, start_pos, w, packed):
    B, L, D = x.shape
    T = B * L
    x = x.reshape(T, D)
    p = p.reshape(DEPTH, T, -1)
    cos, sin = _rope_tables(start_pos, L)
    convs, pools, rets = [], [], []
    state_out = None
    collect = not packed

    def ffn(x, name, ln, layer, **kw):
        if collect:
            *outs, wgu, wd = _ffn_ln(x, (w[name + "_w_gate"], w[name + "_w_up"], w[name + "_w_down"]),
                                     w[ln + "_g"], w[ln + "_b"], layer, **kw)
            packed[name, layer] = (wgu, wd)
            return outs
        return _ffn_ln(x, packed[name, layer], w[ln + "_g"], w[ln + "_b"], layer, **kw)

    for i in range(DEPTH):
        x1, = ffn(x, "ffn1", "ln1", i)
        z = _proj_in(x1, w["w_in"], i)
        if caches is None:
            x2, c_new, p_new, r_new = _mixer_prompt(
                z, x1, cos, sin, w["conv_w"], w["pool_w"], w["pool_scale"], w["w_out"], w["ln2_g"], w["ln2_b"],
                i, B)
            rets.append(r_new)
        else:
            y, c_new, p_new, state_out = _mixer_sample(
                z.reshape(B, L, IN_COLS), cos, sin, w["conv_w"], w["pool_w"], w["pool_scale"],
                caches[0], caches[1], caches[2], state_out, i, start_pos)
            x2 = _out_ln(y.reshape(T, D), x1, w["w_out"], w["ln2_g"], w["ln2_b"], i)
        x, = ffn(x2, "ffn2", "ln3", i, ple=(w["ple_gate"], p, w["ple_proj"]))
        convs.append(c_new)
        pools.append(p_new)
    ret = jnp.stack(rets) if caches is None else state_out
    return x.reshape(B, L, D), jnp.stack(convs), jnp.stack(pools), ret


def kernel(x_prompt, x_sample, p_prompt, p_sample, cache_conv, cache_pool, state_ret, ln1_g, ln1_b, ffn1_w_gate, ffn1_w_up, ffn1_w_down, w_in, conv_w, pool_w, pool_scale, w_out, ln2_g, ln2_b, ffn2_w_gate, ffn2_w_up, ffn2_w_down, ple_gate, ple_proj, ln3_g, ln3_b):
    precast = dict(w_in=w_in, pool_w=pool_w, w_out=w_out, ple_gate=ple_gate, ple_proj=ple_proj)
    row_params = dict(ln1_g=ln1_g, ln1_b=ln1_b, ln2_g=ln2_g, ln2_b=ln2_b, ln3_g=ln3_g, ln3_b=ln3_b,
                      pool_scale=pool_scale)
    w = {k: _to_bf16(v) for k, v in precast.items()}
    w.update({k: v.reshape(DEPTH, 1, -1) for k, v in row_params.items()})
    w.update(conv_w=conv_w, ffn1_w_gate=ffn1_w_gate, ffn1_w_up=ffn1_w_up, ffn1_w_down=ffn1_w_down,
             ffn2_w_gate=ffn2_w_gate, ffn2_w_up=ffn2_w_up, ffn2_w_down=ffn2_w_down)
    packed = {}
    y_sample, conv_s, pool_s, ret_s = _trunk(
        x_sample, p_sample, (cache_conv, cache_pool, state_ret), PAST_LEN, w, packed)
    y_prompt, conv_p, pool_p, ret_p = _trunk(x_prompt, p_prompt, None, 0, w, packed)
    return (y_prompt, y_sample, conv_p, pool_p, ret_p, conv_s, pool_s, ret_s)
```

```python
import functools
import math

import jax
import jax.numpy as jnp
from jax import lax
from jax.experimental import pallas as pl
from jax.experimental.pallas import tpu as pltpu

D_MODEL = 2048
DEPTH = 2
PAST_LEN = 16384
CONV_WIDTH = D_MODEL // 4
POOL_WIDTH = D_MODEL // 4
RET_WIDTH = D_MODEL // 2
CONV_K = 3
POOL_WINDOWS = (2, 4, 8, 16)
POOL_GROUP = POOL_WIDTH // len(POOL_WINDOWS)
POOL_HIST = max(POOL_WINDOWS) - 1
RET_HEADS = 4
RET_HEAD_DIM = RET_WIDTH // RET_HEADS
RET_CHUNK = 128
RET_LOG_GAMMA = tuple(math.log(1.0 - 2.0 ** (-5 - h)) for h in range(RET_HEADS))
ROPE_BASE = 10000.0
IN_COLS = 3 * CONV_WIDTH + POOL_WIDTH + 4 * RET_WIDTH
DEEPNORM_ALPHA = (2 * DEPTH) ** 0.25
LN_EPS = 1e-5

COL_B = 0
COL_C = CONV_WIDTH
COL_H = 2 * CONV_WIDTH
COL_P = 3 * CONV_WIDTH
COL_Q = COL_P + POOL_WIDTH
COL_K = COL_Q + RET_WIDTH
COL_V = COL_K + RET_WIDTH
COL_G = COL_V + RET_WIDTH
OUT_POOL = CONV_WIDTH
OUT_RET = CONV_WIDTH + POOL_WIDTH

SUBLANES = 8
CONV_PAD = SUBLANES
POOL_PAD = 16
WOUT_BLOCKS = 4

F32 = jnp.float32
BF16 = jnp.bfloat16
MIB = 1024 * 1024


def _params(semantics, vmem_mib):
    return pltpu.CompilerParams(dimension_semantics=semantics, vmem_limit_bytes=vmem_mib * MIB)


def _layer_norm(r, g, b):
    mu = jnp.mean(r, axis=-1, keepdims=True)
    c = r - mu
    var = jnp.mean(c * c, axis=-1, keepdims=True)
    return c * lax.rsqrt(var + LN_EPS) * g + b


def _silu(x):
    return x * jax.nn.sigmoid(x)


def _dot(a, b):
    return jnp.dot(a, b, preferred_element_type=F32)


def _cast_kernel(x_ref, o_ref):
    o_ref[...] = x_ref[...].astype(o_ref.dtype)


def _to_bf16(w, block_bytes=6 * MIB):
    shape = w.shape
    w = w.reshape(shape[0], -1, shape[-1])
    _, R, C = w.shape
    tr = R
    while tr * C * 4 > block_bytes and tr % 2 == 0 and (tr // 2) % 16 == 0:
        tr //= 2
    out = pl.pallas_call(
        _cast_kernel,
        grid=(shape[0], R // tr),
        in_specs=[pl.BlockSpec((None, tr, C), lambda d, r: (d, r, 0))],
        out_specs=pl.BlockSpec((None, tr, C), lambda d, r: (d, r, 0)),
        out_shape=jax.ShapeDtypeStruct(w.shape, BF16),
        compiler_params=_params(("parallel", "parallel"), 40),
        name="to_bf16",
    )(w)
    return out.reshape(shape)


FFN_PACK = 256
FFN_ROWS = 512


def _ffn_ln_kernel(*refs, nf, tf, packed, ple_cols):
    refs = list(refs)
    x_ref = refs.pop(0)
    if packed:
        wgu_ref, wd_ref = refs[:2]
        refs = refs[2:]
    else:
        wg_ref, wu_ref, wd_ref = refs[:3]
        refs = refs[3:]
    g_ref, b_ref = refs[:2]
    refs = refs[2:]
    if ple_cols:
        gate_ref, p_ref, proj_ref = refs[:3]
        refs = refs[3:]
    o_ref = refs.pop(0)
    if not packed:
        wgu_out_ref, wd_out_ref = refs[:2]
        refs = refs[2:]
    xb_ref = refs.pop(0)
    f = pl.program_id(1)
    n_ple = o_ref.shape[1] // ple_cols if ple_cols else 0

    @pl.when(f == 0)
    def _():
        x = x_ref[...]
        xb_ref[...] = x.astype(BF16)
        o_ref[...] = DEEPNORM_ALPHA * x

    @pl.when(f < nf)
    def _():
        if packed:
            wd = wd_ref[...]
        else:
            wg = wg_ref[...].astype(BF16)
            wu = wu_ref[...].astype(BF16)
            wd = wd_ref[...].astype(BF16)
            wgu_out_ref[...] = jnp.concatenate([wg, wu], axis=1)
            wd_out_ref[...] = wd
        for r0 in range(0, o_ref.shape[0], FFN_ROWS):
            xb = xb_ref[r0:r0 + FFN_ROWS, :]
            if packed:
                hs = []
                for s in range(tf // FFN_PACK):
                    c0 = 2 * FFN_PACK * s
                    gu = _dot(xb, wgu_ref[:, c0:c0 + 2 * FFN_PACK])
                    hs.append((_silu(gu[:, :FFN_PACK]) * gu[:, FFN_PACK:]).astype(BF16))
                h = jnp.concatenate(hs, axis=1)
            else:
                h = (_silu(_dot(xb, wg)) * _dot(xb, wu)).astype(BF16)
            o_ref[r0:r0 + FFN_ROWS, :] += 0.5 * _dot(h, wd)

    for j in range(n_ple):
        @pl.when(f == nf + j)
        def _():
            gate = jax.nn.sigmoid(_dot(xb_ref[...], gate_ref[...]))
            o_ref[:, j * ple_cols:(j + 1) * ple_cols] += gate * _dot(p_ref[...].astype(BF16), proj_ref[...])

    @pl.when(f == nf + n_ple - 1)
    def _():
        o_ref[...] = _layer_norm(o_ref[...], g_ref[...], b_ref[...])


def _ffn_ln(x, weights, ln_g, ln_b, layer, ple=None, tm=1024, packed_tf=512, ple_cols=512):
    T, D = x.shape
    packed = len(weights) == 2
    tf = packed_tf if packed else FFN_PACK
    F = weights[1].shape[0] if packed else weights[0].shape[2]
    nf = F // tf
    n_ple = D // ple_cols if ple is not None else 0
    row = lambda i, f: (i, 0)
    const = lambda i, f: (layer, 0, 0)
    fclamp = lambda f: jnp.minimum(f, nf - 1)
    if packed:
        w_specs = [pl.BlockSpec((D, 2 * tf), lambda i, f: (0, fclamp(f))),
                   pl.BlockSpec((tf, D), lambda i, f: (fclamp(f), 0))]
        single = None
    else:
        assert T == tm
        w_specs = [pl.BlockSpec((None, D, tf), lambda i, f: (layer, 0, fclamp(f))),
                   pl.BlockSpec((None, D, tf), lambda i, f: (layer, 0, fclamp(f))),
                   pl.BlockSpec((None, tf, D), lambda i, f: (layer, fclamp(f), 0))]
        single = pl.Buffered(1)
    in_specs = [pl.BlockSpec((tm, D), row, pipeline_mode=single)] + w_specs + [
        pl.BlockSpec((None, 1, D), const),
        pl.BlockSpec((None, 1, D), const),
    ]
    args = [x, *weights, ln_g, ln_b]
    if ple is not None:
        gate_w, p, proj_w = ple
        P = p.shape[2]
        pcol = lambda i, f: (layer, 0, jnp.maximum(f - nf, 0))
        in_specs += [
            pl.BlockSpec((None, D, ple_cols), pcol),
            pl.BlockSpec((None, tm, P), lambda i, f: (layer, i, 0)),
            pl.BlockSpec((None, P, ple_cols), pcol),
        ]
        args += [gate_w, p, proj_w]
    out_shape = [jax.ShapeDtypeStruct((T, D), F32)]
    out_specs = [pl.BlockSpec((tm, D), row, pipeline_mode=single)]
    if not packed:
        out_shape += [jax.ShapeDtypeStruct((D, 2 * F), BF16), jax.ShapeDtypeStruct((F, D), BF16)]
        out_specs += [pl.BlockSpec((D, 2 * tf), lambda i, f: (0, fclamp(f))),
                      pl.BlockSpec((tf, D), lambda i, f: (fclamp(f), 0))]
    return pl.pallas_call(
        functools.partial(_ffn_ln_kernel, nf=nf, tf=tf, packed=packed,
                          ple_cols=ple_cols if ple is not None else 0),
        grid=(T // tm, nf + n_ple),
        in_specs=in_specs,
        out_specs=out_specs,
        out_shape=out_shape,
        scratch_shapes=[pltpu.VMEM((tm, D), BF16)],
        compiler_params=_params(("parallel", "arbitrary"), 62),
        name="ffn_ln",
    )(*args)


def _proj_in_kernel(x_ref, w_ref, o_ref, xb_ref):
    @pl.when(pl.program_id(1) == 0)
    def _():
        xb_ref[...] = x_ref[...].astype(BF16)

    o_ref[...] = _dot(xb_ref[...], w_ref[...])


def _proj_in(x, w, layer, tm=1024, tn=2048):
    T, D = x.shape
    N = w.shape[2]
    return pl.pallas_call(
        _proj_in_kernel,
        grid=(T // tm, N // tn),
        in_specs=[pl.BlockSpec((tm, D), lambda i, n: (i, 0)),
                  pl.BlockSpec((None, D, tn), lambda i, n: (layer, 0, n))],
        out_specs=pl.BlockSpec((tm, tn), lambda i, n: (i, n)),
        out_shape=jax.ShapeDtypeStruct((T, N), F32),
        scratch_shapes=[pltpu.VMEM((tm, D), BF16)],
        compiler_params=_params(("parallel", "arbitrary"), 58),
        name="proj_in",
    )(x, w)


def _rotary(x, cos, sin):
    half = RET_HEAD_DIM // 2
    x1, x2 = x[:, :half], x[:, half:]
    return jnp.concatenate([x1 * cos - x2 * sin, x1 * sin + x2 * cos], axis=-1)


def _retention_tables(L):
    row = lax.broadcasted_iota(jnp.int32, (L, L), 0)
    col = lax.broadcasted_iota(jnp.int32, (L, L), 1)
    diff = (row - col).astype(F32)
    causal = row >= col
    idx = lax.broadcasted_iota(jnp.int32, (L, 1), 0).astype(F32)
    tables = []
    for lg in RET_LOG_GAMMA:
        decay = jnp.where(causal, jnp.exp(jnp.where(causal, diff, 0.0) * lg), 0.0)
        q_decay = jnp.exp((idx + 1.0) * lg)
        k_decay = jnp.exp((L - 1.0 - idx) * lg)
        tables.append((decay, q_decay, k_decay, math.exp(L * lg)))
    return tables


def _retention_head(q, k, v, gate, S, cos, sin, table):
    decay, q_decay, k_decay, s_decay = table
    qr = _rotary(q, cos, sin)
    kr = _rotary(k, cos, sin) * (RET_HEAD_DIM ** -0.5)
    qb = qr.astype(BF16)
    vb = v.astype(BF16)
    scores = lax.dot_general(qb, kr.astype(BF16), (((1,), (1,)), ((), ())), preferred_element_type=F32)
    inner = _dot((scores * decay).astype(BF16), vb)
    cross = _dot(qb, S.astype(BF16)) * q_decay
    o = inner + cross
    kd = (kr * k_decay).astype(BF16)
    S_new = s_decay * S + lax.dot_general(kd, vb, (((0,), (0,)), ((), ())), preferred_element_type=F32)
    mu = jnp.mean(o, axis=-1, keepdims=True)
    c = o - mu
    var = jnp.mean(c * c, axis=-1, keepdims=True)
    y = c * lax.rsqrt(var + LN_EPS) * _silu(gate)
    return y, S_new


def _window_count(w, pos):
    return jnp.minimum(jnp.float32(w), pos + 1.0)


def _mixer_prompt_kernel(z_ref, cos_ref, sin_ref, convw_ref, poolw_ref, pscale_ref,
                         x_ref, g_ref, b_ref, *rest, tl, nl):
    wout_refs = rest[:WOUT_BLOCKS]
    o_ref, convn_ref, pooln_ref, retn_ref, y_ref, s_ref, ubuf_ref, pbuf_ref = rest[WOUT_BLOCKS:]
    l = pl.program_id(1)

    @pl.when(l == 0)
    def _():
        s_ref[...] = jnp.zeros_like(s_ref)
        ubuf_ref[0:CONV_PAD, :] = jnp.zeros((CONV_PAD, CONV_WIDTH), F32)
        pbuf_ref[0:POOL_PAD, :] = jnp.zeros((POOL_PAD, POOL_WIDTH), F32)

    u = z_ref[:, COL_C:COL_C + CONV_WIDTH] * z_ref[:, COL_H:COL_H + CONV_WIDTH]
    ubuf_ref[CONV_PAD:CONV_PAD + tl, :] = u
    cw = convw_ref[...]
    conv = (ubuf_ref[CONV_PAD - 2:CONV_PAD - 2 + tl, :] * cw[0:1, :]
            + ubuf_ref[CONV_PAD - 1:CONV_PAD - 1 + tl, :] * cw[1:2, :]
            + u * cw[2:3, :])
    y_ref[:, 0:CONV_WIDTH] = (z_ref[:, COL_B:COL_B + CONV_WIDTH] * conv).astype(y_ref.dtype)

    pbuf_ref[POOL_PAD:POOL_PAD + tl, :] = z_ref[:, COL_P:COL_P + POOL_WIDTH]
    pos = (l * tl + lax.broadcasted_iota(jnp.int32, (tl, 1), 0)).astype(F32)
    for gi, w in enumerate(POOL_WINDOWS):
        c0 = gi * POOL_GROUP
        tok = pbuf_ref[POOL_PAD:POOL_PAD + tl, c0:c0 + POOL_GROUP]
        win = tok
        for j in range(1, w):
            win = win + pbuf_ref[POOL_PAD - j:POOL_PAD - j + tl, c0:c0 + POOL_GROUP]
        d = win / _window_count(w, pos) - tok
        yb = _dot(d.astype(BF16), poolw_ref[gi]) * pscale_ref[:, c0:c0 + POOL_GROUP]
        y_ref[:, OUT_POOL + c0:OUT_POOL + c0 + POOL_GROUP] = yb.astype(y_ref.dtype)

    tables = _retention_tables(RET_CHUNK)
    for c in range(tl // RET_CHUNK):
        r0 = c * RET_CHUNK
        cos = cos_ref[r0:r0 + RET_CHUNK, :]
        sin = sin_ref[r0:r0 + RET_CHUNK, :]
        for h in range(RET_HEADS):
            h0 = h * RET_HEAD_DIM
            y, s_new = _retention_head(
                z_ref[r0:r0 + RET_CHUNK, COL_Q + h0:COL_Q + h0 + RET_HEAD_DIM],
                z_ref[r0:r0 + RET_CHUNK, COL_K + h0:COL_K + h0 + RET_HEAD_DIM],
                z_ref[r0:r0 + RET_CHUNK, COL_V + h0:COL_V + h0 + RET_HEAD_DIM],
                z_ref[r0:r0 + RET_CHUNK, COL_G + h0:COL_G + h0 + RET_HEAD_DIM],
                s_ref[h], cos, sin, tables[h])
            s_ref[h] = s_new
            y_ref[r0:r0 + RET_CHUNK, OUT_RET + h0:OUT_RET + h0 + RET_HEAD_DIM] = y.astype(y_ref.dtype)
        yc = y_ref[r0:r0 + RET_CHUNK, :]
        m = jnp.concatenate([_dot(yc, w_ref[...]) for w_ref in wout_refs], axis=1)
        o_ref[r0:r0 + RET_CHUNK, :] = _layer_norm(
            DEEPNORM_ALPHA * x_ref[r0:r0 + RET_CHUNK, :] + m, g_ref[...], b_ref[...])

    @pl.when(l == nl - 1)
    def _():
        convn_ref[0] = ubuf_ref[CONV_PAD + tl - (CONV_K - 1):CONV_PAD + tl, :]
        pooln_ref[0] = pbuf_ref[POOL_PAD + tl - POOL_HIST:POOL_PAD + tl, :]
        retn_ref[0] = s_ref[...]

    ubuf_ref[0:CONV_PAD, :] = ubuf_ref[tl:tl + CONV_PAD, :]
    pbuf_ref[0:POOL_PAD, :] = pbuf_ref[tl:tl + POOL_PAD, :]


def _mixer_prompt(z, x1, cos, sin, conv_w, pool_w, pool_scale, w_out, ln_g, ln_b, layer, batch, tl=256):
    T = z.shape[0]
    L = T // batch
    nl = L // tl
    row = lambda b, l: (b * nl + l, 0)
    lay3 = lambda b, l: (layer, 0, 0)
    return pl.pallas_call(
        functools.partial(_mixer_prompt_kernel, tl=tl, nl=nl),
        grid=(batch, nl),
        in_specs=[
            pl.BlockSpec((tl, IN_COLS), row),
            pl.BlockSpec((tl, RET_HEAD_DIM // 2), lambda b, l: (l, 0)),
            pl.BlockSpec((tl, RET_HEAD_DIM // 2), lambda b, l: (l, 0)),
            pl.BlockSpec((None, CONV_K, CONV_WIDTH), lay3),
            pl.BlockSpec((None, len(POOL_WINDOWS), POOL_GROUP, POOL_GROUP), lambda b, l: (layer, 0, 0, 0)),
            pl.BlockSpec((None, 1, POOL_WIDTH), lay3),
            pl.BlockSpec((tl, D_MODEL), row),
            pl.BlockSpec((None, 1, D_MODEL), lay3),
            pl.BlockSpec((None, 1, D_MODEL), lay3),
        ] + [
            pl.BlockSpec((None, D_MODEL, D_MODEL // WOUT_BLOCKS), (lambda j: lambda b, l: (layer, 0, j))(j),
                         pipeline_mode=pl.Buffered(1))
            for j in range(WOUT_BLOCKS)
        ],
        out_specs=[
            pl.BlockSpec((tl, D_MODEL), row),
            pl.BlockSpec((1, CONV_K - 1, CONV_WIDTH), lambda b, l: (b, 0, 0)),
            pl.BlockSpec((1, POOL_HIST, POOL_WIDTH), lambda b, l: (b, 0, 0)),
            pl.BlockSpec((1, RET_HEADS, RET_HEAD_DIM, RET_HEAD_DIM), lambda b, l: (b, 0, 0, 0)),
        ],
        out_shape=[
            jax.ShapeDtypeStruct((T, D_MODEL), F32),
            jax.ShapeDtypeStruct((batch, CONV_K - 1, CONV_WIDTH), F32),
            jax.ShapeDtypeStruct((batch, POOL_HIST, POOL_WIDTH), F32),
            jax.ShapeDtypeStruct((batch, RET_HEADS, RET_HEAD_DIM, RET_HEAD_DIM), F32),
        ],
        scratch_shapes=[
            pltpu.VMEM((tl, D_MODEL), BF16),
            pltpu.VMEM((RET_HEADS, RET_HEAD_DIM, RET_HEAD_DIM), F32),
            pltpu.VMEM((CONV_PAD + tl, CONV_WIDTH), F32),
            pltpu.VMEM((POOL_PAD + tl, POOL_WIDTH), F32),
        ],
        compiler_params=_params(("parallel", "arbitrary"), 48),
        name="mixer_prompt",
    )(z, cos, sin, conv_w, pool_w, pool_scale, x1, ln_g, ln_b, *([w_out] * WOUT_BLOCKS))


def _mixer_sample_kernel(z_ref, cos_ref, sin_ref, convw_ref, poolw_ref, pscale_ref,
                         convc_ref, poolc_ref, state_ref, *rest, nb, ls, start_pos, aliased):
    if aliased:
        rest = rest[1:]
    y_ref, convn_ref, pooln_ref, staten_ref, ubuf_ref, pbuf_ref = rest
    u = z_ref[:, :, COL_C:COL_C + CONV_WIDTH] * z_ref[:, :, COL_H:COL_H + CONV_WIDTH]
    ubuf_ref[:, CONV_PAD - (CONV_K - 1):CONV_PAD, :] = convc_ref[...]
    ubuf_ref[:, CONV_PAD:CONV_PAD + ls, :] = u
    cw = convw_ref[...]
    conv = (ubuf_ref[:, CONV_PAD - 2:CONV_PAD - 2 + ls, :] * cw[0:1, :]
            + ubuf_ref[:, CONV_PAD - 1:CONV_PAD - 1 + ls, :] * cw[1:2, :]
            + u * cw[2:3, :])
    y_ref[:, :, 0:CONV_WIDTH] = z_ref[:, :, COL_B:COL_B + CONV_WIDTH] * conv
    convn_ref[...] = ubuf_ref[:, CONV_PAD + ls - (CONV_K - 1):CONV_PAD + ls, :]

    pbuf_ref[:, POOL_PAD - POOL_HIST:POOL_PAD, :] = poolc_ref[...]
    pbuf_ref[:, POOL_PAD:POOL_PAD + ls, :] = z_ref[:, :, COL_P:COL_P + POOL_WIDTH]
    pos = (start_pos + lax.broadcasted_iota(jnp.int32, (1, ls, 1), 1)).astype(F32)
    for gi, w in enumerate(POOL_WINDOWS):
        c0 = gi * POOL_GROUP
        tok = pbuf_ref[:, POOL_PAD:POOL_PAD + ls, c0:c0 + POOL_GROUP]
        win = tok
        for j in range(1, w):
            win = win + pbuf_ref[:, POOL_PAD - j:POOL_PAD - j + ls, c0:c0 + POOL_GROUP]
        d = win / _window_count(w, pos) - tok
        yb = _dot(d.reshape(nb * ls, POOL_GROUP).astype(BF16), poolw_ref[gi]).reshape(nb, ls, POOL_GROUP)
        y_ref[:, :, OUT_POOL + c0:OUT_POOL + c0 + POOL_GROUP] = yb * pscale_ref[:, c0:c0 + POOL_GROUP]
    pooln_ref[...] = pbuf_ref[:, POOL_PAD + ls - POOL_HIST:POOL_PAD + ls, :]

    tables = _retention_tables(ls)
    cos = cos_ref[...]
    sin = sin_ref[...]

    def per_sequence(b, carry):
        for h in range(RET_HEADS):
            h0 = h * RET_HEAD_DIM
            y, s_new = _retention_head(
                z_ref[b, :, COL_Q + h0:COL_Q + h0 + RET_HEAD_DIM],
                z_ref[b, :, COL_K + h0:COL_K + h0 + RET_HEAD_DIM],
                z_ref[b, :, COL_V + h0:COL_V + h0 + RET_HEAD_DIM],
                z_ref[b, :, COL_G + h0:COL_G + h0 + RET_HEAD_DIM],
                state_ref[b, h], cos, sin, tables[h])
            staten_ref[b, h] = s_new
            y_ref[b, :, OUT_RET + h0:OUT_RET + h0 + RET_HEAD_DIM] = y
        return carry

    lax.fori_loop(0, nb, per_sequence, 0, unroll=2)


def _mixer_sample(z, cos, sin, conv_w, pool_w, pool_scale, conv_cache, pool_cache, state, state_out,
                  layer, start_pos, nb=8):
    B, ls, _ = z.shape
    seq3 = lambda i: (i, 0, 0)
    lay3 = lambda i: (layer, 0, 0)
    lseq4 = lambda i: (layer, i, 0, 0)
    lseq5 = lambda i: (layer, i, 0, 0, 0)
    in_specs = [
        pl.BlockSpec((nb, ls, IN_COLS), seq3),
        pl.BlockSpec((ls, RET_HEAD_DIM // 2), lambda i: (0, 0)),
        pl.BlockSpec((ls, RET_HEAD_DIM // 2), lambda i: (0, 0)),
        pl.BlockSpec((None, CONV_K, CONV_WIDTH), lay3),
        pl.BlockSpec((None, len(POOL_WINDOWS), POOL_GROUP, POOL_GROUP), lambda i: (layer, 0, 0, 0)),
        pl.BlockSpec((None, 1, POOL_WIDTH), lay3),
        pl.BlockSpec((None, nb, CONV_K - 1, CONV_WIDTH), lseq4),
        pl.BlockSpec((None, nb, POOL_HIST, POOL_WIDTH), lseq4),
        pl.BlockSpec((None, nb, RET_HEADS, RET_HEAD_DIM, RET_HEAD_DIM), lseq5),
    ]
    args = [z, cos, sin, conv_w, pool_w, pool_scale, conv_cache, pool_cache, state]
    aliases = {}
    if state_out is not None:
        in_specs.append(pl.BlockSpec(memory_space=pl.ANY))
        args.append(state_out)
        aliases = {len(args) - 1: 3}
    return pl.pallas_call(
        functools.partial(_mixer_sample_kernel, nb=nb, ls=ls, start_pos=start_pos,
                          aliased=state_out is not None),
        grid=(B // nb,),
        in_specs=in_specs,
        out_specs=[
            pl.BlockSpec((nb, ls, D_MODEL), seq3),
            pl.BlockSpec((nb, CONV_K - 1, CONV_WIDTH), seq3),
            pl.BlockSpec((nb, POOL_HIST, POOL_WIDTH), seq3),
            pl.BlockSpec((None, nb, RET_HEADS, RET_HEAD_DIM, RET_HEAD_DIM), lseq5),
        ],
        out_shape=[
            jax.ShapeDtypeStruct((B, ls, D_MODEL), F32),
            jax.ShapeDtypeStruct((B, CONV_K - 1, CONV_WIDTH), F32),
            jax.ShapeDtypeStruct((B, POOL_HIST, POOL_WIDTH), F32),
            jax.ShapeDtypeStruct(state.shape, F32),
        ],
        scratch_shapes=[
            pltpu.VMEM((nb, CONV_PAD + ls, CONV_WIDTH), F32),
            pltpu.VMEM((nb, POOL_PAD + ls, POOL_WIDTH), F32),
        ],
        input_output_aliases=aliases,
        compiler_params=_params(("parallel",), 48),
        name="mixer_sample",
    )(*args)


def _out_ln_kernel(y_ref, x_ref, w_ref, g_ref, b_ref, o_ref):
    m = _dot(y_ref[...].astype(BF16), w_ref[...])
    o_ref[...] = _layer_norm(DEEPNORM_ALPHA * x_ref[...] + m, g_ref[...], b_ref[...])


def _out_ln(y, x, w, ln_g, ln_b, layer, tm=256):
    T, D = x.shape
    row = lambda i: (i, 0)
    const = lambda i: (layer, 0, 0)
    return pl.pallas_call(
        _out_ln_kernel,
        grid=(T // tm,),
        in_specs=[
            pl.BlockSpec((tm, y.shape[1]), row),
            pl.BlockSpec((tm, D), row),
            pl.BlockSpec((None,) + w.shape[1:], const),
            pl.BlockSpec((None, 1, D), const),
            pl.BlockSpec((None, 1, D), const),
        ],
        out_specs=pl.BlockSpec((tm, D), row),
        out_shape=jax.ShapeDtypeStruct((T, D), F32),
        compiler_params=_params(("parallel",), 48),
        name="out_ln",
    )(y, x, w, ln_g, ln_b)


def _rope_tables(start_pos, length):
    half = RET_HEAD_DIM // 2
    inv = ROPE_BASE ** (-jnp.arange(half, dtype=F32) / half)
    pos = start_pos + jnp.arange(length, dtype=F32)
    ang = pos[:, None] * inv[None, :]
    return jnp.cos(ang), jnp.sin(ang)


def _trunk(x, p, caches, start_pos, w, packed):
    B, L, D = x.shape
    T = B * L
    x = x.reshape(T, D)
    p = p.reshape(DEPTH, T, -1)
    cos, sin = _rope_tables(start_pos, L)
    convs, pools, rets = [], [], []
    state_out = None
    collect = not packed

    def ffn(x, name, ln, layer, **kw):
        if collect:
            *outs, wgu, wd = _ffn_ln(x, (w[name + "_w_gate"], w[name + "_w_up"], w[name + "_w_down"]),
                                     w[ln + "_g"], w[ln + "_b"], layer, **kw)
            packed[name, layer] = (wgu, wd)
            return outs
        return _ffn_ln(x, packed[name, layer], w[ln + "_g"], w[ln + "_b"], layer, **kw)

    for i in range(DEPTH):
        x1, = ffn(x, "ffn1", "ln1", i)
        z = _proj_in(x1, w["w_in"], i)
        if caches is None:
            x2, c_new, p_new, r_new = _mixer_prompt(
                z, x1, cos, sin, w["conv_w"], w["pool_w"], w["pool_scale"], w["w_out"], w["ln2_g"], w["ln2_b"],
                i, B)
            rets.append(r_new)
        else:
            y, c_new, p_new, state_out = _mixer_sample(
                z.reshape(B, L, IN_COLS), cos, sin, w["conv_w"], w["pool_w"], w["pool_scale"],
                caches[0], caches[1], caches[2], state_out, i, start_pos)
            x2 = _out_ln(y.reshape(T, D), x1, w["w_out"], w["ln2_g"], w["ln2_b"], i)
        x, = ffn(x2, "ffn2", "ln3", i, ple=(w["ple_gate"], p, w["ple_proj"]))
        convs.append(c_new)
        pools.append(p_new)
    ret = jnp.stack(rets) if caches is None else state_out
    return x.reshape(B, L, D), jnp.stack(convs), jnp.stack(pools), ret


def kernel(x_prompt, x_sample, p_prompt, p_sample, cache_conv, cache_pool, state_ret, ln1_g, ln1_b, ffn1_w_gate, ffn1_w_up, ffn1_w_down, w_in, conv_w, pool_w, pool_scale, w_out, ln2_g, ln2_b, ffn2_w_gate, ffn2_w_up, ffn2_w_down, ple_gate, ple_proj, ln3_g, ln3_b):
    precast = dict(w_in=w_in, pool_w=pool_w, w_out=w_out, ple_gate=ple_gate, ple_proj=ple_proj)
    row_params = dict(ln1_g=ln1_g, ln1_b=ln1_b, ln2_g=ln2_g, ln2_b=ln2_b, ln3_g=ln3_g, ln3_b=ln3_b,
                      pool_scale=pool_scale)
    w = {k: _to_bf16(v) for k, v in precast.items()}
    w.update({k: v.reshape(DEPTH, 1, -1) for k, v in row_params.items()})
    w.update(conv_w=conv_w, ffn1_w_gate=ffn1_w_gate, ffn1_w_up=ffn1_w_up, ffn1_w_down=ffn1_w_down,
             ffn2_w_gate=ffn2_w_gate, ffn2_w_up=ffn2_w_up, ffn2_w_down=ffn2_w_down)
    packed = {}
    y_sample, conv_s, pool_s, ret_s = _trunk(
        x_sample, p_sample, (cache_conv, cache_pool, state_ret), PAST_LEN, w, packed)
    y_prompt, conv_p, pool_p, ret_p = _trunk(x_prompt, p_prompt, None, 0, w, packed)
    return (y_prompt, y_sample, conv_p, pool_p, ret_p, conv_s, pool_s, ret_s)
```

```python
import functools
import math

import jax
import jax.numpy as jnp
from jax import lax
from jax.experimental import pallas as pl
from jax.experimental.pallas import tpu as pltpu

D_MODEL = 2048
DEPTH = 2
PAST_LEN = 16384
CONV_WIDTH = D_MODEL // 4
POOL_WIDTH = D_MODEL // 4
RET_WIDTH = D_MODEL // 2
CONV_K = 3
POOL_WINDOWS = (2, 4, 8, 16)
POOL_GROUP = POOL_WIDTH // len(POOL_WINDOWS)
POOL_HIST = max(POOL_WINDOWS) - 1
RET_HEADS = 4
RET_HEAD_DIM = RET_WIDTH // RET_HEADS
RET_CHUNK = 128
RET_LOG_GAMMA = tuple(math.log(1.0 - 2.0 ** (-5 - h)) for h in range(RET_HEADS))
ROPE_BASE = 10000.0
IN_COLS = 3 * CONV_WIDTH + POOL_WIDTH + 4 * RET_WIDTH
DEEPNORM_ALPHA = (2 * DEPTH) ** 0.25
LN_EPS = 1e-5

COL_B = 0
COL_C = CONV_WIDTH
COL_H = 2 * CONV_WIDTH
COL_P = 3 * CONV_WIDTH
COL_Q = COL_P + POOL_WIDTH
COL_K = COL_Q + RET_WIDTH
COL_V = COL_K + RET_WIDTH
COL_G = COL_V + RET_WIDTH
OUT_POOL = CONV_WIDTH
OUT_RET = CONV_WIDTH + POOL_WIDTH

SUBLANES = 8
CONV_PAD = SUBLANES
POOL_PAD = 16
WOUT_BLOCKS = 4

F32 = jnp.float32
BF16 = jnp.bfloat16
MIB = 1024 * 1024


def _params(semantics, vmem_mib):
    return pltpu.CompilerParams(dimension_semantics=semantics, vmem_limit_bytes=vmem_mib * MIB)


def _layer_norm(r, g, b):
    mu = jnp.mean(r, axis=-1, keepdims=True)
    c = r - mu
    var = jnp.mean(c * c, axis=-1, keepdims=True)
    return c * lax.rsqrt(var + LN_EPS) * g + b


def _silu(x):
    return x * jax.nn.sigmoid(x)


def _dot(a, b):
    return jnp.dot(a, b, preferred_element_type=F32)


def _cast_kernel(x_ref, o_ref):
    o_ref[...] = x_ref[...].astype(o_ref.dtype)


def _to_bf16(w, block_bytes=6 * MIB):
    shape = w.shape
    w = w.reshape(shape[0], -1, shape[-1])
    _, R, C = w.shape
    tr = R
    while tr * C * 4 > block_bytes and tr % 2 == 0 and (tr // 2) % 16 == 0:
        tr //= 2
    out = pl.pallas_call(
        _cast_kernel,
        grid=(shape[0], R // tr),
        in_specs=[pl.BlockSpec((None, tr, C), lambda d, r: (d, r, 0))],
        out_specs=pl.BlockSpec((None, tr, C), lambda d, r: (d, r, 0)),
        out_shape=jax.ShapeDtypeStruct(w.shape, BF16),
        compiler_params=_params(("parallel", "parallel"), 40),
        name="to_bf16",
    )(w)
    return out.reshape(shape)


FFN_PACK = 256
FFN_ROWS = 512


def _ffn_ln_kernel(*refs, nf, tf, packed, ple_cols):
    refs = list(refs)
    x_ref = refs.pop(0)
    if packed:
        wgu_ref, wd_ref = refs[:2]
        refs = refs[2:]
    else:
        wg_ref, wu_ref, wd_ref = refs[:3]
        refs = refs[3:]
    g_ref, b_ref = refs[:2]
    refs = refs[2:]
    if ple_cols:
        gate_ref, p_ref, proj_ref = refs[:3]
        refs = refs[3:]
    o_ref = refs.pop(0)
    if not packed:
        wgu_out_ref, wd_out_ref = refs[:2]
        refs = refs[2:]
    xb_ref = refs.pop(0)
    f = pl.program_id(1)
    n_ple = o_ref.shape[1] // ple_cols if ple_cols else 0

    @pl.when(f == 0)
    def _():
        x = x_ref[...]
        xb_ref[...] = x.astype(BF16)
        o_ref[...] = DEEPNORM_ALPHA * x

    @pl.when(f < nf)
    def _():
        if packed:
            wd = wd_ref[...]
        else:
            wg = wg_ref[...].astype(BF16)
            wu = wu_ref[...].astype(BF16)
            wd = wd_ref[...].astype(BF16)
            wgu_out_ref[...] = jnp.concatenate([wg, wu], axis=1)
            wd_out_ref[...] = wd
        for r0 in range(0, o_ref.shape[0], FFN_ROWS):
            xb = xb_ref[r0:r0 + FFN_ROWS, :]
            if packed:
                hs = []
                for s in range(tf // FFN_PACK):
                    c0 = 2 * FFN_PACK * s
                    gu = _dot(xb, wgu_ref[:, c0:c0 + 2 * FFN_PACK])
                    hs.append((_silu(gu[:, :FFN_PACK]) * gu[:, FFN_PACK:]).astype(BF16))
                h = jnp.concatenate(hs, axis=1)
            else:
                h = (_silu(_dot(xb, wg)) * _dot(xb, wu)).astype(BF16)
            o_ref[r0:r0 + FFN_ROWS, :] += 0.5 * _dot(h, wd)

    for j in range(n_ple):
        @pl.when(f == nf + j)
        def _():
            gate = jax.nn.sigmoid(_dot(xb_ref[...], gate_ref[...]))
            o_ref[:, j * ple_cols:(j + 1) * ple_cols] += gate * _dot(p_ref[...].astype(BF16), proj_ref[...])

    @pl.when(f == nf + n_ple - 1)
    def _():
        o_ref[...] = _layer_norm(o_ref[...], g_ref[...], b_ref[...])


def _ffn_ln(x, weights, ln_g, ln_b, layer, ple=None, tm=1024, packed_tf=512, ple_cols=512):
    T, D = x.shape
    packed = len(weights) == 2
    tf = packed_tf if packed else FFN_PACK
    F = weights[1].shape[0] if packed else weights[0].shape[2]
    nf = F // tf
    n_ple = D // ple_cols if ple is not None else 0
    row = lambda i, f: (i, 0)
    const = lambda i, f: (layer, 0, 0)
    fclamp = lambda f: jnp.minimum(f, nf - 1)
    if packed:
        w_specs = [pl.BlockSpec((D, 2 * tf), lambda i, f: (0, fclamp(f))),
                   pl.BlockSpec((tf, D), lambda i, f: (fclamp(f), 0))]
        single = None
    else:
        assert T == tm
        w_specs = [pl.BlockSpec((None, D, tf), lambda i, f: (layer, 0, fclamp(f))),
                   pl.BlockSpec((None, D, tf), lambda i, f: (layer, 0, fclamp(f))),
                   pl.BlockSpec((None, tf, D), lambda i, f: (layer, fclamp(f), 0))]
        single = pl.Buffered(1)
    in_specs = [pl.BlockSpec((tm, D), row, pipeline_mode=single)] + w_specs + [
        pl.BlockSpec((None, 1, D), const),
        pl.BlockSpec((None, 1, D), const),
    ]
    args = [x, *weights, ln_g, ln_b]
    if ple is not None:
        gate_w, p, proj_w = ple
        P = p.shape[2]
        pcol = lambda i, f: (layer, 0, jnp.maximum(f - nf, 0))
        in_specs += [
            pl.BlockSpec((None, D, ple_cols), pcol),
            pl.BlockSpec((None, tm, P), lambda i, f: (layer, i, 0)),
            pl.BlockSpec((None, P, ple_cols), pcol),
        ]
        args += [gate_w, p, proj_w]
    out_shape = [jax.ShapeDtypeStruct((T, D), F32)]
    out_specs = [pl.BlockSpec((tm, D), row, pipeline_mode=single)]
    if not packed:
        out_shape += [jax.ShapeDtypeStruct((D, 2 * F), BF16), jax.ShapeDtypeStruct((F, D), BF16)]
        out_specs += [pl.BlockSpec((D, 2 * tf), lambda i, f: (0, fclamp(f))),
                      pl.BlockSpec((tf, D), lambda i, f: (fclamp(f), 0))]
    return pl.pallas_call(
        functools.partial(_ffn_ln_kernel, nf=nf, tf=tf, packed=packed,
                          ple_cols=ple_cols if ple is not None else 0),
        grid=(T // tm, nf + n_ple),
        in_specs=in_specs,
        out_specs=out_specs,
        out_shape=out_shape,
        scratch_shapes=[pltpu.VMEM((tm, D), BF16)],
        compiler_params=_params(("parallel", "arbitrary"), 62),
        name="ffn_ln",
    )(*args)


def _proj_in_kernel(x_ref, w_ref, o_ref, xb_ref):
    @pl.when(pl.program_id(1) == 0)
    def _():
        xb_ref[...] = x_ref[...].astype(BF16)

    o_ref[...] = _dot(xb_ref[...], w_ref[...])


def _proj_in(x, w, layer, tm=1024, tn=2048):
    T, D = x.shape
    N = w.shape[2]
    return pl.pallas_call(
        _proj_in_kernel,
        grid=(T // tm, N // tn),
        in_specs=[pl.BlockSpec((tm, D), lambda i, n: (i, 0)),
                  pl.BlockSpec((None, D, tn), lambda i, n: (layer, 0, n))],
        out_specs=pl.BlockSpec((tm, tn), lambda i, n: (i, n)),
        out_shape=jax.ShapeDtypeStruct((T, N), F32),
        scratch_shapes=[pltpu.VMEM((tm, D), BF16)],
        compiler_params=_params(("parallel", "arbitrary"), 58),
        name="proj_in",
    )(x, w)


def _rotary(x, cos, sin):
    half = RET_HEAD_DIM // 2
    x1, x2 = x[:, :half], x[:, half:]
    return jnp.concatenate([x1 * cos - x2 * sin, x1 * sin + x2 * cos], axis=-1)


def _retention_tables(L):
    row = lax.broadcasted_iota(jnp.int32, (L, L), 0)
    col = lax.broadcasted_iota(jnp.int32, (L, L), 1)
    diff = (row - col).astype(F32)
    causal = row >= col
    idx = lax.broadcasted_iota(jnp.int32, (L, 1), 0).astype(F32)
    tables = []
    for lg in RET_LOG_GAMMA:
        decay = jnp.where(causal, jnp.exp(jnp.where(causal, diff, 0.0) * lg), 0.0)
        q_decay = jnp.exp((idx + 1.0) * lg)
        k_decay = jnp.exp((L - 1.0 - idx) * lg)
        tables.append((decay, q_decay, k_decay, math.exp(L * lg)))
    return tables


def _retention_head(q, k, v, gate, S, cos, sin, table):
    decay, q_decay, k_decay, s_decay = table
    qr = _rotary(q, cos, sin)
    kr = _rotary(k, cos, sin) * (RET_HEAD_DIM ** -0.5)
    qb = qr.astype(BF16)
    vb = v.astype(BF16)
    scores = lax.dot_general(qb, kr.astype(BF16), (((1,), (1,)), ((), ())), preferred_element_type=F32)
    inner = _dot((scores * decay).astype(BF16), vb)
    cross = _dot(qb, S.astype(BF16)) * q_decay
    o = inner + cross
    kd = (kr * k_decay).astype(BF16)
    S_new = s_decay * S + lax.dot_general(kd, vb, (((0,), (0,)), ((), ())), preferred_element_type=F32)
    mu = jnp.mean(o, axis=-1, keepdims=True)
    c = o - mu
    var = jnp.mean(c * c, axis=-1, keepdims=True)
    y = c * lax.rsqrt(var + LN_EPS) * _silu(gate)
    return y, S_new


def _window_count(w, pos):
    return jnp.minimum(jnp.float32(w), pos + 1.0)


def _mixer_prompt_kernel(z_ref, cos_ref, sin_ref, convw_ref, poolw_ref, pscale_ref,
                         x_ref, g_ref, b_ref, *rest, tl, nl):
    wout_refs = rest[:WOUT_BLOCKS]
    o_ref, convn_ref, pooln_ref, retn_ref, y_ref, s_ref, ubuf_ref, pbuf_ref = rest[WOUT_BLOCKS:]
    l = pl.program_id(1)

    @pl.when(l == 0)
    def _():
        s_ref[...] = jnp.zeros_like(s_ref)
        ubuf_ref[0:CONV_PAD, :] = jnp.zeros((CONV_PAD, CONV_WIDTH), F32)
        pbuf_ref[0:POOL_PAD, :] = jnp.zeros((POOL_PAD, POOL_WIDTH), F32)

    u = z_ref[:, COL_C:COL_C + CONV_WIDTH] * z_ref[:, COL_H:COL_H + CONV_WIDTH]
    ubuf_ref[CONV_PAD:CONV_PAD + tl, :] = u
    cw = convw_ref[...]
    conv = (ubuf_ref[CONV_PAD - 2:CONV_PAD - 2 + tl, :] * cw[0:1, :]
            + ubuf_ref[CONV_PAD - 1:CONV_PAD - 1 + tl, :] * cw[1:2, :]
            + u * cw[2:3, :])
    y_ref[:, 0:CONV_WIDTH] = (z_ref[:, COL_B:COL_B + CONV_WIDTH] * conv).astype(y_ref.dtype)

    pbuf_ref[POOL_PAD:POOL_PAD + tl, :] = z_ref[:, COL_P:COL_P + POOL_WIDTH]
    pos = (l * tl + lax.broadcasted_iota(jnp.int32, (tl, 1), 0)).astype(F32)
    for gi, w in enumerate(POOL_WINDOWS):
        c0 = gi * POOL_GROUP
        tok = pbuf_ref[POOL_PAD:POOL_PAD + tl, c0:c0 + POOL_GROUP]
        win = tok
        for j in range(1, w):
            win = win + pbuf_ref[POOL_PAD - j:POOL_PAD - j + tl, c0:c0 + POOL_GROUP]
        d = win / _window_count(w, pos) - tok
        yb = _dot(d.astype(BF16), poolw_ref[gi]) * pscale_ref[:, c0:c0 + POOL_GROUP]
        y_ref[:, OUT_POOL + c0:OUT_POOL + c0 + POOL_GROUP] = yb.astype(y_ref.dtype)

    tables = _retention_tables(RET_CHUNK)
    for c in range(tl // RET_CHUNK):
        r0 = c * RET_CHUNK
        cos = cos_ref[r0:r0 + RET_CHUNK, :]
        sin = sin_ref[r0:r0 + RET_CHUNK, :]
        for h in range(RET_HEADS):
            h0 = h * RET_HEAD_DIM
            y, s_new = _retention_head(
                z_ref[r0:r0 + RET_CHUNK, COL_Q + h0:COL_Q + h0 + RET_HEAD_DIM],
                z_ref[r0:r0 + RET_CHUNK, COL_K + h0:COL_K + h0 + RET_HEAD_DIM],
                z_ref[r0:r0 + RET_CHUNK, COL_V + h0:COL_V + h0 + RET_HEAD_DIM],
                z_ref[r0:r0 + RET_CHUNK, COL_G + h0:COL_G + h0 + RET_HEAD_DIM],
                s_ref[h], cos, sin, tables[h])
            s_ref[h] = s_new
            y_ref[r0:r0 + RET_CHUNK, OUT_RET + h0:OUT_RET + h0 + RET_HEAD_DIM] = y.astype(y_ref.dtype)
        yc = y_ref[r0:r0 + RET_CHUNK, :]
        m = jnp.concatenate([_dot(yc, w_ref[...]) for w_ref in wout_refs], axis=1)
        o_ref[r0:r0 + RET_CHUNK, :] = _layer_norm(
            DEEPNORM_ALPHA * x_ref[r0:r0 + RET_CHUNK, :] + m, g_ref[...], b_ref[...])

    @pl.when(l == nl - 1)
    def _():
        convn_ref[0] = ubuf_ref[CONV_PAD + tl - (CONV_K - 1):CONV_PAD + tl, :]
        pooln_ref[0] = pbuf_ref[POOL_PAD + tl - POOL_HIST:POOL_PAD + tl, :]
        retn_ref[0] = s_ref[...]

    ubuf_ref[0:CONV_PAD, :] = ubuf_ref[tl:tl + CONV_PAD, :]
    pbuf_ref[0:POOL_PAD, :] = pbuf_ref[tl:tl + POOL_PAD, :]


def _mixer_prompt(z, x1, cos, sin, conv_w, pool_w, pool_scale, w_out, ln_g, ln_b, layer, batch, tl=256):
    T = z.shape[0]
    L = T // batch
    nl = L // tl
    row = lambda b, l: (b * nl + l, 0)
    lay3 = lambda b, l: (layer, 0, 0)
    return pl.pallas_call(
        functools.partial(_mixer_prompt_kernel, tl=tl, nl=nl),
        grid=(batch, nl),
        in_specs=[
            pl.BlockSpec((tl, IN_COLS), row),
            pl.BlockSpec((tl, RET_HEAD_DIM // 2), lambda b, l: (l, 0)),
            pl.BlockSpec((tl, RET_HEAD_DIM // 2), lambda b, l: (l, 0)),
            pl.BlockSpec((None, CONV_K, CONV_WIDTH), lay3),
            pl.BlockSpec((None, len(POOL_WINDOWS), POOL_GROUP, POOL_GROUP), lambda b, l: (layer, 0, 0, 0)),
            pl.BlockSpec((None, 1, POOL_WIDTH), lay3),
            pl.BlockSpec((tl, D_MODEL), row),
            pl.BlockSpec((None, 1, D_MODEL), lay3),
            pl.BlockSpec((None, 1, D_MODEL), lay3),
        ] + [
            pl.BlockSpec((None, D_MODEL, D_MODEL // WOUT_BLOCKS), (lambda j: lambda b, l: (layer, 0, j))(j),
                         pipeline_mode=pl.Buffered(1))
            for j in range(WOUT_BLOCKS)
        ],
        out_specs=[
            pl.BlockSpec((tl, D_MODEL), row),
            pl.BlockSpec((1, CONV_K - 1, CONV_WIDTH), lambda b, l: (b, 0, 0)),
            pl.BlockSpec((1, POOL_HIST, POOL_WIDTH), lambda b, l: (b, 0, 0)),
            pl.BlockSpec((1, RET_HEADS, RET_HEAD_DIM, RET_HEAD_DIM), lambda b, l: (b, 0, 0, 0)),
        ],
        out_shape=[
            jax.ShapeDtypeStruct((T, D_MODEL), F32),
            jax.ShapeDtypeStruct((batch, CONV_K - 1, CONV_WIDTH), F32),
            jax.ShapeDtypeStruct((batch, POOL_HIST, POOL_WIDTH), F32),
            jax.ShapeDtypeStruct((batch, RET_HEADS, RET_HEAD_DIM, RET_HEAD_DIM), F32),
        ],
        scratch_shapes=[
            pltpu.VMEM((tl, D_MODEL), BF16),
            pltpu.VMEM((RET_HEADS, RET_HEAD_DIM, RET_HEAD_DIM), F32),
            pltpu.VMEM((CONV_PAD + tl, CONV_WIDTH), F32),
            pltpu.VMEM((POOL_PAD + tl, POOL_WIDTH), F32),
        ],
        compiler_params=_params(("parallel", "arbitrary"), 48),
        name="mixer_prompt",
    )(z, cos, sin, conv_w, pool_w, pool_scale, x1, ln_g, ln_b, *([w_out] * WOUT_BLOCKS))


def _mixer_sample_kernel(z_ref, cos_ref, sin_ref, convw_ref, poolw_ref, pscale_ref,
                         convc_ref, poolc_ref, state_ref, x_ref, g_ref, b_ref, *rest,
                         nb, ls, start_pos, aliased):
    wout_refs = rest[:WOUT_BLOCKS]
    rest = rest[WOUT_BLOCKS:]
    if aliased:
        rest = rest[1:]
    o_ref, convn_ref, pooln_ref, staten_ref, y_ref, ubuf_ref, pbuf_ref = rest
    u = z_ref[:, :, COL_C:COL_C + CONV_WIDTH] * z_ref[:, :, COL_H:COL_H + CONV_WIDTH]
    ubuf_ref[:, CONV_PAD - (CONV_K - 1):CONV_PAD, :] = convc_ref[...]
    ubuf_ref[:, CONV_PAD:CONV_PAD + ls, :] = u
    cw = convw_ref[...]
    conv = (ubuf_ref[:, CONV_PAD - 2:CONV_PAD - 2 + ls, :] * cw[0:1, :]
            + ubuf_ref[:, CONV_PAD - 1:CONV_PAD - 1 + ls, :] * cw[1:2, :]
            + u * cw[2:3, :])
    y_ref[:, :, 0:CONV_WIDTH] = z_ref[:, :, COL_B:COL_B + CONV_WIDTH] * conv
    convn_ref[...] = ubuf_ref[:, CONV_PAD + ls - (CONV_K - 1):CONV_PAD + ls, :]

    pbuf_ref[:, POOL_PAD - POOL_HIST:POOL_PAD, :] = poolc_ref[...]
    pbuf_ref[:, POOL_PAD:POOL_PAD + ls, :] = z_ref[:, :, COL_P:COL_P + POOL_WIDTH]
    pos = (start_pos + lax.broadcasted_iota(jnp.int32, (1, ls, 1), 1)).astype(F32)
    for gi, w in enumerate(POOL_WINDOWS):
        c0 = gi * POOL_GROUP
        tok = pbuf_ref[:, POOL_PAD:POOL_PAD + ls, c0:c0 + POOL_GROUP]
        win = tok
        for j in range(1, w):
            win = win + pbuf_ref[:, POOL_PAD - j:POOL_PAD - j + ls, c0:c0 + POOL_GROUP]
        d = win / _window_count(w, pos) - tok
        yb = _dot(d.reshape(nb * ls, POOL_GROUP).astype(BF16), poolw_ref[gi]).reshape(nb, ls, POOL_GROUP)
        y_ref[:, :, OUT_POOL + c0:OUT_POOL + c0 + POOL_GROUP] = yb * pscale_ref[:, c0:c0 + POOL_GROUP]
    pooln_ref[...] = pbuf_ref[:, POOL_PAD + ls - POOL_HIST:POOL_PAD + ls, :]

    tables = _retention_tables(ls)
    cos = cos_ref[...]
    sin = sin_ref[...]

    def per_sequence(b, carry):
        for h in range(RET_HEADS):
            h0 = h * RET_HEAD_DIM
            y, s_new = _retention_head(
                z_ref[b, :, COL_Q + h0:COL_Q + h0 + RET_HEAD_DIM],
                z_ref[b, :, COL_K + h0:COL_K + h0 + RET_HEAD_DIM],
                z_ref[b, :, COL_V + h0:COL_V + h0 + RET_HEAD_DIM],
                z_ref[b, :, COL_G + h0:COL_G + h0 + RET_HEAD_DIM],
                state_ref[b, h], cos, sin, tables[h])
            staten_ref[b, h] = s_new
            y_ref[b, :, OUT_RET + h0:OUT_RET + h0 + RET_HEAD_DIM] = y
        return carry

    lax.fori_loop(0, nb, per_sequence, 0, unroll=2)

    yv = y_ref[...].reshape(nb * ls, D_MODEL).astype(BF16)
    m = jnp.concatenate([_dot(yv, w_ref[...]) for w_ref in wout_refs], axis=1)
    x2 = _layer_norm(DEEPNORM_ALPHA * x_ref[...].reshape(nb * ls, D_MODEL) + m, g_ref[...], b_ref[...])
    o_ref[...] = x2.reshape(nb, ls, D_MODEL)


def _mixer_sample(z, x1, cos, sin, conv_w, pool_w, pool_scale, conv_cache, pool_cache, state, state_out,
                  w_out, ln_g, ln_b, layer, start_pos, nb=8):
    B, ls, _ = z.shape
    seq3 = lambda i: (i, 0, 0)
    lay3 = lambda i: (layer, 0, 0)
    lseq4 = lambda i: (layer, i, 0, 0)
    lseq5 = lambda i: (layer, i, 0, 0, 0)
    in_specs = [
        pl.BlockSpec((nb, ls, IN_COLS), seq3),
        pl.BlockSpec((ls, RET_HEAD_DIM // 2), lambda i: (0, 0)),
        pl.BlockSpec((ls, RET_HEAD_DIM // 2), lambda i: (0, 0)),
        pl.BlockSpec((None, CONV_K, CONV_WIDTH), lay3),
        pl.BlockSpec((None, len(POOL_WINDOWS), POOL_GROUP, POOL_GROUP), lambda i: (layer, 0, 0, 0)),
        pl.BlockSpec((None, 1, POOL_WIDTH), lay3),
        pl.BlockSpec((None, nb, CONV_K - 1, CONV_WIDTH), lseq4),
        pl.BlockSpec((None, nb, POOL_HIST, POOL_WIDTH), lseq4),
        pl.BlockSpec((None, nb, RET_HEADS, RET_HEAD_DIM, RET_HEAD_DIM), lseq5),
        pl.BlockSpec((nb, ls, D_MODEL), seq3),
        pl.BlockSpec((None, 1, D_MODEL), lay3),
        pl.BlockSpec((None, 1, D_MODEL), lay3),
    ] + [
        pl.BlockSpec((None, D_MODEL, D_MODEL // WOUT_BLOCKS), (lambda j: lambda i: (layer, 0, j))(j),
                     pipeline_mode=pl.Buffered(1))
        for j in range(WOUT_BLOCKS)
    ]
    args = [z, cos, sin, conv_w, pool_w, pool_scale, conv_cache, pool_cache, state, x1, ln_g, ln_b,
            *([w_out] * WOUT_BLOCKS)]
    aliases = {}
    if state_out is not None:
        in_specs.append(pl.BlockSpec(memory_space=pl.ANY))
        args.append(state_out)
        aliases = {len(args) - 1: 3}
    return pl.pallas_call(
        functools.partial(_mixer_sample_kernel, nb=nb, ls=ls, start_pos=start_pos,
                          aliased=state_out is not None),
        grid=(B // nb,),
        in_specs=in_specs,
        out_specs=[
            pl.BlockSpec((nb, ls, D_MODEL), seq3),
            pl.BlockSpec((nb, CONV_K - 1, CONV_WIDTH), seq3),
            pl.BlockSpec((nb, POOL_HIST, POOL_WIDTH), seq3),
            pl.BlockSpec((None, nb, RET_HEADS, RET_HEAD_DIM, RET_HEAD_DIM), lseq5),
        ],
        out_shape=[
            jax.ShapeDtypeStruct((B, ls, D_MODEL), F32),
            jax.ShapeDtypeStruct((B, CONV_K - 1, CONV_WIDTH), F32),
            jax.ShapeDtypeStruct((B, POOL_HIST, POOL_WIDTH), F32),
            jax.ShapeDtypeStruct(state.shape, F32),
        ],
        scratch_shapes=[
            pltpu.VMEM((nb, ls, D_MODEL), F32),
            pltpu.VMEM((nb, CONV_PAD + ls, CONV_WIDTH), F32),
            pltpu.VMEM((nb, POOL_PAD + ls, POOL_WIDTH), F32),
        ],
        input_output_aliases=aliases,
        compiler_params=_params(("parallel",), 48),
        name="mixer_sample",
    )(*args)


def _rope_tables(start_pos, length):
    half = RET_HEAD_DIM // 2
    inv = ROPE_BASE ** (-jnp.arange(half, dtype=F32) / half)
    pos = start_pos + jnp.arange(length, dtype=F32)
    ang = pos[:, None] * inv[None, :]
    return jnp.cos(ang), jnp.sin(ang)


def _trunk(x, p, caches, start_pos, w, packed):
    B, L, D = x.shape
    T = B * L
    x = x.reshape(T, D)
    p = p.reshape(DEPTH, T, -1)
    cos, sin = _rope_tables(start_pos, L)
    convs, pools, rets = [], [], []
    state_out = None
    collect = not packed

    def ffn(x, name, ln, layer, **kw):
        if collect:
            *outs, wgu, wd = _ffn_ln(x, (w[name + "_w_gate"], w[name + "_w_up"], w[name + "_w_down"]),
                                     w[ln + "_g"], w[ln + "_b"], layer, **kw)
            packed[name, layer] = (wgu, wd)
            return outs
        return _ffn_ln(x, packed[name, layer], w[ln + "_g"], w[ln + "_b"], layer, **kw)

    for i in range(DEPTH):
        x1, = ffn(x, "ffn1", "ln1", i)
        z = _proj_in(x1, w["w_in"], i)
        if caches is None:
            x2, c_new, p_new, r_new = _mixer_prompt(
                z, x1, cos, sin, w["conv_w"], w["pool_w"], w["pool_scale"], w["w_out"], w["ln2_g"], w["ln2_b"],
                i, B)
            rets.append(r_new)
        else:
            x2, c_new, p_new, state_out = _mixer_sample(
                z.reshape(B, L, IN_COLS), x1.reshape(B, L, D), cos, sin, w["conv_w"], w["pool_w"],
                w["pool_scale"], caches[0], caches[1], caches[2], state_out,
                w["w_out"], w["ln2_g"], w["ln2_b"], i, start_pos)
            x2 = x2.reshape(T, D)
        x, = ffn(x2, "ffn2", "ln3", i, ple=(w["ple_gate"], p, w["ple_proj"]))
        convs.append(c_new)
        pools.append(p_new)
    ret = jnp.stack(rets) if caches is None else state_out
    return x.reshape(B, L, D), jnp.stack(convs), jnp.stack(pools), ret


def kernel(x_prompt, x_sample, p_prompt, p_sample, cache_conv, cache_pool, state_ret, ln1_g, ln1_b, ffn1_w_gate, ffn1_w_up, ffn1_w_down, w_in, conv_w, pool_w, pool_scale, w_out, ln2_g, ln2_b, ffn2_w_gate, ffn2_w_up, ffn2_w_down, ple_gate, ple_proj, ln3_g, ln3_b):
    precast = dict(w_in=w_in, pool_w=pool_w, w_out=w_out, ple_gate=ple_gate, ple_proj=ple_proj)
    row_params = dict(ln1_g=ln1_g, ln1_b=ln1_b, ln2_g=ln2_g, ln2_b=ln2_b, ln3_g=ln3_g, ln3_b=ln3_b,
                      pool_scale=pool_scale)
    w = {k: _to_bf16(v) for k, v in precast.items()}
    w.update({k: v.reshape(DEPTH, 1, -1) for k, v in row_params.items()})
    w.update(conv_w=conv_w, ffn1_w_gate=ffn1_w_gate, ffn1_w_up=ffn1_w_up, ffn1_w_down=ffn1_w_down,
             ffn2_w_gate=ffn2_w_gate, ffn2_w_up=ffn2_w_up, ffn2_w_down=ffn2_w_down)
    packed = {}
    y_sample, conv_s, pool_s, ret_s = _trunk(
        x_sample, p_sample, (cache_conv, cache_pool, state_ret), PAST_LEN, w, packed)
    y_prompt, conv_p, pool_p, ret_p = _trunk(x_prompt, p_prompt, None, 0, w, packed)
    return (y_prompt, y_sample, conv_p, pool_p, ret_p, conv_s, pool_s, ret_s)
```

```python
import functools
import math

import jax
import jax.numpy as jnp
from jax import lax
from jax.experimental import pallas as pl
from jax.experimental.pallas import tpu as pltpu

D_MODEL = 2048
DEPTH = 2
PAST_LEN = 16384
CONV_WIDTH = D_MODEL // 4
POOL_WIDTH = D_MODEL // 4
RET_WIDTH = D_MODEL // 2
CONV_K = 3
POOL_WINDOWS = (2, 4, 8, 16)
POOL_GROUP = POOL_WIDTH // len(POOL_WINDOWS)
POOL_HIST = max(POOL_WINDOWS) - 1
RET_HEADS = 4
RET_HEAD_DIM = RET_WIDTH // RET_HEADS
RET_CHUNK = 128
RET_LOG_GAMMA = tuple(math.log(1.0 - 2.0 ** (-5 - h)) for h in range(RET_HEADS))
ROPE_BASE = 10000.0
IN_COLS = 3 * CONV_WIDTH + POOL_WIDTH + 4 * RET_WIDTH
DEEPNORM_ALPHA = (2 * DEPTH) ** 0.25
LN_EPS = 1e-5

COL_B = 0
COL_C = CONV_WIDTH
COL_H = 2 * CONV_WIDTH
COL_P = 3 * CONV_WIDTH
COL_Q = COL_P + POOL_WIDTH
COL_K = COL_Q + RET_WIDTH
COL_V = COL_K + RET_WIDTH
COL_G = COL_V + RET_WIDTH
OUT_POOL = CONV_WIDTH
OUT_RET = CONV_WIDTH + POOL_WIDTH

SUBLANES = 8
CONV_PAD = SUBLANES
POOL_PAD = 16
WOUT_BLOCKS = 4

F32 = jnp.float32
BF16 = jnp.bfloat16
MIB = 1024 * 1024


def _params(semantics, vmem_mib):
    return pltpu.CompilerParams(dimension_semantics=semantics, vmem_limit_bytes=vmem_mib * MIB)


def _layer_norm(r, g, b):
    mu = jnp.mean(r, axis=-1, keepdims=True)
    c = r - mu
    var = jnp.mean(c * c, axis=-1, keepdims=True)
    return c * lax.rsqrt(var + LN_EPS) * g + b


def _silu(x):
    return x * jax.nn.sigmoid(x)


def _dot(a, b):
    return jnp.dot(a, b, preferred_element_type=F32)


def _cast_kernel(x_ref, o_ref):
    o_ref[...] = x_ref[...].astype(o_ref.dtype)


def _to_bf16(w, block_bytes=6 * MIB):
    shape = w.shape
    w = w.reshape(shape[0], -1, shape[-1])
    _, R, C = w.shape
    tr = R
    while tr * C * 4 > block_bytes and tr % 2 == 0 and (tr // 2) % 16 == 0:
        tr //= 2
    out = pl.pallas_call(
        _cast_kernel,
        grid=(shape[0], R // tr),
        in_specs=[pl.BlockSpec((None, tr, C), lambda d, r: (d, r, 0))],
        out_specs=pl.BlockSpec((None, tr, C), lambda d, r: (d, r, 0)),
        out_shape=jax.ShapeDtypeStruct(w.shape, BF16),
        compiler_params=_params(("parallel", "parallel"), 40),
        name="to_bf16",
    )(w)
    return out.reshape(shape)


FFN_PACK = 256
FFN_ROWS = 512


def _ffn_ln_kernel(*refs, nf, tf, packed, ple_cols):
    refs = list(refs)
    x_ref = refs.pop(0)
    if packed:
        wgu_ref, wd_ref = refs[:2]
        refs = refs[2:]
    else:
        wg_ref, wu_ref, wd_ref = refs[:3]
        refs = refs[3:]
    g_ref, b_ref = refs[:2]
    refs = refs[2:]
    if ple_cols:
        gate_ref, p_ref, proj_ref = refs[:3]
        refs = refs[3:]
    o_ref = refs.pop(0)
    if not packed:
        wgu_out_ref, wd_out_ref = refs[:2]
        refs = refs[2:]
        if ple_cols:
            gate_out_ref, proj_out_ref = refs[:2]
            refs = refs[2:]
    xb_ref = refs.pop(0)
    f = pl.program_id(1)
    n_ple = o_ref.shape[1] // ple_cols if ple_cols else 0

    @pl.when(f == 0)
    def _():
        x = x_ref[...]
        xb_ref[...] = x.astype(BF16)
        o_ref[...] = DEEPNORM_ALPHA * x

    @pl.when(f < nf)
    def _():
        if packed:
            wd = wd_ref[...]
        else:
            wg = wg_ref[...].astype(BF16)
            wu = wu_ref[...].astype(BF16)
            wd = wd_ref[...].astype(BF16)
            wgu_out_ref[...] = jnp.concatenate([wg, wu], axis=1)
            wd_out_ref[...] = wd
        for r0 in range(0, o_ref.shape[0], FFN_ROWS):
            xb = xb_ref[r0:r0 + FFN_ROWS, :]
            if packed:
                hs = []
                for s in range(tf // FFN_PACK):
                    c0 = 2 * FFN_PACK * s
                    gu = _dot(xb, wgu_ref[:, c0:c0 + 2 * FFN_PACK])
                    hs.append((_silu(gu[:, :FFN_PACK]) * gu[:, FFN_PACK:]).astype(BF16))
                h = jnp.concatenate(hs, axis=1)
            else:
                h = (_silu(_dot(xb, wg)) * _dot(xb, wu)).astype(BF16)
            o_ref[r0:r0 + FFN_ROWS, :] += 0.5 * _dot(h, wd)

    for j in range(n_ple):
        @pl.when(f == nf + j)
        def _():
            gate_w = gate_ref[...]
            proj_w = proj_ref[...]
            if not packed:
                gate_w = gate_w.astype(BF16)
                proj_w = proj_w.astype(BF16)
                gate_out_ref[...] = gate_w
                proj_out_ref[...] = proj_w
            gate = jax.nn.sigmoid(_dot(xb_ref[...], gate_w))
            o_ref[:, j * ple_cols:(j + 1) * ple_cols] += gate * _dot(p_ref[...].astype(BF16), proj_w)

    @pl.when(f == nf + n_ple - 1)
    def _():
        o_ref[...] = _layer_norm(o_ref[...], g_ref[...], b_ref[...])


def _ffn_ln(x, weights, ln_g, ln_b, layer, ple=None, tm=1024, packed_tf=512, ple_cols=512):
    T, D = x.shape
    packed = len(weights) == 2
    tf = packed_tf if packed else FFN_PACK
    F = weights[1].shape[0] if packed else weights[0].shape[2]
    nf = F // tf
    n_ple = D // ple_cols if ple is not None else 0
    row = lambda i, f: (i, 0)
    const = lambda i, f: (layer, 0, 0)
    fclamp = lambda f: jnp.minimum(f, nf - 1)
    if packed:
        w_specs = [pl.BlockSpec((D, 2 * tf), lambda i, f: (0, fclamp(f))),
                   pl.BlockSpec((tf, D), lambda i, f: (fclamp(f), 0))]
        single = None
    else:
        assert T == tm
        w_specs = [pl.BlockSpec((None, D, tf), lambda i, f: (layer, 0, fclamp(f))),
                   pl.BlockSpec((None, D, tf), lambda i, f: (layer, 0, fclamp(f))),
                   pl.BlockSpec((None, tf, D), lambda i, f: (layer, fclamp(f), 0))]
        single = pl.Buffered(1)
    in_specs = [pl.BlockSpec((tm, D), row, pipeline_mode=single)] + w_specs + [
        pl.BlockSpec((None, 1, D), const),
        pl.BlockSpec((None, 1, D), const),
    ]
    args = [x, *weights, ln_g, ln_b]
    if ple is not None:
        gate_w, p, proj_w = ple
        P = p.shape[2]
        pcol = lambda i, f: (0, jnp.maximum(f - nf, 0))
        lpcol = lambda i, f: (layer, 0, jnp.maximum(f - nf, 0))
        in_specs += [
            pl.BlockSpec((D, ple_cols), pcol) if packed else pl.BlockSpec((None, D, ple_cols), lpcol),
            pl.BlockSpec((None, tm, P), lambda i, f: (layer, i, 0)),
            pl.BlockSpec((P, ple_cols), pcol) if packed else pl.BlockSpec((None, P, ple_cols), lpcol),
        ]
        args += [gate_w, p, proj_w]
    out_shape = [jax.ShapeDtypeStruct((T, D), F32)]
    out_specs = [pl.BlockSpec((tm, D), row, pipeline_mode=single)]
    if not packed:
        out_shape += [jax.ShapeDtypeStruct((D, 2 * F), BF16), jax.ShapeDtypeStruct((F, D), BF16)]
        out_specs += [pl.BlockSpec((D, 2 * tf), lambda i, f: (0, fclamp(f))),
                      pl.BlockSpec((tf, D), lambda i, f: (fclamp(f), 0))]
        if ple is not None:
            out_shape += [jax.ShapeDtypeStruct((D, D), BF16), jax.ShapeDtypeStruct((P, D), BF16)]
            out_specs += [pl.BlockSpec((D, ple_cols), pcol), pl.BlockSpec((P, ple_cols), pcol)]
    return pl.pallas_call(
        functools.partial(_ffn_ln_kernel, nf=nf, tf=tf, packed=packed,
                          ple_cols=ple_cols if ple is not None else 0),
        grid=(T // tm, nf + n_ple),
        in_specs=in_specs,
        out_specs=out_specs,
        out_shape=out_shape,
        scratch_shapes=[pltpu.VMEM((tm, D), BF16)],
        compiler_params=_params(("parallel", "arbitrary"), 62),
        name="ffn_ln",
    )(*args)


def _proj_in_kernel(x_ref, w_ref, o_ref, *rest, emit):
    xb_ref = rest[-1]

    @pl.when(pl.program_id(1) == 0)
    def _():
        xb_ref[...] = x_ref[...].astype(BF16)

    w = w_ref[...]
    if emit:
        w = w.astype(BF16)
        rest[0][...] = w
    o_ref[...] = _dot(xb_ref[...], w)


def _proj_in(x, w, layer, tm=1024, tn=None):
    T, D = x.shape
    emit = w.ndim == 3
    N = w.shape[-1]
    if emit:
        assert T == tm
        tn = tn or 1024
        w_spec = pl.BlockSpec((None, D, tn), lambda i, n: (layer, 0, n))
    else:
        tn = tn or 2048
        w_spec = pl.BlockSpec((D, tn), lambda i, n: (0, n))
    out_shape = [jax.ShapeDtypeStruct((T, N), F32)]
    out_specs = [pl.BlockSpec((tm, tn), lambda i, n: (i, n))]
    if emit:
        out_shape.append(jax.ShapeDtypeStruct((D, N), BF16))
        out_specs.append(pl.BlockSpec((D, tn), lambda i, n: (0, n)))
    return pl.pallas_call(
        functools.partial(_proj_in_kernel, emit=emit),
        grid=(T // tm, N // tn),
        in_specs=[pl.BlockSpec((tm, D), lambda i, n: (i, 0)), w_spec],
        out_specs=out_specs,
        out_shape=out_shape,
        scratch_shapes=[pltpu.VMEM((tm, D), BF16)],
        compiler_params=_params(("parallel", "arbitrary"), 58),
        name="proj_in",
    )(x, w)


def _rotary(x, cos, sin):
    half = RET_HEAD_DIM // 2
    x1, x2 = x[:, :half], x[:, half:]
    return jnp.concatenate([x1 * cos - x2 * sin, x1 * sin + x2 * cos], axis=-1)


def _retention_tables(L):
    row = lax.broadcasted_iota(jnp.int32, (L, L), 0)
    col = lax.broadcasted_iota(jnp.int32, (L, L), 1)
    diff = (row - col).astype(F32)
    causal = row >= col
    idx = lax.broadcasted_iota(jnp.int32, (L, 1), 0).astype(F32)
    tables = []
    for lg in RET_LOG_GAMMA:
        decay = jnp.where(causal, jnp.exp(jnp.where(causal, diff, 0.0) * lg), 0.0)
        q_decay = jnp.exp((idx + 1.0) * lg)
        k_decay = jnp.exp((L - 1.0 - idx) * lg)
        tables.append((decay, q_decay, k_decay, math.exp(L * lg)))
    return tables


def _retention_head(q, k, v, gate, S, cos, sin, table):
    decay, q_decay, k_decay, s_decay = table
    qr = _rotary(q, cos, sin)
    kr = _rotary(k, cos, sin) * (RET_HEAD_DIM ** -0.5)
    qb = qr.astype(BF16)
    vb = v.astype(BF16)
    scores = lax.dot_general(qb, kr.astype(BF16), (((1,), (1,)), ((), ())), preferred_element_type=F32)
    inner = _dot((scores * decay).astype(BF16), vb)
    cross = _dot(qb, S.astype(BF16)) * q_decay
    o = inner + cross
    kd = (kr * k_decay).astype(BF16)
    S_new = s_decay * S + lax.dot_general(kd, vb, (((0,), (0,)), ((), ())), preferred_element_type=F32)
    mu = jnp.mean(o, axis=-1, keepdims=True)
    c = o - mu
    var = jnp.mean(c * c, axis=-1, keepdims=True)
    y = c * lax.rsqrt(var + LN_EPS) * _silu(gate)
    return y, S_new


def _window_count(w, pos):
    return jnp.minimum(jnp.float32(w), pos + 1.0)


def _mixer_prompt_kernel(z_ref, cos_ref, sin_ref, convw_ref, poolw_ref, pscale_ref,
                         x_ref, g_ref, b_ref, *rest, tl, nl):
    wout_refs = rest[:WOUT_BLOCKS]
    o_ref, convn_ref, pooln_ref, retn_ref, y_ref, s_ref, ubuf_ref, pbuf_ref = rest[WOUT_BLOCKS:]
    l = pl.program_id(1)

    @pl.when(l == 0)
    def _():
        s_ref[...] = jnp.zeros_like(s_ref)
        ubuf_ref[0:CONV_PAD, :] = jnp.zeros((CONV_PAD, CONV_WIDTH), F32)
        pbuf_ref[0:POOL_PAD, :] = jnp.zeros((POOL_PAD, POOL_WIDTH), F32)

    u = z_ref[:, COL_C:COL_C + CONV_WIDTH] * z_ref[:, COL_H:COL_H + CONV_WIDTH]
    ubuf_ref[CONV_PAD:CONV_PAD + tl, :] = u
    cw = convw_ref[...]
    conv = (ubuf_ref[CONV_PAD - 2:CONV_PAD - 2 + tl, :] * cw[0:1, :]
            + ubuf_ref[CONV_PAD - 1:CONV_PAD - 1 + tl, :] * cw[1:2, :]
            + u * cw[2:3, :])
    y_ref[:, 0:CONV_WIDTH] = (z_ref[:, COL_B:COL_B + CONV_WIDTH] * conv).astype(y_ref.dtype)

    pbuf_ref[POOL_PAD:POOL_PAD + tl, :] = z_ref[:, COL_P:COL_P + POOL_WIDTH]
    pos = (l * tl + lax.broadcasted_iota(jnp.int32, (tl, 1), 0)).astype(F32)
    for gi, w in enumerate(POOL_WINDOWS):
        c0 = gi * POOL_GROUP
        tok = pbuf_ref[POOL_PAD:POOL_PAD + tl, c0:c0 + POOL_GROUP]
        win = tok
        for j in range(1, w):
            win = win + pbuf_ref[POOL_PAD - j:POOL_PAD - j + tl, c0:c0 + POOL_GROUP]
        d = win / _window_count(w, pos) - tok
        yb = _dot(d.astype(BF16), poolw_ref[gi]) * pscale_ref[:, c0:c0 + POOL_GROUP]
        y_ref[:, OUT_POOL + c0:OUT_POOL + c0 + POOL_GROUP] = yb.astype(y_ref.dtype)

    tables = _retention_tables(RET_CHUNK)
    for c in range(tl // RET_CHUNK):
        r0 = c * RET_CHUNK
        cos = cos_ref[r0:r0 + RET_CHUNK, :]
        sin = sin_ref[r0:r0 + RET_CHUNK, :]
        for h in range(RET_HEADS):
            h0 = h * RET_HEAD_DIM
            y, s_new = _retention_head(
                z_ref[r0:r0 + RET_CHUNK, COL_Q + h0:COL_Q + h0 + RET_HEAD_DIM],
                z_ref[r0:r0 + RET_CHUNK, COL_K + h0:COL_K + h0 + RET_HEAD_DIM],
                z_ref[r0:r0 + RET_CHUNK, COL_V + h0:COL_V + h0 + RET_HEAD_DIM],
                z_ref[r0:r0 + RET_CHUNK, COL_G + h0:COL_G + h0 + RET_HEAD_DIM],
                s_ref[h], cos, sin, tables[h])
            s_ref[h] = s_new
            y_ref[r0:r0 + RET_CHUNK, OUT_RET + h0:OUT_RET + h0 + RET_HEAD_DIM] = y.astype(y_ref.dtype)
        yc = y_ref[r0:r0 + RET_CHUNK, :]
        m = jnp.concatenate([_dot(yc, w_ref[...]) for w_ref in wout_refs], axis=1)
        o_ref[r0:r0 + RET_CHUNK, :] = _layer_norm(
            DEEPNORM_ALPHA * x_ref[r0:r0 + RET_CHUNK, :] + m, g_ref[...], b_ref[...])

    @pl.when(l == nl - 1)
    def _():
        convn_ref[0] = ubuf_ref[CONV_PAD + tl - (CONV_K - 1):CONV_PAD + tl, :]
        pooln_ref[0] = pbuf_ref[POOL_PAD + tl - POOL_HIST:POOL_PAD + tl, :]
        retn_ref[0] = s_ref[...]

    ubuf_ref[0:CONV_PAD, :] = ubuf_ref[tl:tl + CONV_PAD, :]
    pbuf_ref[0:POOL_PAD, :] = pbuf_ref[tl:tl + POOL_PAD, :]


def _mixer_prompt(z, x1, cos, sin, conv_w, pool_w, pool_scale, w_out, ln_g, ln_b, layer, batch, tl=256):
    T = z.shape[0]
    L = T // batch
    nl = L // tl
    row = lambda b, l: (b * nl + l, 0)
    lay3 = lambda b, l: (layer, 0, 0)
    return pl.pallas_call(
        functools.partial(_mixer_prompt_kernel, tl=tl, nl=nl),
        grid=(batch, nl),
        in_specs=[
            pl.BlockSpec((tl, IN_COLS), row),
            pl.BlockSpec((tl, RET_HEAD_DIM // 2), lambda b, l: (l, 0)),
            pl.BlockSpec((tl, RET_HEAD_DIM // 2), lambda b, l: (l, 0)),
            pl.BlockSpec((None, CONV_K, CONV_WIDTH), lay3),
            pl.BlockSpec((None, len(POOL_WINDOWS), POOL_GROUP, POOL_GROUP), lambda b, l: (layer, 0, 0, 0)),
            pl.BlockSpec((None, 1, POOL_WIDTH), lay3),
            pl.BlockSpec((tl, D_MODEL), row),
            pl.BlockSpec((None, 1, D_MODEL), lay3),
            pl.BlockSpec((None, 1, D_MODEL), lay3),
        ] + [
            pl.BlockSpec((None, D_MODEL, D_MODEL // WOUT_BLOCKS), (lambda j: lambda b, l: (layer, 0, j))(j),
                         pipeline_mode=pl.Buffered(1))
            for j in range(WOUT_BLOCKS)
        ],
        out_specs=[
            pl.BlockSpec((tl, D_MODEL), row),
            pl.BlockSpec((1, CONV_K - 1, CONV_WIDTH), lambda b, l: (b, 0, 0)),
            pl.BlockSpec((1, POOL_HIST, POOL_WIDTH), lambda b, l: (b, 0, 0)),
            pl.BlockSpec((1, RET_HEADS, RET_HEAD_DIM, RET_HEAD_DIM), lambda b, l: (b, 0, 0, 0)),
        ],
        out_shape=[
            jax.ShapeDtypeStruct((T, D_MODEL), F32),
            jax.ShapeDtypeStruct((batch, CONV_K - 1, CONV_WIDTH), F32),
            jax.ShapeDtypeStruct((batch, POOL_HIST, POOL_WIDTH), F32),
            jax.ShapeDtypeStruct((batch, RET_HEADS, RET_HEAD_DIM, RET_HEAD_DIM), F32),
        ],
        scratch_shapes=[
            pltpu.VMEM((tl, D_MODEL), BF16),
            pltpu.VMEM((RET_HEADS, RET_HEAD_DIM, RET_HEAD_DIM), F32),
            pltpu.VMEM((CONV_PAD + tl, CONV_WIDTH), F32),
            pltpu.VMEM((POOL_PAD + tl, POOL_WIDTH), F32),
        ],
        compiler_params=_params(("parallel", "arbitrary"), 48),
        name="mixer_prompt",
    )(z, cos, sin, conv_w, pool_w, pool_scale, x1, ln_g, ln_b, *([w_out] * WOUT_BLOCKS))


def _mixer_sample_kernel(z_ref, cos_ref, sin_ref, convw_ref, poolw_ref, pscale_ref,
                         convc_ref, poolc_ref, state_ref, x_ref, g_ref, b_ref, *rest,
                         nb, ls, start_pos, aliased):
    wout_refs = rest[:WOUT_BLOCKS]
    rest = rest[WOUT_BLOCKS:]
    if aliased:
        rest = rest[1:]
    o_ref, convn_ref, pooln_ref, staten_ref, y_ref, ubuf_ref, pbuf_ref = rest
    u = z_ref[:, :, COL_C:COL_C + CONV_WIDTH] * z_ref[:, :, COL_H:COL_H + CONV_WIDTH]
    ubuf_ref[:, CONV_PAD - (CONV_K - 1):CONV_PAD, :] = convc_ref[...]
    ubuf_ref[:, CONV_PAD:CONV_PAD + ls, :] = u
    cw = convw_ref[...]
    conv = (ubuf_ref[:, CONV_PAD - 2:CONV_PAD - 2 + ls, :] * cw[0:1, :]
            + ubuf_ref[:, CONV_PAD - 1:CONV_PAD - 1 + ls, :] * cw[1:2, :]
            + u * cw[2:3, :])
    y_ref[:, :, 0:CONV_WIDTH] = z_ref[:, :, COL_B:COL_B + CONV_WIDTH] * conv
    convn_ref[...] = ubuf_ref[:, CONV_PAD + ls - (CONV_K - 1):CONV_PAD + ls, :]

    pbuf_ref[:, POOL_PAD - POOL_HIST:POOL_PAD, :] = poolc_ref[...]
    pbuf_ref[:, POOL_PAD:POOL_PAD + ls, :] = z_ref[:, :, COL_P:COL_P + POOL_WIDTH]
    pos = (start_pos + lax.broadcasted_iota(jnp.int32, (1, ls, 1), 1)).astype(F32)
    for gi, w in enumerate(POOL_WINDOWS):
        c0 = gi * POOL_GROUP
        tok = pbuf_ref[:, POOL_PAD:POOL_PAD + ls, c0:c0 + POOL_GROUP]
        win = tok
        for j in range(1, w):
            win = win + pbuf_ref[:, POOL_PAD - j:POOL_PAD - j + ls, c0:c0 + POOL_GROUP]
        d = win / _window_count(w, pos) - tok
        yb = _dot(d.reshape(nb * ls, POOL_GROUP).astype(BF16), poolw_ref[gi]).reshape(nb, ls, POOL_GROUP)
        y_ref[:, :, OUT_POOL + c0:OUT_POOL + c0 + POOL_GROUP] = yb * pscale_ref[:, c0:c0 + POOL_GROUP]
    pooln_ref[...] = pbuf_ref[:, POOL_PAD + ls - POOL_HIST:POOL_PAD + ls, :]

    tables = _retention_tables(ls)
    cos = cos_ref[...]
    sin = sin_ref[...]

    def per_sequence(b, carry):
        for h in range(RET_HEADS):
            h0 = h * RET_HEAD_DIM
            y, s_new = _retention_head(
                z_ref[b, :, COL_Q + h0:COL_Q + h0 + RET_HEAD_DIM],
                z_ref[b, :, COL_K + h0:COL_K + h0 + RET_HEAD_DIM],
                z_ref[b, :, COL_V + h0:COL_V + h0 + RET_HEAD_DIM],
                z_ref[b, :, COL_G + h0:COL_G + h0 + RET_HEAD_DIM],
                state_ref[b, h], cos, sin, tables[h])
            staten_ref[b, h] = s_new
            y_ref[b, :, OUT_RET + h0:OUT_RET + h0 + RET_HEAD_DIM] = y
        return carry

    lax.fori_loop(0, nb, per_sequence, 0, unroll=4)

    yv = y_ref[...].reshape(nb * ls, D_MODEL).astype(BF16)
    m = jnp.concatenate([_dot(yv, w_ref[...]) for w_ref in wout_refs], axis=1)
    x2 = _layer_norm(DEEPNORM_ALPHA * x_ref[...].reshape(nb * ls, D_MODEL) + m, g_ref[...], b_ref[...])
    o_ref[...] = x2.reshape(nb, ls, D_MODEL)


def _mixer_sample(z, x1, cos, sin, conv_w, pool_w, pool_scale, conv_cache, pool_cache, state, state_out,
                  w_out, ln_g, ln_b, layer, start_pos, nb=8):
    B, ls, _ = z.shape
    seq3 = lambda i: (i, 0, 0)
    lay3 = lambda i: (layer, 0, 0)
    lseq4 = lambda i: (layer, i, 0, 0)
    lseq5 = lambda i: (layer, i, 0, 0, 0)
    in_specs = [
        pl.BlockSpec((nb, ls, IN_COLS), seq3),
        pl.BlockSpec((ls, RET_HEAD_DIM // 2), lambda i: (0, 0)),
        pl.BlockSpec((ls, RET_HEAD_DIM // 2), lambda i: (0, 0)),
        pl.BlockSpec((None, CONV_K, CONV_WIDTH), lay3),
        pl.BlockSpec((None, len(POOL_WINDOWS), POOL_GROUP, POOL_GROUP), lambda i: (layer, 0, 0, 0)),
        pl.BlockSpec((None, 1, POOL_WIDTH), lay3),
        pl.BlockSpec((None, nb, CONV_K - 1, CONV_WIDTH), lseq4),
        pl.BlockSpec((None, nb, POOL_HIST, POOL_WIDTH), lseq4),
        pl.BlockSpec((None, nb, RET_HEADS, RET_HEAD_DIM, RET_HEAD_DIM), lseq5),
        pl.BlockSpec((nb, ls, D_MODEL), seq3),
        pl.BlockSpec((None, 1, D_MODEL), lay3),
        pl.BlockSpec((None, 1, D_MODEL), lay3),
    ] + [
        pl.BlockSpec((None, D_MODEL, D_MODEL // WOUT_BLOCKS), (lambda j: lambda i: (layer, 0, j))(j),
                     pipeline_mode=pl.Buffered(1))
        for j in range(WOUT_BLOCKS)
    ]
    args = [z, cos, sin, conv_w, pool_w, pool_scale, conv_cache, pool_cache, state, x1, ln_g, ln_b,
            *([w_out] * WOUT_BLOCKS)]
    aliases = {}
    if state_out is not None:
        in_specs.append(pl.BlockSpec(memory_space=pl.ANY))
        args.append(state_out)
        aliases = {len(args) - 1: 3}
    return pl.pallas_call(
        functools.partial(_mixer_sample_kernel, nb=nb, ls=ls, start_pos=start_pos,
                          aliased=state_out is not None),
        grid=(B // nb,),
        in_specs=in_specs,
        out_specs=[
            pl.BlockSpec((nb, ls, D_MODEL), seq3),
            pl.BlockSpec((nb, CONV_K - 1, CONV_WIDTH), seq3),
            pl.BlockSpec((nb, POOL_HIST, POOL_WIDTH), seq3),
            pl.BlockSpec((None, nb, RET_HEADS, RET_HEAD_DIM, RET_HEAD_DIM), lseq5),
        ],
        out_shape=[
            jax.ShapeDtypeStruct((B, ls, D_MODEL), F32),
            jax.ShapeDtypeStruct((B, CONV_K - 1, CONV_WIDTH), F32),
            jax.ShapeDtypeStruct((B, POOL_HIST, POOL_WIDTH), F32),
            jax.ShapeDtypeStruct(state.shape, F32),
        ],
        scratch_shapes=[
            pltpu.VMEM((nb, ls, D_MODEL), F32),
            pltpu.VMEM((nb, CONV_PAD + ls, CONV_WIDTH), F32),
            pltpu.VMEM((nb, POOL_PAD + ls, POOL_WIDTH), F32),
        ],
        input_output_aliases=aliases,
        compiler_params=_params(("parallel",), 48),
        name="mixer_sample",
    )(*args)


def _rope_tables(start_pos, length):
    half = RET_HEAD_DIM // 2
    inv = ROPE_BASE ** (-jnp.arange(half, dtype=F32) / half)
    pos = start_pos + jnp.arange(length, dtype=F32)
    ang = pos[:, None] * inv[None, :]
    return jnp.cos(ang), jnp.sin(ang)


def _trunk(x, p, caches, start_pos, w, packed):
    B, L, D = x.shape
    T = B * L
    x = x.reshape(T, D)
    p = p.reshape(DEPTH, T, -1)
    cos, sin = _rope_tables(start_pos, L)
    convs, pools, rets = [], [], []
    state_out = None
    collect = not packed

    def ffn(x, name, ln, layer, with_ple=False):
        ln_g, ln_b = w[ln + "_g"], w[ln + "_b"]
        if collect:
            ple = (w["ple_gate"], p, w["ple_proj"]) if with_ple else None
            out, *copies = _ffn_ln(x, (w[name + "_w_gate"], w[name + "_w_up"], w[name + "_w_down"]),
                                   ln_g, ln_b, layer, ple=ple)
            packed[name, layer] = tuple(copies[:2])
            if with_ple:
                packed["ple", layer] = tuple(copies[2:])
            return out
        ple = (packed["ple", layer][0], p, packed["ple", layer][1]) if with_ple else None
        return _ffn_ln(x, packed[name, layer], ln_g, ln_b, layer, ple=ple)[0]

    for i in range(DEPTH):
        x1 = ffn(x, "ffn1", "ln1", i)
        if collect:
            z, packed["w_in", i] = _proj_in(x1, w["w_in"], i)
        else:
            z, = _proj_in(x1, packed["w_in", i], i)
        if caches is None:
            x2, c_new, p_new, r_new = _mixer_prompt(
                z, x1, cos, sin, w["conv_w"], w["pool_w"], w["pool_scale"], w["w_out"], w["ln2_g"], w["ln2_b"],
                i, B)
            rets.append(r_new)
        else:
            x2, c_new, p_new, state_out = _mixer_sample(
                z.reshape(B, L, IN_COLS), x1.reshape(B, L, D), cos, sin, w["conv_w"], w["pool_w"],
                w["pool_scale"], caches[0], caches[1], caches[2], state_out,
                w["w_out"], w["ln2_g"], w["ln2_b"], i, start_pos)
            x2 = x2.reshape(T, D)
        x = ffn(x2, "ffn2", "ln3", i, with_ple=True)
        convs.append(c_new)
        pools.append(p_new)
    ret = jnp.stack(rets) if caches is None else state_out
    return x.reshape(B, L, D), jnp.stack(convs), jnp.stack(pools), ret


def kernel(x_prompt, x_sample, p_prompt, p_sample, cache_conv, cache_pool, state_ret, ln1_g, ln1_b, ffn1_w_gate, ffn1_w_up, ffn1_w_down, w_in, conv_w, pool_w, pool_scale, w_out, ln2_g, ln2_b, ffn2_w_gate, ffn2_w_up, ffn2_w_down, ple_gate, ple_proj, ln3_g, ln3_b):
    precast = dict(pool_w=pool_w, w_out=w_out)
    row_params = dict(ln1_g=ln1_g, ln1_b=ln1_b, ln2_g=ln2_g, ln2_b=ln2_b, ln3_g=ln3_g, ln3_b=ln3_b,
                      pool_scale=pool_scale)
    w = {k: _to_bf16(v) for k, v in precast.items()}
    w.update({k: v.reshape(DEPTH, 1, -1) for k, v in row_params.items()})
    w.update(conv_w=conv_w, ffn1_w_gate=ffn1_w_gate, ffn1_w_up=ffn1_w_up, ffn1_w_down=ffn1_w_down,
             ffn2_w_gate=ffn2_w_gate, ffn2_w_up=ffn2_w_up, ffn2_w_down=ffn2_w_down,
             w_in=w_in, ple_gate=ple_gate, ple_proj=ple_proj)
    packed = {}
    y_sample, conv_s, pool_s, ret_s = _trunk(
        x_sample, p_sample, (cache_conv, cache_pool, state_ret), PAST_LEN, w, packed)
    y_prompt, conv_p, pool_p, ret_p = _trunk(x_prompt, p_prompt, None, 0, w, packed)
    return (y_prompt, y_sample, conv_p, pool_p, ret_p, conv_s, pool_s, ret_s)
```

```python
import functools
import math

import jax
import jax.numpy as jnp
from jax import lax
from jax.experimental import pallas as pl
from jax.experimental.pallas import tpu as pltpu

D_MODEL = 2048
DEPTH = 2
PAST_LEN = 16384
CONV_WIDTH = D_MODEL // 4
POOL_WIDTH = D_MODEL // 4
RET_WIDTH = D_MODEL // 2
CONV_K = 3
POOL_WINDOWS = (2, 4, 8, 16)
POOL_GROUP = POOL_WIDTH // len(POOL_WINDOWS)
POOL_HIST = max(POOL_WINDOWS) - 1
RET_HEADS = 4
RET_HEAD_DIM = RET_WIDTH // RET_HEADS
RET_CHUNK = 128
RET_LOG_GAMMA = tuple(math.log(1.0 - 2.0 ** (-5 - h)) for h in range(RET_HEADS))
ROPE_BASE = 10000.0
IN_COLS = 3 * CONV_WIDTH + POOL_WIDTH + 4 * RET_WIDTH
DEEPNORM_ALPHA = (2 * DEPTH) ** 0.25
LN_EPS = 1e-5

COL_B = 0
COL_C = CONV_WIDTH
COL_H = 2 * CONV_WIDTH
COL_P = 3 * CONV_WIDTH
COL_Q = COL_P + POOL_WIDTH
COL_K = COL_Q + RET_WIDTH
COL_V = COL_K + RET_WIDTH
COL_G = COL_V + RET_WIDTH
OUT_POOL = CONV_WIDTH
OUT_RET = CONV_WIDTH + POOL_WIDTH

SUBLANES = 8
CONV_PAD = SUBLANES
POOL_PAD = 16
WOUT_BLOCKS = 4

F32 = jnp.float32
BF16 = jnp.bfloat16
MIB = 1024 * 1024


def _params(semantics, vmem_mib):
    return pltpu.CompilerParams(dimension_semantics=semantics, vmem_limit_bytes=vmem_mib * MIB)


def _layer_norm(r, g, b):
    mu = jnp.mean(r, axis=-1, keepdims=True)
    c = r - mu
    var = jnp.mean(c * c, axis=-1, keepdims=True)
    return c * lax.rsqrt(var + LN_EPS) * g + b


def _silu(x):
    return x * jax.nn.sigmoid(x)


def _dot(a, b):
    return jnp.dot(a, b, preferred_element_type=F32)


def _cast_kernel(x_ref, o_ref):
    o_ref[...] = x_ref[...].astype(o_ref.dtype)


def _to_bf16(w, block_bytes=6 * MIB):
    shape = w.shape
    w = w.reshape(shape[0], -1, shape[-1])
    _, R, C = w.shape
    tr = R
    while tr * C * 4 > block_bytes and tr % 2 == 0 and (tr // 2) % 16 == 0:
        tr //= 2
    out = pl.pallas_call(
        _cast_kernel,
        grid=(shape[0], R // tr),
        in_specs=[pl.BlockSpec((None, tr, C), lambda d, r: (d, r, 0))],
        out_specs=pl.BlockSpec((None, tr, C), lambda d, r: (d, r, 0)),
        out_shape=jax.ShapeDtypeStruct(w.shape, BF16),
        compiler_params=_params(("parallel", "parallel"), 40),
        name="to_bf16",
    )(w)
    return out.reshape(shape)


FFN_PACK = 256
FFN_ROWS = 512


def _ffn_ln_kernel(*refs, nf, tf, packed, ple_cols):
    refs = list(refs)
    x_ref = refs.pop(0)
    if packed:
        wgu_ref, wd_ref = refs[:2]
        refs = refs[2:]
    else:
        wg_ref, wu_ref, wd_ref = refs[:3]
        refs = refs[3:]
    g_ref, b_ref = refs[:2]
    refs = refs[2:]
    if ple_cols:
        gate_ref, p_ref, proj_ref = refs[:3]
        refs = refs[3:]
    o_ref = refs.pop(0)
    if not packed:
        wgu_out_ref, wd_out_ref = refs[:2]
        refs = refs[2:]
        if ple_cols:
            gate_out_ref, proj_out_ref = refs[:2]
            refs = refs[2:]
    xb_ref = refs.pop(0)
    f = pl.program_id(1)
    n_ple = o_ref.shape[1] // ple_cols if ple_cols else 0

    @pl.when(f == 0)
    def _():
        x = x_ref[...]
        xb_ref[...] = x.astype(BF16)
        o_ref[...] = DEEPNORM_ALPHA * x

    @pl.when(f < nf)
    def _():
        if packed:
            wd = wd_ref[...]
        else:
            wg = wg_ref[...].astype(BF16)
            wu = wu_ref[...].astype(BF16)
            wd = wd_ref[...].astype(BF16)
            wgu_out_ref[...] = jnp.concatenate([wg, wu], axis=1)
            wd_out_ref[...] = wd
        for r0 in range(0, o_ref.shape[0], FFN_ROWS):
            xb = xb_ref[r0:r0 + FFN_ROWS, :]
            if packed:
                hs = []
                for s in range(tf // FFN_PACK):
                    c0 = 2 * FFN_PACK * s
                    gu = _dot(xb, wgu_ref[:, c0:c0 + 2 * FFN_PACK])
                    hs.append((_silu(gu[:, :FFN_PACK]) * gu[:, FFN_PACK:]).astype(BF16))
                h = jnp.concatenate(hs, axis=1)
            else:
                h = (_silu(_dot(xb, wg)) * _dot(xb, wu)).astype(BF16)
            o_ref[r0:r0 + FFN_ROWS, :] += 0.5 * _dot(h, wd)

    for j in range(n_ple):
        @pl.when(f == nf + j)
        def _():
            gate_w = gate_ref[...]
            proj_w = proj_ref[...]
            if not packed:
                gate_w = gate_w.astype(BF16)
                proj_w = proj_w.astype(BF16)
                gate_out_ref[...] = gate_w
                proj_out_ref[...] = proj_w
            gate = jax.nn.sigmoid(_dot(xb_ref[...], gate_w))
            o_ref[:, j * ple_cols:(j + 1) * ple_cols] += gate * _dot(p_ref[...].astype(BF16), proj_w)

    @pl.when(f == nf + n_ple - 1)
    def _():
        o_ref[...] = _layer_norm(o_ref[...], g_ref[...], b_ref[...])


def _ffn_ln(x, weights, ln_g, ln_b, layer, ple=None, tm=1024, packed_tf=512, ple_cols=512):
    T, D = x.shape
    packed = len(weights) == 2
    tf = packed_tf if packed else FFN_PACK
    F = weights[1].shape[0] if packed else weights[0].shape[2]
    nf = F // tf
    n_ple = D // ple_cols if ple is not None else 0
    row = lambda i, f: (i, 0)
    const = lambda i, f: (layer, 0, 0)
    fclamp = lambda f: jnp.minimum(f, nf - 1)
    if packed:
        w_specs = [pl.BlockSpec((D, 2 * tf), lambda i, f: (0, fclamp(f))),
                   pl.BlockSpec((tf, D), lambda i, f: (fclamp(f), 0))]
        single = None
    else:
        assert T == tm
        w_specs = [pl.BlockSpec((None, D, tf), lambda i, f: (layer, 0, fclamp(f))),
                   pl.BlockSpec((None, D, tf), lambda i, f: (layer, 0, fclamp(f))),
                   pl.BlockSpec((None, tf, D), lambda i, f: (layer, fclamp(f), 0))]
        single = pl.Buffered(1)
    in_specs = [pl.BlockSpec((tm, D), row, pipeline_mode=single)] + w_specs + [
        pl.BlockSpec((None, 1, D), const),
        pl.BlockSpec((None, 1, D), const),
    ]
    args = [x, *weights, ln_g, ln_b]
    if ple is not None:
        gate_w, p, proj_w = ple
        P = p.shape[2]
        pcol = lambda i, f: (0, jnp.maximum(f - nf, 0))
        lpcol = lambda i, f: (layer, 0, jnp.maximum(f - nf, 0))
        in_specs += [
            pl.BlockSpec((D, ple_cols), pcol) if packed else pl.BlockSpec((None, D, ple_cols), lpcol),
            pl.BlockSpec((None, tm, P), lambda i, f: (layer, i, 0)),
            pl.BlockSpec((P, ple_cols), pcol) if packed else pl.BlockSpec((None, P, ple_cols), lpcol),
        ]
        args += [gate_w, p, proj_w]
    out_shape = [jax.ShapeDtypeStruct((T, D), F32)]
    out_specs = [pl.BlockSpec((tm, D), row, pipeline_mode=single)]
    if not packed:
        out_shape += [jax.ShapeDtypeStruct((D, 2 * F), BF16), jax.ShapeDtypeStruct((F, D), BF16)]
        out_specs += [pl.BlockSpec((D, 2 * tf), lambda i, f: (0, fclamp(f))),
                      pl.BlockSpec((tf, D), lambda i, f: (fclamp(f), 0))]
        if ple is not None:
            out_shape += [jax.ShapeDtypeStruct((D, D), BF16), jax.ShapeDtypeStruct((P, D), BF16)]
            out_specs += [pl.BlockSpec((D, ple_cols), pcol), pl.BlockSpec((P, ple_cols), pcol)]
    return pl.pallas_call(
        functools.partial(_ffn_ln_kernel, nf=nf, tf=tf, packed=packed,
                          ple_cols=ple_cols if ple is not None else 0),
        grid=(T // tm, nf + n_ple),
        in_specs=in_specs,
        out_specs=out_specs,
        out_shape=out_shape,
        scratch_shapes=[pltpu.VMEM((tm, D), BF16)],
        compiler_params=_params(("parallel", "arbitrary"), 62),
        name="ffn_ln",
    )(*args)


def _proj_in_kernel(x_ref, w_ref, o_ref, *rest, emit):
    xb_ref = rest[-1]

    @pl.when(pl.program_id(1) == 0)
    def _():
        xb_ref[...] = x_ref[...].astype(BF16)

    w = w_ref[...]
    if emit:
        w = w.astype(BF16)
        rest[0][...] = w
    o_ref[...] = _dot(xb_ref[...], w)


def _proj_in(x, w, layer, tm=1024, tn=None):
    T, D = x.shape
    emit = w.ndim == 3
    N = w.shape[-1]
    if emit:
        assert T == tm
        tn = tn or 1024
        w_spec = pl.BlockSpec((None, D, tn), lambda i, n: (layer, 0, n))
    else:
        tn = tn or 2048
        w_spec = pl.BlockSpec((D, tn), lambda i, n: (0, n))
    out_shape = [jax.ShapeDtypeStruct((T, N), F32)]
    out_specs = [pl.BlockSpec((tm, tn), lambda i, n: (i, n))]
    if emit:
        out_shape.append(jax.ShapeDtypeStruct((D, N), BF16))
        out_specs.append(pl.BlockSpec((D, tn), lambda i, n: (0, n)))
    return pl.pallas_call(
        functools.partial(_proj_in_kernel, emit=emit),
        grid=(T // tm, N // tn),
        in_specs=[pl.BlockSpec((tm, D), lambda i, n: (i, 0)), w_spec],
        out_specs=out_specs,
        out_shape=out_shape,
        scratch_shapes=[pltpu.VMEM((tm, D), BF16)],
        compiler_params=_params(("parallel", "arbitrary"), 58),
        name="proj_in",
    )(x, w)


def _rotary(x, cos, sin):
    half = RET_HEAD_DIM // 2
    x1, x2 = x[:, :half], x[:, half:]
    return jnp.concatenate([x1 * cos - x2 * sin, x1 * sin + x2 * cos], axis=-1)


def _retention_tables(L):
    row = lax.broadcasted_iota(jnp.int32, (L, L), 0)
    col = lax.broadcasted_iota(jnp.int32, (L, L), 1)
    diff = (row - col).astype(F32)
    causal = row >= col
    idx = lax.broadcasted_iota(jnp.int32, (L, 1), 0).astype(F32)
    tables = []
    for lg in RET_LOG_GAMMA:
        decay = jnp.where(causal, jnp.exp(jnp.where(causal, diff, 0.0) * lg), 0.0)
        q_decay = jnp.exp((idx + 1.0) * lg)
        k_decay = jnp.exp((L - 1.0 - idx) * lg)
        tables.append((decay, q_decay, k_decay, math.exp(L * lg)))
    return tables


def _retention_head(q, k, v, gate, S, cos, sin, table):
    decay, q_decay, k_decay, s_decay = table
    qr = _rotary(q, cos, sin)
    kr = _rotary(k, cos, sin) * (RET_HEAD_DIM ** -0.5)
    qb = qr.astype(BF16)
    vb = v.astype(BF16)
    scores = lax.dot_general(qb, kr.astype(BF16), (((1,), (1,)), ((), ())), preferred_element_type=F32)
    inner = _dot((scores * decay).astype(BF16), vb)
    cross = _dot(qb, S.astype(BF16)) * q_decay
    o = inner + cross
    kd = (kr * k_decay).astype(BF16)
    S_new = s_decay * S + lax.dot_general(kd, vb, (((0,), (0,)), ((), ())), preferred_element_type=F32)
    mu = jnp.mean(o, axis=-1, keepdims=True)
    c = o - mu
    var = jnp.mean(c * c, axis=-1, keepdims=True)
    y = c * lax.rsqrt(var + LN_EPS) * _silu(gate)
    return y, S_new


def _window_count(w, pos):
    return jnp.minimum(jnp.float32(w), pos + 1.0)


def _mixer_prompt_kernel(z_ref, cos_ref, sin_ref, convw_ref, poolw_ref, pscale_ref,
                         x_ref, g_ref, b_ref, *rest, tl, nl):
    wout_refs = rest[:WOUT_BLOCKS]
    o_ref, convn_ref, pooln_ref, retn_ref, y_ref, s_ref, ubuf_ref, pbuf_ref = rest[WOUT_BLOCKS:]
    l = pl.program_id(1)

    @pl.when(l == 0)
    def _():
        s_ref[...] = jnp.zeros_like(s_ref)
        ubuf_ref[0:CONV_PAD, :] = jnp.zeros((CONV_PAD, CONV_WIDTH), F32)
        pbuf_ref[0:POOL_PAD, :] = jnp.zeros((POOL_PAD, POOL_WIDTH), F32)

    u = z_ref[:, COL_C:COL_C + CONV_WIDTH] * z_ref[:, COL_H:COL_H + CONV_WIDTH]
    ubuf_ref[CONV_PAD:CONV_PAD + tl, :] = u
    cw = convw_ref[...]
    conv = (ubuf_ref[CONV_PAD - 2:CONV_PAD - 2 + tl, :] * cw[0:1, :]
            + ubuf_ref[CONV_PAD - 1:CONV_PAD - 1 + tl, :] * cw[1:2, :]
            + u * cw[2:3, :])
    y_ref[:, 0:CONV_WIDTH] = (z_ref[:, COL_B:COL_B + CONV_WIDTH] * conv).astype(y_ref.dtype)

    pbuf_ref[POOL_PAD:POOL_PAD + tl, :] = z_ref[:, COL_P:COL_P + POOL_WIDTH]
    pos = (l * tl + lax.broadcasted_iota(jnp.int32, (tl, 1), 0)).astype(F32)
    for gi, w in enumerate(POOL_WINDOWS):
        c0 = gi * POOL_GROUP
        tok = pbuf_ref[POOL_PAD:POOL_PAD + tl, c0:c0 + POOL_GROUP]
        win = tok
        for j in range(1, w):
            win = win + pbuf_ref[POOL_PAD - j:POOL_PAD - j + tl, c0:c0 + POOL_GROUP]
        d = win / _window_count(w, pos) - tok
        yb = _dot(d.astype(BF16), poolw_ref[gi]) * pscale_ref[:, c0:c0 + POOL_GROUP]
        y_ref[:, OUT_POOL + c0:OUT_POOL + c0 + POOL_GROUP] = yb.astype(y_ref.dtype)

    tables = _retention_tables(RET_CHUNK)
    for c in range(tl // RET_CHUNK):
        r0 = c * RET_CHUNK
        cos = cos_ref[r0:r0 + RET_CHUNK, :]
        sin = sin_ref[r0:r0 + RET_CHUNK, :]
        for h in range(RET_HEADS):
            h0 = h * RET_HEAD_DIM
            y, s_new = _retention_head(
                z_ref[r0:r0 + RET_CHUNK, COL_Q + h0:COL_Q + h0 + RET_HEAD_DIM],
                z_ref[r0:r0 + RET_CHUNK, COL_K + h0:COL_K + h0 + RET_HEAD_DIM],
                z_ref[r0:r0 + RET_CHUNK, COL_V + h0:COL_V + h0 + RET_HEAD_DIM],
                z_ref[r0:r0 + RET_CHUNK, COL_G + h0:COL_G + h0 + RET_HEAD_DIM],
                s_ref[h], cos, sin, tables[h])
            s_ref[h] = s_new
            y_ref[r0:r0 + RET_CHUNK, OUT_RET + h0:OUT_RET + h0 + RET_HEAD_DIM] = y.astype(y_ref.dtype)
        yc = y_ref[r0:r0 + RET_CHUNK, :]
        m = jnp.concatenate([_dot(yc, w_ref[...]) for w_ref in wout_refs], axis=1)
        o_ref[r0:r0 + RET_CHUNK, :] = _layer_norm(
            DEEPNORM_ALPHA * x_ref[r0:r0 + RET_CHUNK, :] + m, g_ref[...], b_ref[...])

    @pl.when(l == nl - 1)
    def _():
        convn_ref[0] = ubuf_ref[CONV_PAD + tl - (CONV_K - 1):CONV_PAD + tl, :]
        pooln_ref[0] = pbuf_ref[POOL_PAD + tl - POOL_HIST:POOL_PAD + tl, :]
        retn_ref[0] = s_ref[...]

    ubuf_ref[0:CONV_PAD, :] = ubuf_ref[tl:tl + CONV_PAD, :]
    pbuf_ref[0:POOL_PAD, :] = pbuf_ref[tl:tl + POOL_PAD, :]


def _mixer_prompt(z, x1, cos, sin, conv_w, pool_w, pool_scale, w_out, ln_g, ln_b, layer, batch, tl=512):
    T = z.shape[0]
    L = T // batch
    nl = L // tl
    row = lambda b, l: (b * nl + l, 0)
    lay3 = lambda b, l: (layer, 0, 0)
    return pl.pallas_call(
        functools.partial(_mixer_prompt_kernel, tl=tl, nl=nl),
        grid=(batch, nl),
        in_specs=[
            pl.BlockSpec((tl, IN_COLS), row),
            pl.BlockSpec((tl, RET_HEAD_DIM // 2), lambda b, l: (l, 0)),
            pl.BlockSpec((tl, RET_HEAD_DIM // 2), lambda b, l: (l, 0)),
            pl.BlockSpec((None, CONV_K, CONV_WIDTH), lay3),
            pl.BlockSpec((None, len(POOL_WINDOWS), POOL_GROUP, POOL_GROUP), lambda b, l: (layer, 0, 0, 0)),
            pl.BlockSpec((None, 1, POOL_WIDTH), lay3),
            pl.BlockSpec((tl, D_MODEL), row),
            pl.BlockSpec((None, 1, D_MODEL), lay3),
            pl.BlockSpec((None, 1, D_MODEL), lay3),
        ] + [
            pl.BlockSpec((None, D_MODEL, D_MODEL // WOUT_BLOCKS), (lambda j: lambda b, l: (layer, 0, j))(j),
                         pipeline_mode=pl.Buffered(1))
            for j in range(WOUT_BLOCKS)
        ],
        out_specs=[
            pl.BlockSpec((tl, D_MODEL), row),
            pl.BlockSpec((1, CONV_K - 1, CONV_WIDTH), lambda b, l: (b, 0, 0)),
            pl.BlockSpec((1, POOL_HIST, POOL_WIDTH), lambda b, l: (b, 0, 0)),
            pl.BlockSpec((1, RET_HEADS, RET_HEAD_DIM, RET_HEAD_DIM), lambda b, l: (b, 0, 0, 0)),
        ],
        out_shape=[
            jax.ShapeDtypeStruct((T, D_MODEL), F32),
            jax.ShapeDtypeStruct((batch, CONV_K - 1, CONV_WIDTH), F32),
            jax.ShapeDtypeStruct((batch, POOL_HIST, POOL_WIDTH), F32),
            jax.ShapeDtypeStruct((batch, RET_HEADS, RET_HEAD_DIM, RET_HEAD_DIM), F32),
        ],
        scratch_shapes=[
            pltpu.VMEM((tl, D_MODEL), BF16),
            pltpu.VMEM((RET_HEADS, RET_HEAD_DIM, RET_HEAD_DIM), F32),
            pltpu.VMEM((CONV_PAD + tl, CONV_WIDTH), F32),
            pltpu.VMEM((POOL_PAD + tl, POOL_WIDTH), F32),
        ],
        compiler_params=_params(("parallel", "arbitrary"), 62),
        name="mixer_prompt",
    )(z, cos, sin, conv_w, pool_w, pool_scale, x1, ln_g, ln_b, *([w_out] * WOUT_BLOCKS))


def _mixer_sample_kernel(z_ref, cos_ref, sin_ref, convw_ref, poolw_ref, pscale_ref,
                         convc_ref, poolc_ref, state_ref, x_ref, g_ref, b_ref, *rest,
                         nb, ls, start_pos, aliased):
    wout_refs = rest[:WOUT_BLOCKS]
    rest = rest[WOUT_BLOCKS:]
    if aliased:
        rest = rest[1:]
    o_ref, convn_ref, pooln_ref, staten_ref, y_ref, ubuf_ref, pbuf_ref = rest
    u = z_ref[:, :, COL_C:COL_C + CONV_WIDTH] * z_ref[:, :, COL_H:COL_H + CONV_WIDTH]
    ubuf_ref[:, CONV_PAD - (CONV_K - 1):CONV_PAD, :] = convc_ref[...]
    ubuf_ref[:, CONV_PAD:CONV_PAD + ls, :] = u
    cw = convw_ref[...]
    conv = (ubuf_ref[:, CONV_PAD - 2:CONV_PAD - 2 + ls, :] * cw[0:1, :]
            + ubuf_ref[:, CONV_PAD - 1:CONV_PAD - 1 + ls, :] * cw[1:2, :]
            + u * cw[2:3, :])
    y_ref[:, :, 0:CONV_WIDTH] = z_ref[:, :, COL_B:COL_B + CONV_WIDTH] * conv
    convn_ref[...] = ubuf_ref[:, CONV_PAD + ls - (CONV_K - 1):CONV_PAD + ls, :]

    pbuf_ref[:, POOL_PAD - POOL_HIST:POOL_PAD, :] = poolc_ref[...]
    pbuf_ref[:, POOL_PAD:POOL_PAD + ls, :] = z_ref[:, :, COL_P:COL_P + POOL_WIDTH]
    pos = (start_pos + lax.broadcasted_iota(jnp.int32, (1, ls, 1), 1)).astype(F32)
    for gi, w in enumerate(POOL_WINDOWS):
        c0 = gi * POOL_GROUP
        tok = pbuf_ref[:, POOL_PAD:POOL_PAD + ls, c0:c0 + POOL_GROUP]
        win = tok
        for j in range(1, w):
            win = win + pbuf_ref[:, POOL_PAD - j:POOL_PAD - j + ls, c0:c0 + POOL_GROUP]
        d = win / _window_count(w, pos) - tok
        yb = _dot(d.reshape(nb * ls, POOL_GROUP).astype(BF16), poolw_ref[gi]).reshape(nb, ls, POOL_GROUP)
        y_ref[:, :, OUT_POOL + c0:OUT_POOL + c0 + POOL_GROUP] = yb * pscale_ref[:, c0:c0 + POOL_GROUP]
    pooln_ref[...] = pbuf_ref[:, POOL_PAD + ls - POOL_HIST:POOL_PAD + ls, :]

    tables = _retention_tables(ls)
    cos = cos_ref[...]
    sin = sin_ref[...]

    def per_sequence(b, carry):
        for h in range(RET_HEADS):
            h0 = h * RET_HEAD_DIM
            y, s_new = _retention_head(
                z_ref[b, :, COL_Q + h0:COL_Q + h0 + RET_HEAD_DIM],
                z_ref[b, :, COL_K + h0:COL_K + h0 + RET_HEAD_DIM],
                z_ref[b, :, COL_V + h0:COL_V + h0 + RET_HEAD_DIM],
                z_ref[b, :, COL_G + h0:COL_G + h0 + RET_HEAD_DIM],
                state_ref[b, h], cos, sin, tables[h])
            staten_ref[b, h] = s_new
            y_ref[b, :, OUT_RET + h0:OUT_RET + h0 + RET_HEAD_DIM] = y
        return carry

    lax.fori_loop(0, nb, per_sequence, 0, unroll=4)

    yv = y_ref[...].reshape(nb * ls, D_MODEL).astype(BF16)
    m = jnp.concatenate([_dot(yv, w_ref[...]) for w_ref in wout_refs], axis=1)
    x2 = _layer_norm(DEEPNORM_ALPHA * x_ref[...].reshape(nb * ls, D_MODEL) + m, g_ref[...], b_ref[...])
    o_ref[...] = x2.reshape(nb, ls, D_MODEL)


def _mixer_sample(z, x1, cos, sin, conv_w, pool_w, pool_scale, conv_cache, pool_cache, state, state_out,
                  w_out, ln_g, ln_b, layer, start_pos, nb=8):
    B, ls, _ = z.shape
    seq3 = lambda i: (i, 0, 0)
    lay3 = lambda i: (layer, 0, 0)
    lseq4 = lambda i: (layer, i, 0, 0)
    lseq5 = lambda i: (layer, i, 0, 0, 0)
    in_specs = [
        pl.BlockSpec((nb, ls, IN_COLS), seq3),
        pl.BlockSpec((ls, RET_HEAD_DIM // 2), lambda i: (0, 0)),
        pl.BlockSpec((ls, RET_HEAD_DIM // 2), lambda i: (0, 0)),
        pl.BlockSpec((None, CONV_K, CONV_WIDTH), lay3),
        pl.BlockSpec((None, len(POOL_WINDOWS), POOL_GROUP, POOL_GROUP), lambda i: (layer, 0, 0, 0)),
        pl.BlockSpec((None, 1, POOL_WIDTH), lay3),
        pl.BlockSpec((None, nb, CONV_K - 1, CONV_WIDTH), lseq4),
        pl.BlockSpec((None, nb, POOL_HIST, POOL_WIDTH), lseq4),
        pl.BlockSpec((None, nb, RET_HEADS, RET_HEAD_DIM, RET_HEAD_DIM), lseq5),
        pl.BlockSpec((nb, ls, D_MODEL), seq3),
        pl.BlockSpec((None, 1, D_MODEL), lay3),
        pl.BlockSpec((None, 1, D_MODEL), lay3),
    ] + [
        pl.BlockSpec((None, D_MODEL, D_MODEL // WOUT_BLOCKS), (lambda j: lambda i: (layer, 0, j))(j),
                     pipeline_mode=pl.Buffered(1))
        for j in range(WOUT_BLOCKS)
    ]
    args = [z, cos, sin, conv_w, pool_w, pool_scale, conv_cache, pool_cache, state, x1, ln_g, ln_b,
            *([w_out] * WOUT_BLOCKS)]
    aliases = {}
    if state_out is not None:
        in_specs.append(pl.BlockSpec(memory_space=pl.ANY))
        args.append(state_out)
        aliases = {len(args) - 1: 3}
    return pl.pallas_call(
        functools.partial(_mixer_sample_kernel, nb=nb, ls=ls, start_pos=start_pos,
                          aliased=state_out is not None),
        grid=(B // nb,),
        in_specs=in_specs,
        out_specs=[
            pl.BlockSpec((nb, ls, D_MODEL), seq3),
            pl.BlockSpec((nb, CONV_K - 1, CONV_WIDTH), seq3),
            pl.BlockSpec((nb, POOL_HIST, POOL_WIDTH), seq3),
            pl.BlockSpec((None, nb, RET_HEADS, RET_HEAD_DIM, RET_HEAD_DIM), lseq5),
        ],
        out_shape=[
            jax.ShapeDtypeStruct((B, ls, D_MODEL), F32),
            jax.ShapeDtypeStruct((B, CONV_K - 1, CONV_WIDTH), F32),
            jax.ShapeDtypeStruct((B, POOL_HIST, POOL_WIDTH), F32),
            jax.ShapeDtypeStruct(state.shape, F32),
        ],
        scratch_shapes=[
            pltpu.VMEM((nb, ls, D_MODEL), F32),
            pltpu.VMEM((nb, CONV_PAD + ls, CONV_WIDTH), F32),
            pltpu.VMEM((nb, POOL_PAD + ls, POOL_WIDTH), F32),
        ],
        input_output_aliases=aliases,
        compiler_params=_params(("parallel",), 48),
        name="mixer_sample",
    )(*args)


def _rope_tables(start_pos, length):
    half = RET_HEAD_DIM // 2
    inv = ROPE_BASE ** (-jnp.arange(half, dtype=F32) / half)
    pos = start_pos + jnp.arange(length, dtype=F32)
    ang = pos[:, None] * inv[None, :]
    return jnp.cos(ang), jnp.sin(ang)


def _trunk(x, p, caches, start_pos, w, packed):
    B, L, D = x.shape
    T = B * L
    x = x.reshape(T, D)
    p = p.reshape(DEPTH, T, -1)
    cos, sin = _rope_tables(start_pos, L)
    convs, pools, rets = [], [], []
    state_out = None
    collect = not packed

    def ffn(x, name, ln, layer, with_ple=False):
        ln_g, ln_b = w[ln + "_g"], w[ln + "_b"]
        if collect:
            ple = (w["ple_gate"], p, w["ple_proj"]) if with_ple else None
            out, *copies = _ffn_ln(x, (w[name + "_w_gate"], w[name + "_w_up"], w[name + "_w_down"]),
                                   ln_g, ln_b, layer, ple=ple)
            packed[name, layer] = tuple(copies[:2])
            if with_ple:
                packed["ple", layer] = tuple(copies[2:])
            return out
        ple = (packed["ple", layer][0], p, packed["ple", layer][1]) if with_ple else None
        return _ffn_ln(x, packed[name, layer], ln_g, ln_b, layer, ple=ple)[0]

    for i in range(DEPTH):
        x1 = ffn(x, "ffn1", "ln1", i)
        if collect:
            z, packed["w_in", i] = _proj_in(x1, w["w_in"], i)
        else:
            z, = _proj_in(x1, packed["w_in", i], i)
        if caches is None:
            x2, c_new, p_new, r_new = _mixer_prompt(
                z, x1, cos, sin, w["conv_w"], w["pool_w"], w["pool_scale"], w["w_out"], w["ln2_g"], w["ln2_b"],
                i, B)
            rets.append(r_new)
        else:
            x2, c_new, p_new, state_out = _mixer_sample(
                z.reshape(B, L, IN_COLS), x1.reshape(B, L, D), cos, sin, w["conv_w"], w["pool_w"],
                w["pool_scale"], caches[0], caches[1], caches[2], state_out,
                w["w_out"], w["ln2_g"], w["ln2_b"], i, start_pos)
            x2 = x2.reshape(T, D)
        x = ffn(x2, "ffn2", "ln3", i, with_ple=True)
        convs.append(c_new)
        pools.append(p_new)
    ret = jnp.stack(rets) if caches is None else state_out
    return x.reshape(B, L, D), jnp.stack(convs), jnp.stack(pools), ret


def kernel(x_prompt, x_sample, p_prompt, p_sample, cache_conv, cache_pool, state_ret, ln1_g, ln1_b, ffn1_w_gate, ffn1_w_up, ffn1_w_down, w_in, conv_w, pool_w, pool_scale, w_out, ln2_g, ln2_b, ffn2_w_gate, ffn2_w_up, ffn2_w_down, ple_gate, ple_proj, ln3_g, ln3_b):
    precast = dict(pool_w=pool_w, w_out=w_out)
    row_params = dict(ln1_g=ln1_g, ln1_b=ln1_b, ln2_g=ln2_g, ln2_b=ln2_b, ln3_g=ln3_g, ln3_b=ln3_b,
                      pool_scale=pool_scale)
    w = {k: _to_bf16(v) for k, v in precast.items()}
    w.update({k: v.reshape(DEPTH, 1, -1) for k, v in row_params.items()})
    w.update(conv_w=conv_w, ffn1_w_gate=ffn1_w_gate, ffn1_w_up=ffn1_w_up, ffn1_w_down=ffn1_w_down,
             ffn2_w_gate=ffn2_w_gate, ffn2_w_up=ffn2_w_up, ffn2_w_down=ffn2_w_down,
             w_in=w_in, ple_gate=ple_gate, ple_proj=ple_proj)
    packed = {}
    y_sample, conv_s, pool_s, ret_s = _trunk(
        x_sample, p_sample, (cache_conv, cache_pool, state_ret), PAST_LEN, w, packed)
    y_prompt, conv_p, pool_p, ret_p = _trunk(x_prompt, p_prompt, None, 0, w, packed)
    return (y_prompt, y_sample, conv_p, pool_p, ret_p, conv_s, pool_s, ret_s)
```

```python
import functools
import math

import jax
import jax.numpy as jnp
from jax import lax
from jax.experimental import pallas as pl
from jax.experimental.pallas import tpu as pltpu

D_MODEL = 2048
DEPTH = 2
PAST_LEN = 16384
CONV_WIDTH = D_MODEL // 4
POOL_WIDTH = D_MODEL // 4
RET_WIDTH = D_MODEL // 2
CONV_K = 3
POOL_WINDOWS = (2, 4, 8, 16)
POOL_GROUP = POOL_WIDTH // len(POOL_WINDOWS)
POOL_HIST = max(POOL_WINDOWS) - 1
RET_HEADS = 4
RET_HEAD_DIM = RET_WIDTH // RET_HEADS
RET_CHUNK = 128
RET_LOG_GAMMA = tuple(math.log(1.0 - 2.0 ** (-5 - h)) for h in range(RET_HEADS))
ROPE_BASE = 10000.0
IN_COLS = 3 * CONV_WIDTH + POOL_WIDTH + 4 * RET_WIDTH
DEEPNORM_ALPHA = (2 * DEPTH) ** 0.25
LN_EPS = 1e-5

COL_B = 0
COL_C = CONV_WIDTH
COL_H = 2 * CONV_WIDTH
COL_P = 3 * CONV_WIDTH
COL_Q = COL_P + POOL_WIDTH
COL_K = COL_Q + RET_WIDTH
COL_V = COL_K + RET_WIDTH
COL_G = COL_V + RET_WIDTH
OUT_POOL = CONV_WIDTH
OUT_RET = CONV_WIDTH + POOL_WIDTH

SUBLANES = 8
CONV_PAD = SUBLANES
POOL_PAD = 16
WOUT_BLOCKS = 4

F32 = jnp.float32
BF16 = jnp.bfloat16
MIB = 1024 * 1024


def _params(semantics, vmem_mib):
    return pltpu.CompilerParams(dimension_semantics=semantics, vmem_limit_bytes=vmem_mib * MIB)


def _layer_norm(r, g, b):
    mu = jnp.mean(r, axis=-1, keepdims=True)
    c = r - mu
    var = jnp.mean(c * c, axis=-1, keepdims=True)
    return c * lax.rsqrt(var + LN_EPS) * g + b


def _silu(x):
    return x * jax.nn.sigmoid(x)


def _dot(a, b):
    return jnp.dot(a, b, preferred_element_type=F32)


def _cast_kernel(x_ref, o_ref):
    cb, _, cw = o_ref.shape
    for j in range(cb):
        o_ref[j] = x_ref[:, j * cw:(j + 1) * cw].astype(o_ref.dtype)


def _to_bf16(w, col_blocks=1, block_bytes=6 * MIB):
    depth, C = w.shape[0], w.shape[-1]
    w = w.reshape(depth, -1, C)
    R = w.shape[1]
    tr = R
    while tr * C * 4 > block_bytes and tr % 2 == 0 and (tr // 2) % 16 == 0:
        tr //= 2
    cw = C // col_blocks
    return pl.pallas_call(
        _cast_kernel,
        grid=(depth, R // tr),
        in_specs=[pl.BlockSpec((None, tr, C), lambda d, r: (d, r, 0))],
        out_specs=pl.BlockSpec((None, col_blocks, tr, cw), lambda d, r: (d, 0, r, 0)),
        out_shape=jax.ShapeDtypeStruct((depth, col_blocks, R, cw), BF16),
        compiler_params=_params(("parallel", "parallel"), 40),
        name="to_bf16",
    )(w)


FFN_PACK = 256


def _ffn_ln_kernel(*refs, nf, tf, packed, ple_cols):
    refs = list(refs)
    x_ref = refs.pop(0)
    if packed:
        wgu_ref, wd_ref = refs[:2]
        refs = refs[2:]
    else:
        wg_ref, wu_ref, wd_ref = refs[:3]
        refs = refs[3:]
    gb_ref = refs.pop(0)
    if ple_cols:
        gate_ref, p_ref, proj_ref = refs[:3]
        refs = refs[3:]
    o_ref = refs.pop(0)
    if not packed:
        wgu_out_ref, wd_out_ref = refs[:2]
        refs = refs[2:]
        if ple_cols:
            gate_out_ref, proj_out_ref = refs[:2]
            refs = refs[2:]
    xb_ref = refs.pop(0)
    f = pl.program_id(1)
    n_ple = o_ref.shape[1] // ple_cols if ple_cols else 0

    @pl.when(f == 0)
    def _():
        x = x_ref[...]
        xb_ref[...] = x.astype(BF16)
        o_ref[...] = DEEPNORM_ALPHA * x

    def step(with_ple):
        if packed:
            wd = wd_ref[...]
        else:
            wg = wg_ref[...].astype(BF16)
            wu = wu_ref[...].astype(BF16)
            wd = wd_ref[...].astype(BF16)
            wgu_out_ref[...] = jnp.concatenate([wg, wu], axis=1)
            wd_out_ref[...] = wd
        xb = xb_ref[...]
        if packed:
            hs = []
            for s in range(tf // FFN_PACK):
                c0 = 2 * FFN_PACK * s
                gu = _dot(xb, wgu_ref[:, c0:c0 + 2 * FFN_PACK])
                hs.append((_silu(gu[:, :FFN_PACK]) * gu[:, FFN_PACK:]).astype(BF16))
            h = jnp.concatenate(hs, axis=1)
        else:
            h = (_silu(_dot(xb, wg)) * _dot(xb, wu)).astype(BF16)
        o_ref[...] += 0.5 * _dot(h, wd)
        if with_ple:
            gate_w = gate_ref[...]
            proj_w = proj_ref[...]
            if not packed:
                gate_w = gate_w.astype(BF16)
                proj_w = proj_w.astype(BF16)
                gate_out_ref[...] = gate_w
                proj_out_ref[...] = proj_w
            gate = jax.nn.sigmoid(_dot(xb_ref[...], gate_w))
            cols = pl.ds(pl.multiple_of(f * ple_cols, ple_cols), ple_cols)
            o_ref[:, cols] += gate * _dot(p_ref[...].astype(BF16), proj_w)

    if n_ple:
        pl.when(f < n_ple)(functools.partial(step, True))
        pl.when(f >= n_ple)(functools.partial(step, False))
    else:
        step(False)

    @pl.when(f == nf - 1)
    def _():
        o_ref[...] = _layer_norm(o_ref[...], gb_ref[0:1, :], gb_ref[1:2, :])


def _ffn_ln(x, weights, ln_gb, layer, ple=None, tm=1024, packed_tf=512, ple_cols=512):
    T, D = x.shape
    packed = len(weights) == 2
    tf = packed_tf if packed else FFN_PACK
    F = weights[1].shape[0] if packed else weights[0].shape[2]
    nf = F // tf
    n_ple = D // ple_cols if ple is not None else 0
    row = lambda i, f: (i, 0)
    const = lambda i, f: (layer, 0, 0)
    if packed:
        w_specs = [pl.BlockSpec((D, 2 * tf), lambda i, f: (0, f)),
                   pl.BlockSpec((tf, D), lambda i, f: (f, 0))]
        single = None
    else:
        assert T == tm
        w_specs = [pl.BlockSpec((None, D, tf), lambda i, f: (layer, 0, f)),
                   pl.BlockSpec((None, D, tf), lambda i, f: (layer, 0, f)),
                   pl.BlockSpec((None, tf, D), lambda i, f: (layer, f, 0))]
        single = pl.Buffered(1)
    in_specs = [pl.BlockSpec((tm, D), row, pipeline_mode=single)] + w_specs + [
        pl.BlockSpec((None, 2, D), const),
    ]
    args = [x, *weights, ln_gb]
    if ple is not None:
        gate_w, p, proj_w = ple
        P = p.shape[2]
        assert n_ple <= nf
        pcol = lambda i, f: (0, jnp.minimum(f, n_ple - 1))
        lpcol = lambda i, f: (layer, 0, jnp.minimum(f, n_ple - 1))
        in_specs += [
            pl.BlockSpec((D, ple_cols), pcol) if packed else pl.BlockSpec((None, D, ple_cols), lpcol),
            pl.BlockSpec((None, tm, P), lambda i, f: (layer, i, 0)),
            pl.BlockSpec((P, ple_cols), pcol) if packed else pl.BlockSpec((None, P, ple_cols), lpcol),
        ]
        args += [gate_w, p, proj_w]
    out_shape = [jax.ShapeDtypeStruct((T, D), F32)]
    out_specs = [pl.BlockSpec((tm, D), row, pipeline_mode=single)]
    if not packed:
        out_shape += [jax.ShapeDtypeStruct((D, 2 * F), BF16), jax.ShapeDtypeStruct((F, D), BF16)]
        out_specs += [pl.BlockSpec((D, 2 * tf), lambda i, f: (0, f)),
                      pl.BlockSpec((tf, D), lambda i, f: (f, 0))]
        if ple is not None:
            out_shape += [jax.ShapeDtypeStruct((D, D), BF16), jax.ShapeDtypeStruct((P, D), BF16)]
            out_specs += [pl.BlockSpec((D, ple_cols), pcol), pl.BlockSpec((P, ple_cols), pcol)]
    return pl.pallas_call(
        functools.partial(_ffn_ln_kernel, nf=nf, tf=tf, packed=packed,
                          ple_cols=ple_cols if ple is not None else 0),
        grid=(T // tm, nf),
        in_specs=in_specs,
        out_specs=out_specs,
        out_shape=out_shape,
        scratch_shapes=[pltpu.VMEM((tm, D), BF16)],
        compiler_params=_params(("parallel", "arbitrary"), 62),
        name="ffn_ln",
    )(*args)


def _proj_in_kernel(x_ref, w_ref, o_ref, *rest, emit):
    xb_ref = rest[-1]

    @pl.when(pl.program_id(1) == 0)
    def _():
        xb_ref[...] = x_ref[...].astype(BF16)

    w = w_ref[...]
    if emit:
        w = w.astype(BF16)
        rest[0][...] = w
    o_ref[...] = _dot(xb_ref[...], w)


def _proj_in(x, w, layer, tm=1024, tn=None):
    T, D = x.shape
    emit = w.ndim == 3
    N = w.shape[-1]
    if emit:
        assert T == tm
        tn = tn or 1024
        w_spec = pl.BlockSpec((None, D, tn), lambda i, n: (layer, 0, n))
    else:
        tn = tn or 2048
        w_spec = pl.BlockSpec((D, tn), lambda i, n: (0, n))
    out_shape = [jax.ShapeDtypeStruct((T, N), F32)]
    out_specs = [pl.BlockSpec((tm, tn), lambda i, n: (i, n))]
    if emit:
        out_shape.append(jax.ShapeDtypeStruct((D, N), BF16))
        out_specs.append(pl.BlockSpec((D, tn), lambda i, n: (0, n)))
    return pl.pallas_call(
        functools.partial(_proj_in_kernel, emit=emit),
        grid=(T // tm, N // tn),
        in_specs=[pl.BlockSpec((tm, D), lambda i, n: (i, 0)), w_spec],
        out_specs=out_specs,
        out_shape=out_shape,
        scratch_shapes=[pltpu.VMEM((tm, D), BF16)],
        compiler_params=_params(("parallel", "arbitrary"), 58),
        name="proj_in",
    )(x, w)


def _rotary(x, cos, sin):
    half = RET_HEAD_DIM // 2
    x1, x2 = x[:, :half], x[:, half:]
    return jnp.concatenate([x1 * cos - x2 * sin, x1 * sin + x2 * cos], axis=-1)


def _retention_tables(L):
    row = lax.broadcasted_iota(jnp.int32, (L, L), 0)
    col = lax.broadcasted_iota(jnp.int32, (L, L), 1)
    diff = (row - col).astype(F32)
    causal = row >= col
    idx = lax.broadcasted_iota(jnp.int32, (L, 1), 0).astype(F32)
    tables = []
    for lg in RET_LOG_GAMMA:
        decay = jnp.where(causal, jnp.exp(jnp.where(causal, diff, 0.0) * lg), 0.0)
        q_decay = jnp.exp((idx + 1.0) * lg)
        k_decay = jnp.exp((L - 1.0 - idx) * lg)
        tables.append((decay, q_decay, k_decay, math.exp(L * lg)))
    return tables


def _retention_head(q, k, v, gate, S, cos, sin, table):
    decay, q_decay, k_decay, s_decay = table
    qr = _rotary(q, cos, sin)
    kr = _rotary(k, cos, sin) * (RET_HEAD_DIM ** -0.5)
    qb = qr.astype(BF16)
    vb = v.astype(BF16)
    scores = lax.dot_general(qb, kr.astype(BF16), (((1,), (1,)), ((), ())), preferred_element_type=F32)
    inner = _dot((scores * decay).astype(BF16), vb)
    cross = _dot(qb, S.astype(BF16)) * q_decay
    o = inner + cross
    kd = (kr * k_decay).astype(BF16)
    S_new = s_decay * S + lax.dot_general(kd, vb, (((0,), (0,)), ((), ())), preferred_element_type=F32)
    mu = jnp.mean(o, axis=-1, keepdims=True)
    c = o - mu
    var = jnp.mean(c * c, axis=-1, keepdims=True)
    y = c * lax.rsqrt(var + LN_EPS) * _silu(gate)
    return y, S_new


def _window_count(w, pos):
    return jnp.minimum(jnp.float32(w), pos + 1.0)


def _mixer_prompt_kernel(z_ref, rope_ref, convw_ref, poolw_ref, pscale_ref, x_ref, gb_ref, wout_ref,
                         o_ref, convn_ref, pooln_ref, retn_ref,
                         y_ref, s_ref, ubuf_ref, pbuf_ref, *, tl, nl):
    half = RET_HEAD_DIM // 2
    l = pl.program_id(1)

    @pl.when(l == 0)
    def _():
        s_ref[...] = jnp.zeros_like(s_ref)
        ubuf_ref[0:CONV_PAD, :] = jnp.zeros((CONV_PAD, CONV_WIDTH), F32)
        pbuf_ref[0:POOL_PAD, :] = jnp.zeros((POOL_PAD, POOL_WIDTH), F32)

    u = z_ref[:, COL_C:COL_C + CONV_WIDTH] * z_ref[:, COL_H:COL_H + CONV_WIDTH]
    ubuf_ref[CONV_PAD:CONV_PAD + tl, :] = u
    cw = convw_ref[...]
    conv = (ubuf_ref[CONV_PAD - 2:CONV_PAD - 2 + tl, :] * cw[0:1, :]
            + ubuf_ref[CONV_PAD - 1:CONV_PAD - 1 + tl, :] * cw[1:2, :]
            + u * cw[2:3, :])
    y_ref[:, 0:CONV_WIDTH] = (z_ref[:, COL_B:COL_B + CONV_WIDTH] * conv).astype(y_ref.dtype)

    pbuf_ref[POOL_PAD:POOL_PAD + tl, :] = z_ref[:, COL_P:COL_P + POOL_WIDTH]
    pos = (l * tl + lax.broadcasted_iota(jnp.int32, (tl, 1), 0)).astype(F32)
    for gi, w in enumerate(POOL_WINDOWS):
        c0 = gi * POOL_GROUP
        tok = pbuf_ref[POOL_PAD:POOL_PAD + tl, c0:c0 + POOL_GROUP]
        win = tok
        for j in range(1, w):
            win = win + pbuf_ref[POOL_PAD - j:POOL_PAD - j + tl, c0:c0 + POOL_GROUP]
        d = win / _window_count(w, pos) - tok
        yb = _dot(d.astype(BF16), poolw_ref[gi]) * pscale_ref[:, c0:c0 + POOL_GROUP]
        y_ref[:, OUT_POOL + c0:OUT_POOL + c0 + POOL_GROUP] = yb.astype(y_ref.dtype)

    tables = _retention_tables(RET_CHUNK)
    for c in range(tl // RET_CHUNK):
        r0 = c * RET_CHUNK
        cos = rope_ref[r0:r0 + RET_CHUNK, 0:half]
        sin = rope_ref[r0:r0 + RET_CHUNK, half:2 * half]
        for h in range(RET_HEADS):
            h0 = h * RET_HEAD_DIM
            y, s_new = _retention_head(
                z_ref[r0:r0 + RET_CHUNK, COL_Q + h0:COL_Q + h0 + RET_HEAD_DIM],
                z_ref[r0:r0 + RET_CHUNK, COL_K + h0:COL_K + h0 + RET_HEAD_DIM],
                z_ref[r0:r0 + RET_CHUNK, COL_V + h0:COL_V + h0 + RET_HEAD_DIM],
                z_ref[r0:r0 + RET_CHUNK, COL_G + h0:COL_G + h0 + RET_HEAD_DIM],
                s_ref[h], cos, sin, tables[h])
            s_ref[h] = s_new
            y_ref[r0:r0 + RET_CHUNK, OUT_RET + h0:OUT_RET + h0 + RET_HEAD_DIM] = y.astype(y_ref.dtype)
        yc = y_ref[r0:r0 + RET_CHUNK, :]
        m = jnp.concatenate([_dot(yc, wout_ref[j]) for j in range(WOUT_BLOCKS)], axis=1)
        o_ref[r0:r0 + RET_CHUNK, :] = _layer_norm(
            DEEPNORM_ALPHA * x_ref[r0:r0 + RET_CHUNK, :] + m, gb_ref[0:1, :], gb_ref[1:2, :])

    @pl.when(l == nl - 1)
    def _():
        convn_ref[0] = ubuf_ref[CONV_PAD + tl - (CONV_K - 1):CONV_PAD + tl, :]
        pooln_ref[0] = pbuf_ref[POOL_PAD + tl - POOL_HIST:POOL_PAD + tl, :]
        retn_ref[0] = s_ref[...]

    ubuf_ref[0:CONV_PAD, :] = ubuf_ref[tl:tl + CONV_PAD, :]
    pbuf_ref[0:POOL_PAD, :] = pbuf_ref[tl:tl + POOL_PAD, :]


def _mixer_prompt(z, x1, rope, conv_w, pool_w, pool_scale, w_out, ln_gb, layer, batch, tl=256):
    T = z.shape[0]
    L = T // batch
    nl = L // tl
    row = lambda b, l: (b * nl + l, 0)
    lay3 = lambda b, l: (layer, 0, 0)
    return pl.pallas_call(
        functools.partial(_mixer_prompt_kernel, tl=tl, nl=nl),
        grid=(batch, nl),
        in_specs=[
            pl.BlockSpec((tl, IN_COLS), row),
            pl.BlockSpec((tl, RET_HEAD_DIM), lambda b, l: (l, 0)),
            pl.BlockSpec((None, CONV_K, CONV_WIDTH), lay3),
            pl.BlockSpec((None, len(POOL_WINDOWS), POOL_GROUP, POOL_GROUP), lambda b, l: (layer, 0, 0, 0)),
            pl.BlockSpec((None, 1, POOL_WIDTH), lay3),
            pl.BlockSpec((tl, D_MODEL), row),
            pl.BlockSpec((None, 2, D_MODEL), lay3),
            pl.BlockSpec((None, WOUT_BLOCKS, D_MODEL, D_MODEL // WOUT_BLOCKS), lambda b, l: (layer, 0, 0, 0),
                         pipeline_mode=pl.Buffered(1)),
        ],
        out_specs=[
            pl.BlockSpec((tl, D_MODEL), row),
            pl.BlockSpec((1, CONV_K - 1, CONV_WIDTH), lambda b, l: (b, 0, 0)),
            pl.BlockSpec((1, POOL_HIST, POOL_WIDTH), lambda b, l: (b, 0, 0)),
            pl.BlockSpec((1, RET_HEADS, RET_HEAD_DIM, RET_HEAD_DIM), lambda b, l: (b, 0, 0, 0)),
        ],
        out_shape=[
            jax.ShapeDtypeStruct((T, D_MODEL), F32),
            jax.ShapeDtypeStruct((batch, CONV_K - 1, CONV_WIDTH), F32),
            jax.ShapeDtypeStruct((batch, POOL_HIST, POOL_WIDTH), F32),
            jax.ShapeDtypeStruct((batch, RET_HEADS, RET_HEAD_DIM, RET_HEAD_DIM), F32),
        ],
        scratch_shapes=[
            pltpu.VMEM((tl, D_MODEL), BF16),
            pltpu.VMEM((RET_HEADS, RET_HEAD_DIM, RET_HEAD_DIM), F32),
            pltpu.VMEM((CONV_PAD + tl, CONV_WIDTH), F32),
            pltpu.VMEM((POOL_PAD + tl, POOL_WIDTH), F32),
        ],
        compiler_params=_params(("parallel", "arbitrary"), 48),
        name="mixer_prompt",
    )(z, rope, conv_w, pool_w, pool_scale, x1, ln_gb, w_out)


def _mixer_sample_kernel(z_ref, rope_ref, convw_ref, poolw_ref, pscale_ref,
                         convc_ref, poolc_ref, state_ref, x_ref, gb_ref, wout_ref, *rest,
                         nb, ls, start_pos, aliased):
    half = RET_HEAD_DIM // 2
    if aliased:
        rest = rest[1:]
    o_ref, convn_ref, pooln_ref, staten_ref, y_ref, ubuf_ref, pbuf_ref = rest
    u = z_ref[:, :, COL_C:COL_C + CONV_WIDTH] * z_ref[:, :, COL_H:COL_H + CONV_WIDTH]
    ubuf_ref[:, CONV_PAD - (CONV_K - 1):CONV_PAD, :] = convc_ref[...]
    ubuf_ref[:, CONV_PAD:CONV_PAD + ls, :] = u
    cw = convw_ref[...]
    conv = (ubuf_ref[:, CONV_PAD - 2:CONV_PAD - 2 + ls, :] * cw[0:1, :]
            + ubuf_ref[:, CONV_PAD - 1:CONV_PAD - 1 + ls, :] * cw[1:2, :]
            + u * cw[2:3, :])
    y_ref[:, :, 0:CONV_WIDTH] = z_ref[:, :, COL_B:COL_B + CONV_WIDTH] * conv
    convn_ref[...] = ubuf_ref[:, CONV_PAD + ls - (CONV_K - 1):CONV_PAD + ls, :]

    pbuf_ref[:, POOL_PAD - POOL_HIST:POOL_PAD, :] = poolc_ref[...]
    pbuf_ref[:, POOL_PAD:POOL_PAD + ls, :] = z_ref[:, :, COL_P:COL_P + POOL_WIDTH]
    pos = (start_pos + lax.broadcasted_iota(jnp.int32, (1, ls, 1), 1)).astype(F32)
    for gi, w in enumerate(POOL_WINDOWS):
        c0 = gi * POOL_GROUP
        tok = pbuf_ref[:, POOL_PAD:POOL_PAD + ls, c0:c0 + POOL_GROUP]
        win = tok
        for j in range(1, w):
            win = win + pbuf_ref[:, POOL_PAD - j:POOL_PAD - j + ls, c0:c0 + POOL_GROUP]
        d = win / _window_count(w, pos) - tok
        yb = _dot(d.reshape(nb * ls, POOL_GROUP).astype(BF16), poolw_ref[gi]).reshape(nb, ls, POOL_GROUP)
        y_ref[:, :, OUT_POOL + c0:OUT_POOL + c0 + POOL_GROUP] = yb * pscale_ref[:, c0:c0 + POOL_GROUP]
    pooln_ref[...] = pbuf_ref[:, POOL_PAD + ls - POOL_HIST:POOL_PAD + ls, :]

    tables = _retention_tables(ls)
    cos = rope_ref[:, 0:half]
    sin = rope_ref[:, half:2 * half]

    def per_sequence(b, carry):
        for h in range(RET_HEADS):
            h0 = h * RET_HEAD_DIM
            y, s_new = _retention_head(
                z_ref[b, :, COL_Q + h0:COL_Q + h0 + RET_HEAD_DIM],
                z_ref[b, :, COL_K + h0:COL_K + h0 + RET_HEAD_DIM],
                z_ref[b, :, COL_V + h0:COL_V + h0 + RET_HEAD_DIM],
                z_ref[b, :, COL_G + h0:COL_G + h0 + RET_HEAD_DIM],
                state_ref[b, h], cos, sin, tables[h])
            staten_ref[b, h] = s_new
            y_ref[b, :, OUT_RET + h0:OUT_RET + h0 + RET_HEAD_DIM] = y
        return carry

    lax.fori_loop(0, nb, per_sequence, 0, unroll=4)

    yv = y_ref[...].reshape(nb * ls, D_MODEL).astype(BF16)
    m = jnp.concatenate([_dot(yv, wout_ref[j]) for j in range(WOUT_BLOCKS)], axis=1)
    x2 = _layer_norm(DEEPNORM_ALPHA * x_ref[...].reshape(nb * ls, D_MODEL) + m, gb_ref[0:1, :], gb_ref[1:2, :])
    o_ref[...] = x2.reshape(nb, ls, D_MODEL)


def _mixer_sample(z, x1, rope, conv_w, pool_w, pool_scale, conv_cache, pool_cache, state, state_out,
                  w_out, ln_gb, layer, start_pos, nb=8):
    B, ls, _ = z.shape
    seq3 = lambda i: (i, 0, 0)
    lay3 = lambda i: (layer, 0, 0)
    lseq4 = lambda i: (layer, i, 0, 0)
    lseq5 = lambda i: (layer, i, 0, 0, 0)
    in_specs = [
        pl.BlockSpec((nb, ls, IN_COLS), seq3),
        pl.BlockSpec((ls, RET_HEAD_DIM), lambda i: (0, 0)),
        pl.BlockSpec((None, CONV_K, CONV_WIDTH), lay3),
        pl.BlockSpec((None, len(POOL_WINDOWS), POOL_GROUP, POOL_GROUP), lambda i: (layer, 0, 0, 0)),
        pl.BlockSpec((None, 1, POOL_WIDTH), lay3),
        pl.BlockSpec((None, nb, CONV_K - 1, CONV_WIDTH), lseq4),
        pl.BlockSpec((None, nb, POOL_HIST, POOL_WIDTH), lseq4),
        pl.BlockSpec((None, nb, RET_HEADS, RET_HEAD_DIM, RET_HEAD_DIM), lseq5),
        pl.BlockSpec((nb, ls, D_MODEL), seq3),
        pl.BlockSpec((None, 2, D_MODEL), lay3),
        pl.BlockSpec((None, WOUT_BLOCKS, D_MODEL, D_MODEL // WOUT_BLOCKS), lambda i: (layer, 0, 0, 0),
                     pipeline_mode=pl.Buffered(1)),
    ]
    args = [z, rope, conv_w, pool_w, pool_scale, conv_cache, pool_cache, state, x1, ln_gb, w_out]
    aliases = {}
    if state_out is not None:
        in_specs.append(pl.BlockSpec(memory_space=pl.ANY))
        args.append(state_out)
        aliases = {len(args) - 1: 3}
    return pl.pallas_call(
        functools.partial(_mixer_sample_kernel, nb=nb, ls=ls, start_pos=start_pos,
                          aliased=state_out is not None),
        grid=(B // nb,),
        in_specs=in_specs,
        out_specs=[
            pl.BlockSpec((nb, ls, D_MODEL), seq3),
            pl.BlockSpec((nb, CONV_K - 1, CONV_WIDTH), seq3),
            pl.BlockSpec((nb, POOL_HIST, POOL_WIDTH), seq3),
            pl.BlockSpec((None, nb, RET_HEADS, RET_HEAD_DIM, RET_HEAD_DIM), lseq5),
        ],
        out_shape=[
            jax.ShapeDtypeStruct((B, ls, D_MODEL), F32),
            jax.ShapeDtypeStruct((B, CONV_K - 1, CONV_WIDTH), F32),
            jax.ShapeDtypeStruct((B, POOL_HIST, POOL_WIDTH), F32),
            jax.ShapeDtypeStruct(state.shape, F32),
        ],
        scratch_shapes=[
            pltpu.VMEM((nb, ls, D_MODEL), F32),
            pltpu.VMEM((nb, CONV_PAD + ls, CONV_WIDTH), F32),
            pltpu.VMEM((nb, POOL_PAD + ls, POOL_WIDTH), F32),
        ],
        input_output_aliases=aliases,
        compiler_params=_params(("parallel",), 48),
        name="mixer_sample",
    )(*args)


def _rope_tables(start_pos, length):
    half = RET_HEAD_DIM // 2
    inv = ROPE_BASE ** (-jnp.arange(half, dtype=F32) / half)
    pos = start_pos + jnp.arange(length, dtype=F32)
    ang = pos[:, None] * inv[None, :]
    return jnp.concatenate([jnp.cos(ang), jnp.sin(ang)], axis=1)


def _trunk(x, p, caches, start_pos, w, packed):
    B, L, D = x.shape
    T = B * L
    x = x.reshape(T, D)
    p = p.reshape(DEPTH, T, -1)
    rope = _rope_tables(start_pos, L)
    convs, pools, rets = [], [], []
    state_out = None
    collect = not packed

    def ffn(x, name, ln, layer, with_ple=False):
        if collect:
            ple = (w["ple_gate"], p, w["ple_proj"]) if with_ple else None
            out, *copies = _ffn_ln(x, (w[name + "_w_gate"], w[name + "_w_up"], w[name + "_w_down"]),
                                   w[ln], layer, ple=ple)
            packed[name, layer] = tuple(copies[:2])
            if with_ple:
                packed["ple", layer] = tuple(copies[2:])
            return out
        ple = (packed["ple", layer][0], p, packed["ple", layer][1]) if with_ple else None
        return _ffn_ln(x, packed[name, layer], w[ln], layer, ple=ple)[0]

    for i in range(DEPTH):
        x1 = ffn(x, "ffn1", "ln1", i)
        if collect:
            z, packed["w_in", i] = _proj_in(x1, w["w_in"], i)
        else:
            z, = _proj_in(x1, packed["w_in", i], i)
        if caches is None:
            x2, c_new, p_new, r_new = _mixer_prompt(
                z, x1, rope, w["conv_w"], w["pool_w"], w["pool_scale"], w["w_out"], w["ln2"], i, B)
            rets.append(r_new)
        else:
            x2, c_new, p_new, state_out = _mixer_sample(
                z.reshape(B, L, IN_COLS), x1.reshape(B, L, D), rope, w["conv_w"], w["pool_w"],
                w["pool_scale"], caches[0], caches[1], caches[2], state_out, w["w_out"], w["ln2"], i, start_pos)
            x2 = x2.reshape(T, D)
        x = ffn(x2, "ffn2", "ln3", i, with_ple=True)
        convs.append(c_new)
        pools.append(p_new)
    ret = jnp.stack(rets) if caches is None else state_out
    return x.reshape(B, L, D), jnp.stack(convs), jnp.stack(pools), ret


def kernel(x_prompt, x_sample, p_prompt, p_sample, cache_conv, cache_pool, state_ret, ln1_g, ln1_b, ffn1_w_gate, ffn1_w_up, ffn1_w_down, w_in, conv_w, pool_w, pool_scale, w_out, ln2_g, ln2_b, ffn2_w_gate, ffn2_w_up, ffn2_w_down, ple_gate, ple_proj, ln3_g, ln3_b):
    w = dict(
        pool_w=_to_bf16(pool_w).reshape(pool_w.shape),
        w_out=_to_bf16(w_out, col_blocks=WOUT_BLOCKS),
        pool_scale=pool_scale.reshape(DEPTH, 1, -1),
        ln1=jnp.stack([ln1_g, ln1_b], axis=1), ln2=jnp.stack([ln2_g, ln2_b], axis=1),
        ln3=jnp.stack([ln3_g, ln3_b], axis=1))
    w.update(conv_w=conv_w, ffn1_w_gate=ffn1_w_gate, ffn1_w_up=ffn1_w_up, ffn1_w_down=ffn1_w_down,
             ffn2_w_gate=ffn2_w_gate, ffn2_w_up=ffn2_w_up, ffn2_w_down=ffn2_w_down,
             w_in=w_in, ple_gate=ple_gate, ple_proj=ple_proj)
    packed = {}
    y_sample, conv_s, pool_s, ret_s = _trunk(
        x_sample, p_sample, (cache_conv, cache_pool, state_ret), PAST_LEN, w, packed)
    y_prompt, conv_p, pool_p, ret_p = _trunk(x_prompt, p_prompt, None, 0, w, packed)
    return (y_prompt, y_sample, conv_p, pool_p, ret_p, conv_s, pool_s, ret_s)
```

```python
import functools
import math

import jax
import jax.numpy as jnp
from jax import lax
from jax.experimental import pallas as pl
from jax.experimental.pallas import tpu as pltpu

D_MODEL = 2048
DEPTH = 2
PAST_LEN = 16384
CONV_WIDTH = D_MODEL // 4
POOL_WIDTH = D_MODEL // 4
RET_WIDTH = D_MODEL // 2
CONV_K = 3
POOL_WINDOWS = (2, 4, 8, 16)
POOL_GROUP = POOL_WIDTH // len(POOL_WINDOWS)
POOL_HIST = max(POOL_WINDOWS) - 1
RET_HEADS = 4
RET_HEAD_DIM = RET_WIDTH // RET_HEADS
RET_CHUNK = 128
RET_LOG_GAMMA = tuple(math.log(1.0 - 2.0 ** (-5 - h)) for h in range(RET_HEADS))
ROPE_BASE = 10000.0
IN_COLS = 3 * CONV_WIDTH + POOL_WIDTH + 4 * RET_WIDTH
DEEPNORM_ALPHA = (2 * DEPTH) ** 0.25
LN_EPS = 1e-5

COL_B = 0
COL_C = CONV_WIDTH
COL_H = 2 * CONV_WIDTH
COL_P = 3 * CONV_WIDTH
COL_Q = COL_P + POOL_WIDTH
COL_K = COL_Q + RET_WIDTH
COL_V = COL_K + RET_WIDTH
COL_G = COL_V + RET_WIDTH
OUT_POOL = CONV_WIDTH
OUT_RET = CONV_WIDTH + POOL_WIDTH

SUBLANES = 8
CONV_PAD = SUBLANES
POOL_PAD = 16
WOUT_BLOCKS = 4

F32 = jnp.float32
BF16 = jnp.bfloat16
MIB = 1024 * 1024


def _params(semantics, vmem_mib):
    return pltpu.CompilerParams(dimension_semantics=semantics, vmem_limit_bytes=vmem_mib * MIB)


def _layer_norm(r, g, b):
    mu = jnp.mean(r, axis=-1, keepdims=True)
    c = r - mu
    var = jnp.mean(c * c, axis=-1, keepdims=True)
    return c * lax.rsqrt(var + LN_EPS) * g + b


def _silu(x):
    return x * jax.nn.sigmoid(x)


def _dot(a, b):
    return jnp.dot(a, b, preferred_element_type=F32)


def _cast_kernel(x_ref, o_ref):
    cb, _, cw = o_ref.shape
    for j in range(cb):
        o_ref[j] = x_ref[:, j * cw:(j + 1) * cw].astype(o_ref.dtype)


def _to_bf16(w, col_blocks=1, block_bytes=6 * MIB):
    depth, C = w.shape[0], w.shape[-1]
    w = w.reshape(depth, -1, C)
    R = w.shape[1]
    tr = R
    while tr * C * 4 > block_bytes and tr % 2 == 0 and (tr // 2) % 16 == 0:
        tr //= 2
    cw = C // col_blocks
    return pl.pallas_call(
        _cast_kernel,
        grid=(depth, R // tr),
        in_specs=[pl.BlockSpec((None, tr, C), lambda d, r: (d, r, 0))],
        out_specs=pl.BlockSpec((None, col_blocks, tr, cw), lambda d, r: (d, 0, r, 0)),
        out_shape=jax.ShapeDtypeStruct((depth, col_blocks, R, cw), BF16),
        compiler_params=_params(("parallel", "parallel"), 40),
        name="to_bf16",
    )(w)


FFN_PACK = 256
FFN_ROWS = 512


def _ffn_ln_kernel(*refs, nf, tf, packed, ple_cols):
    refs = list(refs)
    x_ref = refs.pop(0)
    if packed:
        wgu_ref, wd_ref = refs[:2]
        refs = refs[2:]
    else:
        wg_ref, wu_ref, wd_ref = refs[:3]
        refs = refs[3:]
    gb_ref = refs.pop(0)
    if ple_cols:
        gate_ref, p_ref, proj_ref = refs[:3]
        refs = refs[3:]
    o_ref = refs.pop(0)
    if not packed:
        wgu_out_ref, wd_out_ref = refs[:2]
        refs = refs[2:]
        if ple_cols:
            gate_out_ref, proj_out_ref = refs[:2]
            refs = refs[2:]
    xb_ref = refs.pop(0)
    f = pl.program_id(1)
    n_ple = o_ref.shape[1] // ple_cols if ple_cols else 0

    @pl.when(f == 0)
    def _():
        x = x_ref[...]
        xb_ref[...] = x.astype(BF16)
        o_ref[...] = DEEPNORM_ALPHA * x

    def step(with_ple):
        if packed:
            wd = wd_ref[...]
        else:
            wg = wg_ref[...].astype(BF16)
            wu = wu_ref[...].astype(BF16)
            wd = wd_ref[...].astype(BF16)
            wgu_out_ref[...] = jnp.concatenate([wg, wu], axis=1)
            wd_out_ref[...] = wd
        for r0 in range(0, o_ref.shape[0], FFN_ROWS):
            xb = xb_ref[r0:r0 + FFN_ROWS, :]
            if packed:
                hs = []
                for s in range(tf // FFN_PACK):
                    c0 = 2 * FFN_PACK * s
                    gu = _dot(xb, wgu_ref[:, c0:c0 + 2 * FFN_PACK])
                    hs.append((_silu(gu[:, :FFN_PACK]) * gu[:, FFN_PACK:]).astype(BF16))
                h = jnp.concatenate(hs, axis=1)
            else:
                h = (_silu(_dot(xb, wg)) * _dot(xb, wu)).astype(BF16)
            o_ref[r0:r0 + FFN_ROWS, :] += 0.5 * _dot(h, wd)
        if with_ple:
            gate_w = gate_ref[...]
            proj_w = proj_ref[...]
            if not packed:
                gate_w = gate_w.astype(BF16)
                proj_w = proj_w.astype(BF16)
                gate_out_ref[...] = gate_w
                proj_out_ref[...] = proj_w
            gate = jax.nn.sigmoid(_dot(xb_ref[...], gate_w))
            cols = pl.ds(pl.multiple_of(f * ple_cols, ple_cols), ple_cols)
            o_ref[:, cols] += gate * _dot(p_ref[...].astype(BF16), proj_w)

    if n_ple:
        pl.when(f < n_ple)(functools.partial(step, True))
        pl.when(f >= n_ple)(functools.partial(step, False))
    else:
        step(False)

    @pl.when(f == nf - 1)
    def _():
        o_ref[...] = _layer_norm(o_ref[...], gb_ref[0:1, :], gb_ref[1:2, :])


def _ffn_ln(x, weights, ln_gb, layer, ple=None, tm=1024, packed_tf=512, ple_cols=512):
    T, D = x.shape
    packed = len(weights) == 2
    tf = packed_tf if packed else FFN_PACK
    F = weights[1].shape[0] if packed else weights[0].shape[2]
    nf = F // tf
    n_ple = D // ple_cols if ple is not None else 0
    row = lambda i, f: (i, 0)
    const = lambda i, f: (layer, 0, 0)
    if packed:
        w_specs = [pl.BlockSpec((D, 2 * tf), lambda i, f: (0, f)),
                   pl.BlockSpec((tf, D), lambda i, f: (f, 0))]
        single = None
    else:
        assert T == tm
        w_specs = [pl.BlockSpec((None, D, tf), lambda i, f: (layer, 0, f)),
                   pl.BlockSpec((None, D, tf), lambda i, f: (layer, 0, f)),
                   pl.BlockSpec((None, tf, D), lambda i, f: (layer, f, 0))]
        single = pl.Buffered(1)
    in_specs = [pl.BlockSpec((tm, D), row, pipeline_mode=single)] + w_specs + [
        pl.BlockSpec((None, 2, D), const),
    ]
    args = [x, *weights, ln_gb]
    if ple is not None:
        gate_w, p, proj_w = ple
        P = p.shape[2]
        assert n_ple <= nf
        pcol = lambda i, f: (0, jnp.minimum(f, n_ple - 1))
        lpcol = lambda i, f: (layer, 0, jnp.minimum(f, n_ple - 1))
        in_specs += [
            pl.BlockSpec((D, ple_cols), pcol) if packed else pl.BlockSpec((None, D, ple_cols), lpcol),
            pl.BlockSpec((None, tm, P), lambda i, f: (layer, i, 0)),
            pl.BlockSpec((P, ple_cols), pcol) if packed else pl.BlockSpec((None, P, ple_cols), lpcol),
        ]
        args += [gate_w, p, proj_w]
    out_shape = [jax.ShapeDtypeStruct((T, D), F32)]
    out_specs = [pl.BlockSpec((tm, D), row, pipeline_mode=single)]
    if not packed:
        out_shape += [jax.ShapeDtypeStruct((D, 2 * F), BF16), jax.ShapeDtypeStruct((F, D), BF16)]
        out_specs += [pl.BlockSpec((D, 2 * tf), lambda i, f: (0, f)),
                      pl.BlockSpec((tf, D), lambda i, f: (f, 0))]
        if ple is not None:
            out_shape += [jax.ShapeDtypeStruct((D, D), BF16), jax.ShapeDtypeStruct((P, D), BF16)]
            out_specs += [pl.BlockSpec((D, ple_cols), pcol), pl.BlockSpec((P, ple_cols), pcol)]
    return pl.pallas_call(
        functools.partial(_ffn_ln_kernel, nf=nf, tf=tf, packed=packed,
                          ple_cols=ple_cols if ple is not None else 0),
        grid=(T // tm, nf),
        in_specs=in_specs,
        out_specs=out_specs,
        out_shape=out_shape,
        scratch_shapes=[pltpu.VMEM((tm, D), BF16)],
        compiler_params=_params(("parallel", "arbitrary"), 62),
        name="ffn_ln",
    )(*args)


def _proj_in_kernel(x_ref, w_ref, o_ref, *rest, emit):
    xb_ref = rest[-1]

    @pl.when(pl.program_id(1) == 0)
    def _():
        xb_ref[...] = x_ref[...].astype(BF16)

    w = w_ref[...]
    if emit:
        w = w.astype(BF16)
        rest[0][...] = w
    o_ref[...] = _dot(xb_ref[...], w)


def _proj_in(x, w, layer, tm=1024, tn=None):
    T, D = x.shape
    emit = w.ndim == 3
    N = w.shape[-1]
    if emit:
        assert T == tm
        tn = tn or 1024
        w_spec = pl.BlockSpec((None, D, tn), lambda i, n: (layer, 0, n))
    else:
        tn = tn or 2048
        w_spec = pl.BlockSpec((D, tn), lambda i, n: (0, n))
    out_shape = [jax.ShapeDtypeStruct((T, N), F32)]
    out_specs = [pl.BlockSpec((tm, tn), lambda i, n: (i, n))]
    if emit:
        out_shape.append(jax.ShapeDtypeStruct((D, N), BF16))
        out_specs.append(pl.BlockSpec((D, tn), lambda i, n: (0, n)))
    return pl.pallas_call(
        functools.partial(_proj_in_kernel, emit=emit),
        grid=(T // tm, N // tn),
        in_specs=[pl.BlockSpec((tm, D), lambda i, n: (i, 0)), w_spec],
        out_specs=out_specs,
        out_shape=out_shape,
        scratch_shapes=[pltpu.VMEM((tm, D), BF16)],
        compiler_params=_params(("parallel", "arbitrary"), 58),
        name="proj_in",
    )(x, w)


def _rotary(x, cos, sin):
    half = RET_HEAD_DIM // 2
    x1, x2 = x[:, :half], x[:, half:]
    return jnp.concatenate([x1 * cos - x2 * sin, x1 * sin + x2 * cos], axis=-1)


def _retention_tables(L):
    row = lax.broadcasted_iota(jnp.int32, (L, L), 0)
    col = lax.broadcasted_iota(jnp.int32, (L, L), 1)
    diff = (row - col).astype(F32)
    causal = row >= col
    idx = lax.broadcasted_iota(jnp.int32, (L, 1), 0).astype(F32)
    tables = []
    for lg in RET_LOG_GAMMA:
        decay = jnp.where(causal, jnp.exp(jnp.where(causal, diff, 0.0) * lg), 0.0)
        q_decay = jnp.exp((idx + 1.0) * lg)
        k_decay = jnp.exp((L - 1.0 - idx) * lg)
        tables.append((decay, q_decay, k_decay, math.exp(L * lg)))
    return tables


def _retention_head(q, k, v, gate, S, cos, sin, table):
    decay, q_decay, k_decay, s_decay = table
    qr = _rotary(q, cos, sin)
    kr = _rotary(k, cos, sin) * (RET_HEAD_DIM ** -0.5)
    qb = qr.astype(BF16)
    vb = v.astype(BF16)
    scores = lax.dot_general(qb, kr.astype(BF16), (((1,), (1,)), ((), ())), preferred_element_type=F32)
    inner = _dot((scores * decay).astype(BF16), vb)
    cross = _dot(qb, S.astype(BF16)) * q_decay
    o = inner + cross
    kd = (kr * k_decay).astype(BF16)
    S_new = s_decay * S + lax.dot_general(kd, vb, (((0,), (0,)), ((), ())), preferred_element_type=F32)
    mu = jnp.mean(o, axis=-1, keepdims=True)
    c = o - mu
    var = jnp.mean(c * c, axis=-1, keepdims=True)
    y = c * lax.rsqrt(var + LN_EPS) * _silu(gate)
    return y, S_new


def _window_count(w, pos):
    return jnp.minimum(jnp.float32(w), pos + 1.0)


def _mixer_prompt_kernel(z_ref, rope_ref, convw_ref, poolw_ref, pscale_ref, x_ref, gb_ref, wout_ref,
                         o_ref, convn_ref, pooln_ref, retn_ref,
                         y_ref, s_ref, ubuf_ref, pbuf_ref, *, tl, nl):
    half = RET_HEAD_DIM // 2
    l = pl.program_id(1)

    @pl.when(l == 0)
    def _():
        s_ref[...] = jnp.zeros_like(s_ref)
        ubuf_ref[0:CONV_PAD, :] = jnp.zeros((CONV_PAD, CONV_WIDTH), F32)
        pbuf_ref[0:POOL_PAD, :] = jnp.zeros((POOL_PAD, POOL_WIDTH), F32)

    u = z_ref[:, COL_C:COL_C + CONV_WIDTH] * z_ref[:, COL_H:COL_H + CONV_WIDTH]
    ubuf_ref[CONV_PAD:CONV_PAD + tl, :] = u
    cw = convw_ref[...]
    conv = (ubuf_ref[CONV_PAD - 2:CONV_PAD - 2 + tl, :] * cw[0:1, :]
            + ubuf_ref[CONV_PAD - 1:CONV_PAD - 1 + tl, :] * cw[1:2, :]
            + u * cw[2:3, :])
    y_ref[:, 0:CONV_WIDTH] = (z_ref[:, COL_B:COL_B + CONV_WIDTH] * conv).astype(y_ref.dtype)

    pbuf_ref[POOL_PAD:POOL_PAD + tl, :] = z_ref[:, COL_P:COL_P + POOL_WIDTH]
    pos = (l * tl + lax.broadcasted_iota(jnp.int32, (tl, 1), 0)).astype(F32)
    for gi, w in enumerate(POOL_WINDOWS):
        c0 = gi * POOL_GROUP
        tok = pbuf_ref[POOL_PAD:POOL_PAD + tl, c0:c0 + POOL_GROUP]
        win = tok
        for j in range(1, w):
            win = win + pbuf_ref[POOL_PAD - j:POOL_PAD - j + tl, c0:c0 + POOL_GROUP]
        d = win / _window_count(w, pos) - tok
        yb = _dot(d.astype(BF16), poolw_ref[gi]) * pscale_ref[:, c0:c0 + POOL_GROUP]
        y_ref[:, OUT_POOL + c0:OUT_POOL + c0 + POOL_GROUP] = yb.astype(y_ref.dtype)

    tables = _retention_tables(RET_CHUNK)
    for c in range(tl // RET_CHUNK):
        r0 = c * RET_CHUNK
        cos = rope_ref[r0:r0 + RET_CHUNK, 0:half]
        sin = rope_ref[r0:r0 + RET_CHUNK, half:2 * half]
        for h in range(RET_HEADS):
            h0 = h * RET_HEAD_DIM
            y, s_new = _retention_head(
                z_ref[r0:r0 + RET_CHUNK, COL_Q + h0:COL_Q + h0 + RET_HEAD_DIM],
                z_ref[r0:r0 + RET_CHUNK, COL_K + h0:COL_K + h0 + RET_HEAD_DIM],
                z_ref[r0:r0 + RET_CHUNK, COL_V + h0:COL_V + h0 + RET_HEAD_DIM],
                z_ref[r0:r0 + RET_CHUNK, COL_G + h0:COL_G + h0 + RET_HEAD_DIM],
                s_ref[h], cos, sin, tables[h])
            s_ref[h] = s_new
            y_ref[r0:r0 + RET_CHUNK, OUT_RET + h0:OUT_RET + h0 + RET_HEAD_DIM] = y.astype(y_ref.dtype)
        yc = y_ref[r0:r0 + RET_CHUNK, :]
        m = jnp.concatenate([_dot(yc, wout_ref[j]) for j in range(WOUT_BLOCKS)], axis=1)
        o_ref[r0:r0 + RET_CHUNK, :] = _layer_norm(
            DEEPNORM_ALPHA * x_ref[r0:r0 + RET_CHUNK, :] + m, gb_ref[0:1, :], gb_ref[1:2, :])

    @pl.when(l == nl - 1)
    def _():
        convn_ref[0] = ubuf_ref[CONV_PAD + tl - (CONV_K - 1):CONV_PAD + tl, :]
        pooln_ref[0] = pbuf_ref[POOL_PAD + tl - POOL_HIST:POOL_PAD + tl, :]
        retn_ref[0] = s_ref[...]

    ubuf_ref[0:CONV_PAD, :] = ubuf_ref[tl:tl + CONV_PAD, :]
    pbuf_ref[0:POOL_PAD, :] = pbuf_ref[tl:tl + POOL_PAD, :]


def _mixer_prompt(z, x1, rope, conv_w, pool_w, pool_scale, w_out, ln_gb, layer, batch, tl=256):
    T = z.shape[0]
    L = T // batch
    nl = L // tl
    row = lambda b, l: (b * nl + l, 0)
    lay3 = lambda b, l: (layer, 0, 0)
    return pl.pallas_call(
        functools.partial(_mixer_prompt_kernel, tl=tl, nl=nl),
        grid=(batch, nl),
        in_specs=[
            pl.BlockSpec((tl, IN_COLS), row),
            pl.BlockSpec((tl, RET_HEAD_DIM), lambda b, l: (l, 0)),
            pl.BlockSpec((None, CONV_K, CONV_WIDTH), lay3),
            pl.BlockSpec((None, len(POOL_WINDOWS), POOL_GROUP, POOL_GROUP), lambda b, l: (layer, 0, 0, 0)),
            pl.BlockSpec((None, 1, POOL_WIDTH), lay3),
            pl.BlockSpec((tl, D_MODEL), row),
            pl.BlockSpec((None, 2, D_MODEL), lay3),
            pl.BlockSpec((None, WOUT_BLOCKS, D_MODEL, D_MODEL // WOUT_BLOCKS), lambda b, l: (layer, 0, 0, 0),
                         pipeline_mode=pl.Buffered(1)),
        ],
        out_specs=[
            pl.BlockSpec((tl, D_MODEL), row),
            pl.BlockSpec((1, CONV_K - 1, CONV_WIDTH), lambda b, l: (b, 0, 0)),
            pl.BlockSpec((1, POOL_HIST, POOL_WIDTH), lambda b, l: (b, 0, 0)),
            pl.BlockSpec((1, RET_HEADS, RET_HEAD_DIM, RET_HEAD_DIM), lambda b, l: (b, 0, 0, 0)),
        ],
        out_shape=[
            jax.ShapeDtypeStruct((T, D_MODEL), F32),
            jax.ShapeDtypeStruct((batch, CONV_K - 1, CONV_WIDTH), F32),
            jax.ShapeDtypeStruct((batch, POOL_HIST, POOL_WIDTH), F32),
            jax.ShapeDtypeStruct((batch, RET_HEADS, RET_HEAD_DIM, RET_HEAD_DIM), F32),
        ],
        scratch_shapes=[
            pltpu.VMEM((tl, D_MODEL), BF16),
            pltpu.VMEM((RET_HEADS, RET_HEAD_DIM, RET_HEAD_DIM), F32),
            pltpu.VMEM((CONV_PAD + tl, CONV_WIDTH), F32),
            pltpu.VMEM((POOL_PAD + tl, POOL_WIDTH), F32),
        ],
        compiler_params=_params(("parallel", "arbitrary"), 48),
        name="mixer_prompt",
    )(z, rope, conv_w, pool_w, pool_scale, x1, ln_gb, w_out)


def _mixer_sample_kernel(z_ref, rope_ref, convw_ref, poolw_ref, pscale_ref,
                         convc_ref, poolc_ref, state_ref, x_ref, gb_ref, wout_ref, *rest,
                         nb, ls, start_pos, aliased):
    half = RET_HEAD_DIM // 2
    if aliased:
        rest = rest[1:]
    o_ref, convn_ref, pooln_ref, staten_ref, y_ref, ubuf_ref, pbuf_ref = rest
    u = z_ref[:, :, COL_C:COL_C + CONV_WIDTH] * z_ref[:, :, COL_H:COL_H + CONV_WIDTH]
    ubuf_ref[:, CONV_PAD - (CONV_K - 1):CONV_PAD, :] = convc_ref[...]
    ubuf_ref[:, CONV_PAD:CONV_PAD + ls, :] = u
    cw = convw_ref[...]
    conv = (ubuf_ref[:, CONV_PAD - 2:CONV_PAD - 2 + ls, :] * cw[0:1, :]
            + ubuf_ref[:, CONV_PAD - 1:CONV_PAD - 1 + ls, :] * cw[1:2, :]
            + u * cw[2:3, :])
    odd = lax.rem(pl.program_id(0), 2)
    base = odd * nb
    y_ref[pl.ds(base, nb), :, 0:CONV_WIDTH] = z_ref[:, :, COL_B:COL_B + CONV_WIDTH] * conv
    convn_ref[...] = ubuf_ref[:, CONV_PAD + ls - (CONV_K - 1):CONV_PAD + ls, :]

    pbuf_ref[:, POOL_PAD - POOL_HIST:POOL_PAD, :] = poolc_ref[...]
    pbuf_ref[:, POOL_PAD:POOL_PAD + ls, :] = z_ref[:, :, COL_P:COL_P + POOL_WIDTH]
    pos = (start_pos + lax.broadcasted_iota(jnp.int32, (1, ls, 1), 1)).astype(F32)
    for gi, w in enumerate(POOL_WINDOWS):
        c0 = gi * POOL_GROUP
        tok = pbuf_ref[:, POOL_PAD:POOL_PAD + ls, c0:c0 + POOL_GROUP]
        win = tok
        for j in range(1, w):
            win = win + pbuf_ref[:, POOL_PAD - j:POOL_PAD - j + ls, c0:c0 + POOL_GROUP]
        d = win / _window_count(w, pos) - tok
        yb = _dot(d.reshape(nb * ls, POOL_GROUP).astype(BF16), poolw_ref[gi]).reshape(nb, ls, POOL_GROUP)
        y_ref[pl.ds(base, nb), :, OUT_POOL + c0:OUT_POOL + c0 + POOL_GROUP] = yb * pscale_ref[:, c0:c0 + POOL_GROUP]
    pooln_ref[...] = pbuf_ref[:, POOL_PAD + ls - POOL_HIST:POOL_PAD + ls, :]

    tables = _retention_tables(ls)
    cos = rope_ref[:, 0:half]
    sin = rope_ref[:, half:2 * half]

    def per_sequence(b, carry):
        for h in range(RET_HEADS):
            h0 = h * RET_HEAD_DIM
            y, s_new = _retention_head(
                z_ref[b, :, COL_Q + h0:COL_Q + h0 + RET_HEAD_DIM],
                z_ref[b, :, COL_K + h0:COL_K + h0 + RET_HEAD_DIM],
                z_ref[b, :, COL_V + h0:COL_V + h0 + RET_HEAD_DIM],
                z_ref[b, :, COL_G + h0:COL_G + h0 + RET_HEAD_DIM],
                state_ref[b, h], cos, sin, tables[h])
            staten_ref[b, h] = s_new
            y_ref[base + b, :, OUT_RET + h0:OUT_RET + h0 + RET_HEAD_DIM] = y
        return carry

    lax.fori_loop(0, nb, per_sequence, 0, unroll=4)

    @pl.when(odd == 1)
    def _():
        rows = 2 * nb * ls
        yv = y_ref[...].reshape(rows, D_MODEL).astype(BF16)
        m = jnp.concatenate([_dot(yv, wout_ref[j]) for j in range(WOUT_BLOCKS)], axis=1)
        x2 = _layer_norm(DEEPNORM_ALPHA * x_ref[...].reshape(rows, D_MODEL) + m, gb_ref[0:1, :], gb_ref[1:2, :])
        o_ref[...] = x2.reshape(2 * nb, ls, D_MODEL)


def _mixer_sample(z, x1, rope, conv_w, pool_w, pool_scale, conv_cache, pool_cache, state, state_out,
                  w_out, ln_gb, layer, start_pos, nb=8):
    B, ls, _ = z.shape
    seq3 = lambda i: (i, 0, 0)
    lay3 = lambda i: (layer, 0, 0)
    lseq4 = lambda i: (layer, i, 0, 0)
    lseq5 = lambda i: (layer, i, 0, 0, 0)
    in_specs = [
        pl.BlockSpec((nb, ls, IN_COLS), seq3),
        pl.BlockSpec((ls, RET_HEAD_DIM), lambda i: (0, 0)),
        pl.BlockSpec((None, CONV_K, CONV_WIDTH), lay3),
        pl.BlockSpec((None, len(POOL_WINDOWS), POOL_GROUP, POOL_GROUP), lambda i: (layer, 0, 0, 0)),
        pl.BlockSpec((None, 1, POOL_WIDTH), lay3),
        pl.BlockSpec((None, nb, CONV_K - 1, CONV_WIDTH), lseq4),
        pl.BlockSpec((None, nb, POOL_HIST, POOL_WIDTH), lseq4),
        pl.BlockSpec((None, nb, RET_HEADS, RET_HEAD_DIM, RET_HEAD_DIM), lseq5),
        pl.BlockSpec((2 * nb, ls, D_MODEL), lambda i: (i // 2, 0, 0)),
        pl.BlockSpec((None, 2, D_MODEL), lay3),
        pl.BlockSpec((None, WOUT_BLOCKS, D_MODEL, D_MODEL // WOUT_BLOCKS), lambda i: (layer, 0, 0, 0),
                     pipeline_mode=pl.Buffered(1)),
    ]
    args = [z, rope, conv_w, pool_w, pool_scale, conv_cache, pool_cache, state, x1, ln_gb, w_out]
    aliases = {}
    if state_out is not None:
        in_specs.append(pl.BlockSpec(memory_space=pl.ANY))
        args.append(state_out)
        aliases = {len(args) - 1: 3}
    return pl.pallas_call(
        functools.partial(_mixer_sample_kernel, nb=nb, ls=ls, start_pos=start_pos,
                          aliased=state_out is not None),
        grid=(B // nb,),
        in_specs=in_specs,
        out_specs=[
            pl.BlockSpec((2 * nb, ls, D_MODEL), lambda i: (i // 2, 0, 0)),
            pl.BlockSpec((nb, CONV_K - 1, CONV_WIDTH), seq3),
            pl.BlockSpec((nb, POOL_HIST, POOL_WIDTH), seq3),
            pl.BlockSpec((None, nb, RET_HEADS, RET_HEAD_DIM, RET_HEAD_DIM), lseq5),
        ],
        out_shape=[
            jax.ShapeDtypeStruct((B, ls, D_MODEL), F32),
            jax.ShapeDtypeStruct((B, CONV_K - 1, CONV_WIDTH), F32),
            jax.ShapeDtypeStruct((B, POOL_HIST, POOL_WIDTH), F32),
            jax.ShapeDtypeStruct(state.shape, F32),
        ],
        scratch_shapes=[
            pltpu.VMEM((2 * nb, ls, D_MODEL), F32),
            pltpu.VMEM((nb, CONV_PAD + ls, CONV_WIDTH), F32),
            pltpu.VMEM((nb, POOL_PAD + ls, POOL_WIDTH), F32),
        ],
        input_output_aliases=aliases,
        compiler_params=_params(("arbitrary",), 56),
        name="mixer_sample",
    )(*args)


def _rope_tables(start_pos, length):
    half = RET_HEAD_DIM // 2
    inv = ROPE_BASE ** (-jnp.arange(half, dtype=F32) / half)
    pos = start_pos + jnp.arange(length, dtype=F32)
    ang = pos[:, None] * inv[None, :]
    return jnp.concatenate([jnp.cos(ang), jnp.sin(ang)], axis=1)


def _trunk(x, p, caches, start_pos, w, packed):
    B, L, D = x.shape
    T = B * L
    x = x.reshape(T, D)
    p = p.reshape(DEPTH, T, -1)
    rope = _rope_tables(start_pos, L)
    convs, pools, rets = [], [], []
    state_out = None
    collect = not packed

    def ffn(x, name, ln, layer, with_ple=False):
        if collect:
            ple = (w["ple_gate"], p, w["ple_proj"]) if with_ple else None
            out, *copies = _ffn_ln(x, (w[name + "_w_gate"], w[name + "_w_up"], w[name + "_w_down"]),
                                   w[ln], layer, ple=ple)
            packed[name, layer] = tuple(copies[:2])
            if with_ple:
                packed["ple", layer] = tuple(copies[2:])
            return out
        ple = (packed["ple", layer][0], p, packed["ple", layer][1]) if with_ple else None
        return _ffn_ln(x, packed[name, layer], w[ln], layer, ple=ple)[0]

    for i in range(DEPTH):
        x1 = ffn(x, "ffn1", "ln1", i)
        if collect:
            z, packed["w_in", i] = _proj_in(x1, w["w_in"], i)
        else:
            z, = _proj_in(x1, packed["w_in", i], i)
        if caches is None:
            x2, c_new, p_new, r_new = _mixer_prompt(
                z, x1, rope, w["conv_w"], w["pool_w"], w["pool_scale"], w["w_out"], w["ln2"], i, B)
            rets.append(r_new)
        else:
            x2, c_new, p_new, state_out = _mixer_sample(
                z.reshape(B, L, IN_COLS), x1.reshape(B, L, D), rope, w["conv_w"], w["pool_w"],
                w["pool_scale"], caches[0], caches[1], caches[2], state_out, w["w_out"], w["ln2"], i, start_pos)
            x2 = x2.reshape(T, D)
        x = ffn(x2, "ffn2", "ln3", i, with_ple=True)
        convs.append(c_new)
        pools.append(p_new)
    ret = jnp.stack(rets) if caches is None else state_out
    return x.reshape(B, L, D), jnp.stack(convs), jnp.stack(pools), ret


def kernel(x_prompt, x_sample, p_prompt, p_sample, cache_conv, cache_pool, state_ret, ln1_g, ln1_b, ffn1_w_gate, ffn1_w_up, ffn1_w_down, w_in, conv_w, pool_w, pool_scale, w_out, ln2_g, ln2_b, ffn2_w_gate, ffn2_w_up, ffn2_w_down, ple_gate, ple_proj, ln3_g, ln3_b):
    w = dict(
        pool_w=_to_bf16(pool_w).reshape(pool_w.shape),
        w_out=_to_bf16(w_out, col_blocks=WOUT_BLOCKS),
        pool_scale=pool_scale.reshape(DEPTH, 1, -1),
        ln1=jnp.stack([ln1_g, ln1_b], axis=1), ln2=jnp.stack([ln2_g, ln2_b], axis=1),
        ln3=jnp.stack([ln3_g, ln3_b], axis=1))
    w.update(conv_w=conv_w, ffn1_w_gate=ffn1_w_gate, ffn1_w_up=ffn1_w_up, ffn1_w_down=ffn1_w_down,
             ffn2_w_gate=ffn2_w_gate, ffn2_w_up=ffn2_w_up, ffn2_w_down=ffn2_w_down,
             w_in=w_in, ple_gate=ple_gate, ple_proj=ple_proj)
    packed = {}
    y_sample, conv_s, pool_s, ret_s = _trunk(
        x_sample, p_sample, (cache_conv, cache_pool, state_ret), PAST_LEN, w, packed)
    y_prompt, conv_p, pool_p, ret_p = _trunk(x_prompt, p_prompt, None, 0, w, packed)
    return (y_prompt, y_sample, conv_p, pool_p, ret_p, conv_s, pool_s, ret_s)
```

```python
import functools
import math

import jax
import jax.numpy as jnp
from jax import lax
from jax.experimental import pallas as pl
from jax.experimental.pallas import tpu as pltpu

D_MODEL = 2048
DEPTH = 2
PAST_LEN = 16384
CONV_WIDTH = D_MODEL // 4
POOL_WIDTH = D_MODEL // 4
RET_WIDTH = D_MODEL // 2
CONV_K = 3
POOL_WINDOWS = (2, 4, 8, 16)
POOL_GROUP = POOL_WIDTH // len(POOL_WINDOWS)
POOL_HIST = max(POOL_WINDOWS) - 1
RET_HEADS = 4
RET_HEAD_DIM = RET_WIDTH // RET_HEADS
RET_CHUNK = 128
RET_LOG_GAMMA = tuple(math.log(1.0 - 2.0 ** (-5 - h)) for h in range(RET_HEADS))
ROPE_BASE = 10000.0
IN_COLS = 3 * CONV_WIDTH + POOL_WIDTH + 4 * RET_WIDTH
DEEPNORM_ALPHA = (2 * DEPTH) ** 0.25
LN_EPS = 1e-5

COL_B = 0
COL_C = CONV_WIDTH
COL_H = 2 * CONV_WIDTH
COL_P = 3 * CONV_WIDTH
COL_Q = COL_P + POOL_WIDTH
COL_K = COL_Q + RET_WIDTH
COL_V = COL_K + RET_WIDTH
COL_G = COL_V + RET_WIDTH
OUT_POOL = CONV_WIDTH
OUT_RET = CONV_WIDTH + POOL_WIDTH

SUBLANES = 8
CONV_PAD = SUBLANES
POOL_PAD = 16
WOUT_BLOCKS = 4
LN_ROWS = 32

F32 = jnp.float32
BF16 = jnp.bfloat16
MIB = 1024 * 1024


def _params(semantics, vmem_mib):
    return pltpu.CompilerParams(dimension_semantics=semantics, vmem_limit_bytes=vmem_mib * MIB)


def _layer_norm(r, g, b):
    mu = jnp.mean(r, axis=-1, keepdims=True)
    c = r - mu
    var = jnp.mean(c * c, axis=-1, keepdims=True)
    return c * lax.rsqrt(var + LN_EPS) * g + b


def _silu(x):
    return x * jax.nn.sigmoid(x)


def _dot(a, b):
    return jnp.dot(a, b, preferred_element_type=F32)


def _cast_kernel(x_ref, o_ref):
    cb, _, cw = o_ref.shape
    for j in range(cb):
        o_ref[j] = x_ref[:, j * cw:(j + 1) * cw].astype(o_ref.dtype)


def _to_bf16(w, col_blocks=1, block_bytes=6 * MIB):
    depth, C = w.shape[0], w.shape[-1]
    w = w.reshape(depth, -1, C)
    R = w.shape[1]
    tr = R
    while tr * C * 4 > block_bytes and tr % 2 == 0 and (tr // 2) % 16 == 0:
        tr //= 2
    cw = C // col_blocks
    return pl.pallas_call(
        _cast_kernel,
        grid=(depth, R // tr),
        in_specs=[pl.BlockSpec((None, tr, C), lambda d, r: (d, r, 0))],
        out_specs=pl.BlockSpec((None, col_blocks, tr, cw), lambda d, r: (d, 0, r, 0)),
        out_shape=jax.ShapeDtypeStruct((depth, col_blocks, R, cw), BF16),
        compiler_params=_params(("parallel", "parallel"), 40),
        name="to_bf16",
    )(w)


FFN_PACK = 256
FFN_ROWS = 512


def _ffn_ln_kernel(*refs, nf, tf, packed, ple_cols):
    refs = list(refs)
    x_ref = refs.pop(0)
    if packed:
        wgu_ref, wd_ref = refs[:2]
        refs = refs[2:]
    else:
        wg_ref, wu_ref, wd_ref = refs[:3]
        refs = refs[3:]
    gb_ref = refs.pop(0)
    if ple_cols:
        gate_ref, p_ref, proj_ref = refs[:3]
        refs = refs[3:]
    o_ref = refs.pop(0)
    if not packed:
        wgu_out_ref, wd_out_ref = refs[:2]
        refs = refs[2:]
        if ple_cols:
            gate_out_ref, proj_out_ref = refs[:2]
            refs = refs[2:]
    xb_ref = refs.pop(0)
    f = pl.program_id(1)
    n_ple = o_ref.shape[1] // ple_cols if ple_cols else 0

    @pl.when(f == 0)
    def _():
        x = x_ref[...]
        xb_ref[...] = x.astype(BF16)
        o_ref[...] = DEEPNORM_ALPHA * x

    def step(with_ple):
        if packed:
            wd = wd_ref[...]
        else:
            wg = wg_ref[...].astype(BF16)
            wu = wu_ref[...].astype(BF16)
            wd = wd_ref[...].astype(BF16)
            wgu_out_ref[...] = jnp.concatenate([wg, wu], axis=1)
            wd_out_ref[...] = wd
        for r0 in range(0, o_ref.shape[0], FFN_ROWS):
            xb = xb_ref[r0:r0 + FFN_ROWS, :]
            if packed:
                hs = []
                for s in range(tf // FFN_PACK):
                    c0 = 2 * FFN_PACK * s
                    gu = _dot(xb, wgu_ref[:, c0:c0 + 2 * FFN_PACK])
                    hs.append((_silu(gu[:, :FFN_PACK]) * gu[:, FFN_PACK:]).astype(BF16))
                h = jnp.concatenate(hs, axis=1)
            else:
                h = (_silu(_dot(xb, wg)) * _dot(xb, wu)).astype(BF16)
            o_ref[r0:r0 + FFN_ROWS, :] += 0.5 * _dot(h, wd)
        if with_ple:
            gate_w = gate_ref[...]
            proj_w = proj_ref[...]
            if not packed:
                gate_w = gate_w.astype(BF16)
                proj_w = proj_w.astype(BF16)
                gate_out_ref[...] = gate_w
                proj_out_ref[...] = proj_w
            gate = jax.nn.sigmoid(_dot(xb_ref[...], gate_w))
            cols = pl.ds(pl.multiple_of(f * ple_cols, ple_cols), ple_cols)
            o_ref[:, cols] += gate * _dot(p_ref[...].astype(BF16), proj_w)

    if n_ple:
        pl.when(f < n_ple)(functools.partial(step, True))
        pl.when(f >= n_ple)(functools.partial(step, False))
    else:
        step(False)

    @pl.when(f == nf - 1)
    def _():
        o_ref[...] = _layer_norm(o_ref[...], gb_ref[0:1, :], gb_ref[1:2, :])


def _ffn_ln(x, weights, ln_gb, layer, ple=None, tm=1024, packed_tf=512, ple_cols=512):
    T, D = x.shape
    packed = len(weights) == 2
    tf = packed_tf if packed else FFN_PACK
    F = weights[1].shape[0] if packed else weights[0].shape[2]
    nf = F // tf
    n_ple = D // ple_cols if ple is not None else 0
    row = lambda i, f: (i, 0)
    const = lambda i, f: (layer, 0, 0)
    if packed:
        w_specs = [pl.BlockSpec((D, 2 * tf), lambda i, f: (0, f)),
                   pl.BlockSpec((tf, D), lambda i, f: (f, 0))]
        single = None
    else:
        assert T == tm
        w_specs = [pl.BlockSpec((None, D, tf), lambda i, f: (layer, 0, f)),
                   pl.BlockSpec((None, D, tf), lambda i, f: (layer, 0, f)),
                   pl.BlockSpec((None, tf, D), lambda i, f: (layer, f, 0))]
        single = pl.Buffered(1)
    in_specs = [pl.BlockSpec((tm, D), row, pipeline_mode=single)] + w_specs + [
        pl.BlockSpec((None, 2, D), const),
    ]
    args = [x, *weights, ln_gb]
    if ple is not None:
        gate_w, p, proj_w = ple
        P = p.shape[2]
        assert n_ple <= nf
        pcol = lambda i, f: (0, jnp.minimum(f, n_ple - 1))
        lpcol = lambda i, f: (layer, 0, jnp.minimum(f, n_ple - 1))
        in_specs += [
            pl.BlockSpec((D, ple_cols), pcol) if packed else pl.BlockSpec((None, D, ple_cols), lpcol),
            pl.BlockSpec((None, tm, P), lambda i, f: (layer, i, 0)),
            pl.BlockSpec((P, ple_cols), pcol) if packed else pl.BlockSpec((None, P, ple_cols), lpcol),
        ]
        args += [gate_w, p, proj_w]
    out_shape = [jax.ShapeDtypeStruct((T, D), F32)]
    out_specs = [pl.BlockSpec((tm, D), row, pipeline_mode=single)]
    if not packed:
        out_shape += [jax.ShapeDtypeStruct((D, 2 * F), BF16), jax.ShapeDtypeStruct((F, D), BF16)]
        out_specs += [pl.BlockSpec((D, 2 * tf), lambda i, f: (0, f)),
                      pl.BlockSpec((tf, D), lambda i, f: (f, 0))]
        if ple is not None:
            out_shape += [jax.ShapeDtypeStruct((D, D), BF16), jax.ShapeDtypeStruct((P, D), BF16)]
            out_specs += [pl.BlockSpec((D, ple_cols), pcol), pl.BlockSpec((P, ple_cols), pcol)]
    return pl.pallas_call(
        functools.partial(_ffn_ln_kernel, nf=nf, tf=tf, packed=packed,
                          ple_cols=ple_cols if ple is not None else 0),
        grid=(T // tm, nf),
        in_specs=in_specs,
        out_specs=out_specs,
        out_shape=out_shape,
        scratch_shapes=[pltpu.VMEM((tm, D), BF16)],
        compiler_params=_params(("parallel", "arbitrary"), 62),
        name="ffn_ln",
    )(*args)


def _proj_in_kernel(x_ref, w_ref, o_ref, *rest, emit):
    xb_ref = rest[-1]

    @pl.when(pl.program_id(1) == 0)
    def _():
        xb_ref[...] = x_ref[...].astype(BF16)

    w = w_ref[...]
    if emit:
        w = w.astype(BF16)
        rest[0][...] = w
    o_ref[...] = _dot(xb_ref[...], w)


def _proj_in(x, w, layer, tm=1024, tn=None):
    T, D = x.shape
    emit = w.ndim == 3
    N = w.shape[-1]
    if emit:
        assert T == tm
        tn = tn or 1024
        w_spec = pl.BlockSpec((None, D, tn), lambda i, n: (layer, 0, n))
    else:
        tn = tn or 2048
        w_spec = pl.BlockSpec((D, tn), lambda i, n: (0, n))
    out_shape = [jax.ShapeDtypeStruct((T, N), F32)]
    out_specs = [pl.BlockSpec((tm, tn), lambda i, n: (i, n))]
    if emit:
        out_shape.append(jax.ShapeDtypeStruct((D, N), BF16))
        out_specs.append(pl.BlockSpec((D, tn), lambda i, n: (0, n)))
    return pl.pallas_call(
        functools.partial(_proj_in_kernel, emit=emit),
        grid=(T // tm, N // tn),
        in_specs=[pl.BlockSpec((tm, D), lambda i, n: (i, 0)), w_spec],
        out_specs=out_specs,
        out_shape=out_shape,
        scratch_shapes=[pltpu.VMEM((tm, D), BF16)],
        compiler_params=_params(("parallel", "arbitrary"), 58),
        name="proj_in",
    )(x, w)


def _rotary(x, cos, sin):
    half = RET_HEAD_DIM // 2
    x1, x2 = x[:, :half], x[:, half:]
    return jnp.concatenate([x1 * cos - x2 * sin, x1 * sin + x2 * cos], axis=-1)


def _retention_table(L, lg):
    row = lax.broadcasted_iota(jnp.int32, (L, L), 0)
    col = lax.broadcasted_iota(jnp.int32, (L, L), 1)
    diff = (row - col).astype(F32)
    causal = row >= col
    idx = lax.broadcasted_iota(jnp.int32, (L, 1), 0).astype(F32)
    decay = jnp.where(causal, jnp.exp(jnp.where(causal, diff, 0.0) * lg), 0.0)
    q_decay = jnp.exp((idx + 1.0) * lg)
    k_decay = jnp.exp((L - 1.0 - idx) * lg)
    return decay, q_decay, k_decay, math.exp(L * lg)


def _retention_head(q, k, v, gate, S, cos, sin, table):
    decay, q_decay, k_decay, s_decay = table
    qr = _rotary(q, cos, sin)
    kr = _rotary(k, cos, sin) * (RET_HEAD_DIM ** -0.5)
    qb = qr.astype(BF16)
    vb = v.astype(BF16)
    scores = lax.dot_general(qb, kr.astype(BF16), (((1,), (1,)), ((), ())), preferred_element_type=F32)
    inner = _dot((scores * decay).astype(BF16), vb)
    cross = _dot(qb, S.astype(BF16)) * q_decay
    o = inner + cross
    kd = (kr * k_decay).astype(BF16)
    S_new = s_decay * S + lax.dot_general(kd, vb, (((0,), (0,)), ((), ())), preferred_element_type=F32)
    mu = jnp.mean(o, axis=-1, keepdims=True)
    c = o - mu
    var = jnp.mean(c * c, axis=-1, keepdims=True)
    y = c * lax.rsqrt(var + LN_EPS) * _silu(gate)
    return y, S_new


def _window_count(w, pos):
    return jnp.minimum(jnp.float32(w), pos + 1.0)


def _mixer_prompt_kernel(z_ref, rope_ref, convw_ref, poolw_ref, pscale_ref, x_ref, gb_ref, wout_ref,
                         o_ref, convn_ref, pooln_ref, retn_ref,
                         y_ref, s_ref, ubuf_ref, pbuf_ref, *, tl, nl):
    half = RET_HEAD_DIM // 2
    l = pl.program_id(1)

    @pl.when(l == 0)
    def _():
        s_ref[...] = jnp.zeros_like(s_ref)
        ubuf_ref[0:CONV_PAD, :] = jnp.zeros((CONV_PAD, CONV_WIDTH), F32)
        pbuf_ref[0:POOL_PAD, :] = jnp.zeros((POOL_PAD, POOL_WIDTH), F32)

    u = z_ref[:, COL_C:COL_C + CONV_WIDTH] * z_ref[:, COL_H:COL_H + CONV_WIDTH]
    ubuf_ref[CONV_PAD:CONV_PAD + tl, :] = u
    cw = convw_ref[...]
    conv = (ubuf_ref[CONV_PAD - 2:CONV_PAD - 2 + tl, :] * cw[0:1, :]
            + ubuf_ref[CONV_PAD - 1:CONV_PAD - 1 + tl, :] * cw[1:2, :]
            + u * cw[2:3, :])
    y_ref[:, 0:CONV_WIDTH] = (z_ref[:, COL_B:COL_B + CONV_WIDTH] * conv).astype(y_ref.dtype)

    pbuf_ref[POOL_PAD:POOL_PAD + tl, :] = z_ref[:, COL_P:COL_P + POOL_WIDTH]
    pos = (l * tl + lax.broadcasted_iota(jnp.int32, (tl, 1), 0)).astype(F32)
    for gi, w in enumerate(POOL_WINDOWS):
        c0 = gi * POOL_GROUP
        tok = pbuf_ref[POOL_PAD:POOL_PAD + tl, c0:c0 + POOL_GROUP]
        win = tok
        for j in range(1, w):
            win = win + pbuf_ref[POOL_PAD - j:POOL_PAD - j + tl, c0:c0 + POOL_GROUP]
        d = win / _window_count(w, pos) - tok
        yb = _dot(d.astype(BF16), poolw_ref[gi]) * pscale_ref[:, c0:c0 + POOL_GROUP]
        y_ref[:, OUT_POOL + c0:OUT_POOL + c0 + POOL_GROUP] = yb.astype(y_ref.dtype)

    for c in range(tl // RET_CHUNK):
        r0 = c * RET_CHUNK
        cos = rope_ref[r0:r0 + RET_CHUNK, 0:half]
        sin = rope_ref[r0:r0 + RET_CHUNK, half:2 * half]
        for h in range(RET_HEADS):
            h0 = h * RET_HEAD_DIM
            y, s_new = _retention_head(
                z_ref[r0:r0 + RET_CHUNK, COL_Q + h0:COL_Q + h0 + RET_HEAD_DIM],
                z_ref[r0:r0 + RET_CHUNK, COL_K + h0:COL_K + h0 + RET_HEAD_DIM],
                z_ref[r0:r0 + RET_CHUNK, COL_V + h0:COL_V + h0 + RET_HEAD_DIM],
                z_ref[r0:r0 + RET_CHUNK, COL_G + h0:COL_G + h0 + RET_HEAD_DIM],
                s_ref[h], cos, sin, _retention_table(RET_CHUNK, RET_LOG_GAMMA[h]))
            s_ref[h] = s_new
            y_ref[r0:r0 + RET_CHUNK, OUT_RET + h0:OUT_RET + h0 + RET_HEAD_DIM] = y.astype(y_ref.dtype)
        yc = y_ref[r0:r0 + RET_CHUNK, :]
        for j in range(WOUT_BLOCKS):
            cols = slice(j * (D_MODEL // WOUT_BLOCKS), (j + 1) * (D_MODEL // WOUT_BLOCKS))
            o_ref[r0:r0 + RET_CHUNK, cols] = DEEPNORM_ALPHA * x_ref[r0:r0 + RET_CHUNK, cols] + _dot(yc, wout_ref[j])
        for r1 in range(r0, r0 + RET_CHUNK, LN_ROWS):
            o_ref[r1:r1 + LN_ROWS, :] = _layer_norm(o_ref[r1:r1 + LN_ROWS, :], gb_ref[0:1, :], gb_ref[1:2, :])

    @pl.when(l == nl - 1)
    def _():
        convn_ref[0] = ubuf_ref[CONV_PAD + tl - (CONV_K - 1):CONV_PAD + tl, :]
        pooln_ref[0] = pbuf_ref[POOL_PAD + tl - POOL_HIST:POOL_PAD + tl, :]
        retn_ref[0] = s_ref[...]

    ubuf_ref[0:CONV_PAD, :] = ubuf_ref[tl:tl + CONV_PAD, :]
    pbuf_ref[0:POOL_PAD, :] = pbuf_ref[tl:tl + POOL_PAD, :]


def _mixer_prompt(z, x1, rope, conv_w, pool_w, pool_scale, w_out, ln_gb, layer, batch, tl=256):
    T = z.shape[0]
    L = T // batch
    nl = L // tl
    row = lambda b, l: (b * nl + l, 0)
    lay3 = lambda b, l: (layer, 0, 0)
    return pl.pallas_call(
        functools.partial(_mixer_prompt_kernel, tl=tl, nl=nl),
        grid=(batch, nl),
        in_specs=[
            pl.BlockSpec((tl, IN_COLS), row),
            pl.BlockSpec((tl, RET_HEAD_DIM), lambda b, l: (l, 0)),
            pl.BlockSpec((None, CONV_K, CONV_WIDTH), lay3),
            pl.BlockSpec((None, len(POOL_WINDOWS), POOL_GROUP, POOL_GROUP), lambda b, l: (layer, 0, 0, 0)),
            pl.BlockSpec((None, 1, POOL_WIDTH), lay3),
            pl.BlockSpec((tl, D_MODEL), row),
            pl.BlockSpec((None, 2, D_MODEL), lay3),
            pl.BlockSpec((None, WOUT_BLOCKS, D_MODEL, D_MODEL // WOUT_BLOCKS), lambda b, l: (layer, 0, 0, 0),
                         pipeline_mode=pl.Buffered(1)),
        ],
        out_specs=[
            pl.BlockSpec((tl, D_MODEL), row),
            pl.BlockSpec((1, CONV_K - 1, CONV_WIDTH), lambda b, l: (b, 0, 0)),
            pl.BlockSpec((1, POOL_HIST, POOL_WIDTH), lambda b, l: (b, 0, 0)),
            pl.BlockSpec((1, RET_HEADS, RET_HEAD_DIM, RET_HEAD_DIM), lambda b, l: (b, 0, 0, 0)),
        ],
        out_shape=[
            jax.ShapeDtypeStruct((T, D_MODEL), F32),
            jax.ShapeDtypeStruct((batch, CONV_K - 1, CONV_WIDTH), F32),
            jax.ShapeDtypeStruct((batch, POOL_HIST, POOL_WIDTH), F32),
            jax.ShapeDtypeStruct((batch, RET_HEADS, RET_HEAD_DIM, RET_HEAD_DIM), F32),
        ],
        scratch_shapes=[
            pltpu.VMEM((tl, D_MODEL), BF16),
            pltpu.VMEM((RET_HEADS, RET_HEAD_DIM, RET_HEAD_DIM), F32),
            pltpu.VMEM((CONV_PAD + tl, CONV_WIDTH), F32),
            pltpu.VMEM((POOL_PAD + tl, POOL_WIDTH), F32),
        ],
        compiler_params=_params(("parallel", "arbitrary"), 48),
        name="mixer_prompt",
    )(z, rope, conv_w, pool_w, pool_scale, x1, ln_gb, w_out)


def _mixer_sample_kernel(z_ref, rope_ref, convw_ref, poolw_ref, pscale_ref,
                         convc_ref, poolc_ref, state_ref, x_ref, gb_ref, wout_ref, *rest,
                         nb, ls, start_pos, aliased):
    half = RET_HEAD_DIM // 2
    if aliased:
        rest = rest[1:]
    o_ref, convn_ref, pooln_ref, staten_ref, y_ref, ubuf_ref, pbuf_ref = rest
    u = z_ref[:, :, COL_C:COL_C + CONV_WIDTH] * z_ref[:, :, COL_H:COL_H + CONV_WIDTH]
    ubuf_ref[:, CONV_PAD - (CONV_K - 1):CONV_PAD, :] = convc_ref[...]
    ubuf_ref[:, CONV_PAD:CONV_PAD + ls, :] = u
    cw = convw_ref[...]
    conv = (ubuf_ref[:, CONV_PAD - 2:CONV_PAD - 2 + ls, :] * cw[0:1, :]
            + ubuf_ref[:, CONV_PAD - 1:CONV_PAD - 1 + ls, :] * cw[1:2, :]
            + u * cw[2:3, :])
    y_ref[:, :, 0:CONV_WIDTH] = z_ref[:, :, COL_B:COL_B + CONV_WIDTH] * conv
    convn_ref[...] = ubuf_ref[:, CONV_PAD + ls - (CONV_K - 1):CONV_PAD + ls, :]

    pbuf_ref[:, POOL_PAD - POOL_HIST:POOL_PAD, :] = poolc_ref[...]
    pbuf_ref[:, POOL_PAD:POOL_PAD + ls, :] = z_ref[:, :, COL_P:COL_P + POOL_WIDTH]
    pos = (start_pos + lax.broadcasted_iota(jnp.int32, (1, ls, 1), 1)).astype(F32)
    for gi, w in enumerate(POOL_WINDOWS):
        c0 = gi * POOL_GROUP
        tok = pbuf_ref[:, POOL_PAD:POOL_PAD + ls, c0:c0 + POOL_GROUP]
        win = tok
        for j in range(1, w):
            win = win + pbuf_ref[:, POOL_PAD - j:POOL_PAD - j + ls, c0:c0 + POOL_GROUP]
        d = win / _window_count(w, pos) - tok
        yb = _dot(d.reshape(nb * ls, POOL_GROUP).astype(BF16), poolw_ref[gi]).reshape(nb, ls, POOL_GROUP)
        y_ref[:, :, OUT_POOL + c0:OUT_POOL + c0 + POOL_GROUP] = yb * pscale_ref[:, c0:c0 + POOL_GROUP]
    pooln_ref[...] = pbuf_ref[:, POOL_PAD + ls - POOL_HIST:POOL_PAD + ls, :]

    tables = [_retention_table(ls, lg) for lg in RET_LOG_GAMMA]
    cos = rope_ref[:, 0:half]
    sin = rope_ref[:, half:2 * half]

    def per_sequence(b, carry):
        for h in range(RET_HEADS):
            h0 = h * RET_HEAD_DIM
            y, s_new = _retention_head(
                z_ref[b, :, COL_Q + h0:COL_Q + h0 + RET_HEAD_DIM],
                z_ref[b, :, COL_K + h0:COL_K + h0 + RET_HEAD_DIM],
                z_ref[b, :, COL_V + h0:COL_V + h0 + RET_HEAD_DIM],
                z_ref[b, :, COL_G + h0:COL_G + h0 + RET_HEAD_DIM],
                state_ref[b, h], cos, sin, tables[h])
            staten_ref[b, h] = s_new
            y_ref[b, :, OUT_RET + h0:OUT_RET + h0 + RET_HEAD_DIM] = y
        return carry

    lax.fori_loop(0, nb, per_sequence, 0, unroll=4)

    yv = y_ref[...].reshape(nb * ls, D_MODEL).astype(BF16)
    m = jnp.concatenate([_dot(yv, wout_ref[j]) for j in range(WOUT_BLOCKS)], axis=1)
    x2 = _layer_norm(DEEPNORM_ALPHA * x_ref[...].reshape(nb * ls, D_MODEL) + m, gb_ref[0:1, :], gb_ref[1:2, :])
    o_ref[...] = x2.reshape(nb, ls, D_MODEL)


def _mixer_sample(z, x1, rope, conv_w, pool_w, pool_scale, conv_cache, pool_cache, state, state_out,
                  w_out, ln_gb, layer, start_pos, nb=8):
    B, ls, _ = z.shape
    seq3 = lambda i: (i, 0, 0)
    lay3 = lambda i: (layer, 0, 0)
    lseq4 = lambda i: (layer, i, 0, 0)
    lseq5 = lambda i: (layer, i, 0, 0, 0)
    in_specs = [
        pl.BlockSpec((nb, ls, IN_COLS), seq3),
        pl.BlockSpec((ls, RET_HEAD_DIM), lambda i: (0, 0)),
        pl.BlockSpec((None, CONV_K, CONV_WIDTH), lay3),
        pl.BlockSpec((None, len(POOL_WINDOWS), POOL_GROUP, POOL_GROUP), lambda i: (layer, 0, 0, 0)),
        pl.BlockSpec((None, 1, POOL_WIDTH), lay3),
        pl.BlockSpec((None, nb, CONV_K - 1, CONV_WIDTH), lseq4),
        pl.BlockSpec((None, nb, POOL_HIST, POOL_WIDTH), lseq4),
        pl.BlockSpec((None, nb, RET_HEADS, RET_HEAD_DIM, RET_HEAD_DIM), lseq5),
        pl.BlockSpec((nb, ls, D_MODEL), seq3),
        pl.BlockSpec((None, 2, D_MODEL), lay3),
        pl.BlockSpec((None, WOUT_BLOCKS, D_MODEL, D_MODEL // WOUT_BLOCKS), lambda i: (layer, 0, 0, 0),
                     pipeline_mode=pl.Buffered(1)),
    ]
    args = [z, rope, conv_w, pool_w, pool_scale, conv_cache, pool_cache, state, x1, ln_gb, w_out]
    aliases = {}
    if state_out is not None:
        in_specs.append(pl.BlockSpec(memory_space=pl.ANY))
        args.append(state_out)
        aliases = {len(args) - 1: 3}
    return pl.pallas_call(
        functools.partial(_mixer_sample_kernel, nb=nb, ls=ls, start_pos=start_pos,
                          aliased=state_out is not None),
        grid=(B // nb,),
        in_specs=in_specs,
        out_specs=[
            pl.BlockSpec((nb, ls, D_MODEL), seq3),
            pl.BlockSpec((nb, CONV_K - 1, CONV_WIDTH), seq3),
            pl.BlockSpec((nb, POOL_HIST, POOL_WIDTH), seq3),
            pl.BlockSpec((None, nb, RET_HEADS, RET_HEAD_DIM, RET_HEAD_DIM), lseq5),
        ],
        out_shape=[
            jax.ShapeDtypeStruct((B, ls, D_MODEL), F32),
            jax.ShapeDtypeStruct((B, CONV_K - 1, CONV_WIDTH), F32),
            jax.ShapeDtypeStruct((B, POOL_HIST, POOL_WIDTH), F32),
            jax.ShapeDtypeStruct(state.shape, F32),
        ],
        scratch_shapes=[
            pltpu.VMEM((nb, ls, D_MODEL), F32),
            pltpu.VMEM((nb, CONV_PAD + ls, CONV_WIDTH), F32),
            pltpu.VMEM((nb, POOL_PAD + ls, POOL_WIDTH), F32),
        ],
        input_output_aliases=aliases,
        compiler_params=_params(("parallel",), 48),
        name="mixer_sample",
    )(*args)


def _rope_tables(start_pos, length):
    half = RET_HEAD_DIM // 2
    inv = ROPE_BASE ** (-jnp.arange(half, dtype=F32) / half)
    pos = start_pos + jnp.arange(length, dtype=F32)
    ang = pos[:, None] * inv[None, :]
    return jnp.concatenate([jnp.cos(ang), jnp.sin(ang)], axis=1)


def _trunk(x, p, caches, start_pos, w, packed):
    B, L, D = x.shape
    T = B * L
    x = x.reshape(T, D)
    p = p.reshape(DEPTH, T, -1)
    rope = _rope_tables(start_pos, L)
    convs, pools, rets = [], [], []
    state_out = None
    collect = not packed

    def ffn(x, name, ln, layer, with_ple=False):
        if collect:
            ple = (w["ple_gate"], p, w["ple_proj"]) if with_ple else None
            out, *copies = _ffn_ln(x, (w[name + "_w_gate"], w[name + "_w_up"], w[name + "_w_down"]),
                                   w[ln], layer, ple=ple)
            packed[name, layer] = tuple(copies[:2])
            if with_ple:
                packed["ple", layer] = tuple(copies[2:])
            return out
        ple = (packed["ple", layer][0], p, packed["ple", layer][1]) if with_ple else None
        return _ffn_ln(x, packed[name, layer], w[ln], layer, ple=ple)[0]

    for i in range(DEPTH):
        x1 = ffn(x, "ffn1", "ln1", i)
        if collect:
            z, packed["w_in", i] = _proj_in(x1, w["w_in"], i)
        else:
            z, = _proj_in(x1, packed["w_in", i], i)
        if caches is None:
            x2, c_new, p_new, r_new = _mixer_prompt(
                z, x1, rope, w["conv_w"], w["pool_w"], w["pool_scale"], w["w_out"], w["ln2"], i, B)
            rets.append(r_new)
        else:
            x2, c_new, p_new, state_out = _mixer_sample(
                z.reshape(B, L, IN_COLS), x1.reshape(B, L, D), rope, w["conv_w"], w["pool_w"],
                w["pool_scale"], caches[0], caches[1], caches[2], state_out, w["w_out"], w["ln2"], i, start_pos)
            x2 = x2.reshape(T, D)
        x = ffn(x2, "ffn2", "ln3", i, with_ple=True)
        convs.append(c_new)
        pools.append(p_new)
    ret = jnp.stack(rets) if caches is None else state_out
    return x.reshape(B, L, D), jnp.stack(convs), jnp.stack(pools), ret


def kernel(x_prompt, x_sample, p_prompt, p_sample, cache_conv, cache_pool, state_ret, ln1_g, ln1_b, ffn1_w_gate, ffn1_w_up, ffn1_w_down, w_in, conv_w, pool_w, pool_scale, w_out, ln2_g, ln2_b, ffn2_w_gate, ffn2_w_up, ffn2_w_down, ple_gate, ple_proj, ln3_g, ln3_b):
    w = dict(
        pool_w=_to_bf16(pool_w).reshape(pool_w.shape),
        w_out=_to_bf16(w_out, col_blocks=WOUT_BLOCKS),
        pool_scale=pool_scale.reshape(DEPTH, 1, -1),
        ln1=jnp.stack([ln1_g, ln1_b], axis=1), ln2=jnp.stack([ln2_g, ln2_b], axis=1),
        ln3=jnp.stack([ln3_g, ln3_b], axis=1))
    w.update(conv_w=conv_w, ffn1_w_gate=ffn1_w_gate, ffn1_w_up=ffn1_w_up, ffn1_w_down=ffn1_w_down,
             ffn2_w_gate=ffn2_w_gate, ffn2_w_up=ffn2_w_up, ffn2_w_down=ffn2_w_down,
             w_in=w_in, ple_gate=ple_gate, ple_proj=ple_proj)
    packed = {}
    y_sample, conv_s, pool_s, ret_s = _trunk(
        x_sample, p_sample, (cache_conv, cache_pool, state_ret), PAST_LEN, w, packed)
    y_prompt, conv_p, pool_p, ret_p = _trunk(x_prompt, p_prompt, None, 0, w, packed)
    return (y_prompt, y_sample, conv_p, pool_p, ret_p, conv_s, pool_s, ret_s)
```
